```python
import math
import jax
import jax.numpy as jnp
from jax import lax
import numpy as np

D_MODEL = 1024
BATCH = 2
SEQ = 16384
DEPTH = 2
DEC_BATCH = 4
DEC_SEQ = 4096
PAST_LEN = 128

GRID_W = 64
D_MIX = D_MODEL
N_MIXERS = 4
GROUP_W = D_MIX // N_MIXERS
HEAD_DIM = 64
GROUP_HEADS = GROUP_W // HEAD_DIM
EPS = 1e-6
SGU_CHUNK = 128
DN_HEADS = GROUP_HEADS
DN_DK = HEAD_DIM
DN_DV = HEAD_DIM
DN_CHUNK = 64
CONV_K = 5
DN_CONV_CH = DN_HEADS * (2 * DN_DK + DN_DV)
ATT_HEADS = GROUP_HEADS
ATT_KV_HEADS = 2
ATT_QBLOCK = 128
ROPE_THETA = 10000.0
ROPE_AXIS_DIM = HEAD_DIM // 2
ROPE_FREQS = ROPE_AXIS_DIM // 2
POOL_WINDOWS = (2, 4, 8, 16)
POOL_GROUP = GROUP_W // len(POOL_WINDOWS)

IN_WIDTHS = (
    GROUP_W, GROUP_W, GROUP_W,
    DN_HEADS * DN_DK, DN_HEADS * DN_DK, DN_HEADS * DN_DV,
    DN_HEADS * DN_DV, 2 * DN_HEADS, 2 * DN_HEADS,
    ATT_HEADS * HEAD_DIM, ATT_KV_HEADS * HEAD_DIM,
    ATT_KV_HEADS * HEAD_DIM, ATT_HEADS * HEAD_DIM,
    GROUP_W, GROUP_W,
)
IN_COLS = sum(IN_WIDTHS)

kernel_name = "hybrid_parallel_group_encoder"


def _rms(x):
    xf = x.astype(jnp.float32)
    return (xf * lax.rsqrt(jnp.mean(xf * xf, axis=-1, keepdims=True) + EPS)).astype(x.dtype)


def _rms_norm(x, w):
    return _rms(x) * w


def _l2norm(x):
    xf = x.astype(jnp.float32)
    return xf * lax.rsqrt(jnp.sum(xf * xf, axis=-1, keepdims=True) + EPS)


def _split_cols(proj):
    offsets = []
    acc = 0
    for width in IN_WIDTHS[:-1]:
        acc += width
        offsets.append(acc)
    return jnp.split(proj, offsets, axis=-1)


def _sgu_mixer(u, v, sgu_w, sgu_b):
    b_, s_, _ = u.shape
    nc = s_ // SGU_CHUNK
    v_n = _rms(v.reshape(b_, nc, SGU_CHUNK, GROUP_HEADS, HEAD_DIM))
    mixed = jnp.einsum("hij,bnjhd->bnihd", sgu_w, v_n) + sgu_b.T[:, :, None]
    out = u.reshape(b_, nc, SGU_CHUNK, GROUP_HEADS, HEAD_DIM) * mixed
    return out.reshape(b_, s_, GROUP_W)


def _centred_depthwise_conv(x, w):
    pad = CONV_K // 2
    s_ = x.shape[1]
    xp = jnp.pad(x, ((0, 0), (pad, pad), (0, 0)))
    return sum(xp[:, i:i + s_] * w[i] for i in range(CONV_K))


def _chunk_gated_delta_rule(q, k, v, g, beta):
    b_, s_, h_, dk = q.shape
    dv = v.shape[-1]
    n = s_ // DN_CHUNK

    def to_chunks(t):
        t = t.astype(jnp.float32).reshape(b_, n, DN_CHUNK, h_, *t.shape[3:])
        return jnp.moveaxis(t, 3, 2)

    q, k, v, g, beta = (to_chunks(t) for t in (q, k, v, g, beta))
    g = jnp.cumsum(g, axis=-1)
    idx = jnp.arange(DN_CHUNK)
    incl = idx[:, None] >= idx[None, :]
    strict = idx[:, None] > idx[None, :]
    decay = jnp.exp(jnp.where(incl, g[..., :, None] - g[..., None, :], -jnp.inf))
    k_beta = k * beta[..., None]
    v_beta = v * beta[..., None]
    lower = jnp.where(strict, jnp.einsum("bnhid,bnhjd->bnhij", k_beta, k) * decay, 0.0)
    eye = jnp.eye(DN_CHUNK, dtype=jnp.float32)
    t_mat = lax.linalg.triangular_solve(lower + eye, jnp.broadcast_to(eye, lower.shape),
                                        left_side=True, lower=True, unit_diagonal=True)
    u = t_mat @ v_beta
    w = t_mat @ (k_beta * jnp.exp(g)[..., None])
    intra = jnp.where(incl, jnp.einsum("bnhid,bnhjd->bnhij", q, k) * decay, 0.0)
    g_last = g[..., -1]
    q_dec = q * jnp.exp(g)[..., None]
    k_dec = k * jnp.exp(g_last[..., None] - g)[..., None]

    def step(state, xs):
        w_c, u_c, q_c, k_c, a_c, gl_c = xs
        v_new = u_c - w_c @ state
        out = q_c @ state + a_c @ v_new
        state = state * jnp.exp(gl_c)[..., None, None] + jnp.einsum("bhcd,bhce->bhde", k_c, v_new)
        return state, out

    xs = tuple(jnp.moveaxis(t, 1, 0) for t in (w, u, q_dec, k_dec, intra, g_last))
    state0 = jnp.zeros((b_, h_, dk, dv), jnp.float32)
    _, out = lax.scan(step, state0, xs)
    return jnp.transpose(out, (1, 0, 3, 2, 4)).reshape(b_, s_, h_, dv)


def _deltanet_mixer(q, k, v, beta_raw, alpha_raw, conv_w, a_log, dt_bias, norm_w):
    b_, s_, _ = q.shape
    qkv = jax.nn.silu(_centred_depthwise_conv(jnp.concatenate([q, k, v], axis=-1), conv_w))
    q, k, v = jnp.split(qkv, [DN_HEADS * DN_DK, 2 * DN_HEADS * DN_DK], axis=-1)
    q = _l2norm(q.reshape(b_, s_, DN_HEADS, DN_DK)) * (DN_DK ** -0.5)
    k = _l2norm(k.reshape(b_, s_, DN_HEADS, DN_DK))
    v = v.reshape(b_, s_, DN_HEADS, DN_DV)
    beta = jax.nn.sigmoid(beta_raw.astype(jnp.float32)).reshape(b_, s_, 2, DN_HEADS)
    g = -jnp.exp(a_log) * jax.nn.softplus(alpha_raw.astype(jnp.float32).reshape(b_, s_, 2, DN_HEADS) + dt_bias)
    o_fwd = _chunk_gated_delta_rule(q, k, v, g[:, :, 0], beta[:, :, 0])
    flip = lambda t: jnp.flip(t, axis=1)
    o_bwd = flip(_chunk_gated_delta_rule(flip(q), flip(k), flip(v), flip(g[:, :, 1]), flip(beta[:, :, 1])))
    o = _rms_norm(o_fwd + o_bwd, norm_w)
    return o.reshape(b_, s_, DN_HEADS * DN_DV)


def _axial_rope_tables(seq_len):
    n_rows = seq_len // GRID_W
    rows = jnp.repeat(jnp.arange(n_rows), GRID_W)
    cols = jnp.tile(jnp.arange(GRID_W), n_rows)
    inv_freq = jnp.power(ROPE_THETA, -2.0 * jnp.arange(ROPE_FREQS, dtype=jnp.float32) / ROPE_AXIS_DIM)
    ang = jnp.stack([rows, cols], axis=-1).astype(jnp.float32)[:, :, None] * inv_freq
    return jnp.cos(ang), jnp.sin(ang)


def _apply_axial_rope(x, cos, sin):
    b_, s_, h_, d_ = x.shape
    xr = x.reshape(b_, s_, h_, 2, 2, ROPE_FREQS)
    x1, x2 = xr[..., 0, :], xr[..., 1, :]
    c = cos[None, :, None]
    s = sin[None, :, None]
    out = jnp.stack([x1 * c - x2 * s, x2 * c + x1 * s], axis=-2)
    return out.reshape(b_, s_, h_, d_).astype(x.dtype)


def _attention_mixer(q, k, v, q_norm_w, k_norm_w, cos, sin):
    b_, s_, _ = q.shape
    grp = ATT_HEADS // ATT_KV_HEADS
    q = _apply_axial_rope(_rms_norm(q.reshape(b_, s_, ATT_HEADS, HEAD_DIM), q_norm_w), cos, sin)
    k = _apply_axial_rope(_rms_norm(k.reshape(b_, s_, ATT_KV_HEADS, HEAD_DIM), k_norm_w), cos, sin)
    v = v.reshape(b_, s_, ATT_KV_HEADS, HEAD_DIM)
    nb = s_ // ATT_QBLOCK
    qb = q.reshape(b_, nb, ATT_QBLOCK, ATT_KV_HEADS, grp, HEAD_DIM).transpose(1, 0, 2, 3, 4, 5)
    scale = HEAD_DIM ** -0.5

    def one_block(q_blk):
        s = jnp.einsum("bqkgd,bskd->bkgqs", q_blk, k, preferred_element_type=jnp.float32) * scale
        p = jax.nn.softmax(s, axis=-1)
        return jnp.einsum("bkgqs,bskd->bqkgd", p.astype(v.dtype), v)

    o = lax.map(one_block, qb)
    return o.transpose(1, 0, 2, 3, 4, 5).reshape(b_, s_, ATT_HEADS * HEAD_DIM)


def _pool_mixer(x, pool_w, pool_scale):
    b_, s_, _ = x.shape
    ng = len(POOL_WINDOWS)
    xf = x.astype(jnp.float32).reshape(b_, s_, ng, POOL_GROUP)
    csum = jnp.concatenate([jnp.zeros((b_, 1, ng, POOL_GROUP), jnp.float32), jnp.cumsum(xf, axis=1)], axis=1)
    t = jnp.arange(s_)
    diffs = []
    for gi, win in enumerate(POOL_WINDOWS):
        lo = jnp.clip(t - win // 2, 0, s_)
        hi = jnp.clip(t + win // 2, 0, s_)
        c_g = csum[:, :, gi]
        mean = (jnp.take(c_g, hi, axis=1) - jnp.take(c_g, lo, axis=1)) / (hi - lo).astype(jnp.float32)[None, :, None]
        diffs.append(mean - xf[:, :, gi])
    d = jnp.stack(diffs, axis=2)
    y = jnp.einsum("bsgc,gcd->bsgd", d, pool_w).reshape(b_, s_, GROUP_W) * pool_scale
    return y


def _layer(x, cos, sin, norm_w, w_in, sgu_w, sgu_b, conv_w, a_log, dt_bias, dn_norm_w,
           q_norm_w, k_norm_w, pool_w, pool_scale, w_out):
    h = _rms_norm(x, norm_w)
    proj = h @ w_in
    (a_u, a_v, a_z, b_q, b_k, b_v, b_z, b_beta, b_alpha,
     c_q, c_k, c_v, c_z, d_x, d_z) = _split_cols(proj)
    y_a = _sgu_mixer(a_u, a_v, sgu_w, sgu_b) * jax.nn.silu(a_z)
    y_b = _deltanet_mixer(b_q, b_k, b_v, b_beta, b_alpha, conv_w, a_log, dt_bias, dn_norm_w) * jax.nn.silu(b_z)
    y_c = _attention_mixer(c_q, c_k, c_v, q_norm_w, k_norm_w, cos, sin) * jax.nn.silu(c_z)
    y_d = _pool_mixer(d_x, pool_w, pool_scale) * jax.nn.silu(d_z)
    mix = jnp.concatenate([y_a.astype(x.dtype), y_b.astype(x.dtype), y_c.astype(x.dtype), y_d.astype(x.dtype)], axis=-1)
    return (x + mix @ w_out).astype(x.dtype)


def setup_inputs(seed: int = 0) -> dict:
    key = jax.random.key(seed)
    ks = jax.random.split(key, 16)
    f32 = jnp.float32

    def nrm(k, shape, scale):
        return jax.random.normal(k, shape, f32) * scale

    def gain(k, shape):
        return 1.0 + 0.02 * jax.random.normal(k, shape, f32)

    dt = jnp.exp(jax.random.uniform(ks[8], (DEPTH, 2, DN_HEADS), f32, math.log(1e-3), math.log(1e-1)))
    return {
        "x_prompt": nrm(ks[0], (BATCH, SEQ, D_MODEL), 1.0),
        "x_sample": nrm(ks[1], (DEC_BATCH, DEC_SEQ, D_MODEL), 1.0),
        "norm_w": gain(ks[2], (DEPTH, D_MODEL)),
        "w_in": nrm(ks[3], (DEPTH, D_MODEL, IN_COLS), D_MODEL ** -0.5),
        "sgu_w": nrm(ks[4], (DEPTH, GROUP_HEADS, SGU_CHUNK, SGU_CHUNK), SGU_CHUNK ** -0.5),
        "sgu_b": gain(ks[5], (DEPTH, GROUP_HEADS, SGU_CHUNK)),
        "conv_w": nrm(ks[6], (DEPTH, CONV_K, DN_CONV_CH), CONV_K ** -0.5),
        "a_log": jnp.log(jax.random.uniform(ks[7], (DEPTH, 2, DN_HEADS), f32, 1.0, 16.0)),
        "dt_bias": dt + jnp.log(-jnp.expm1(-dt)),
        "dn_norm_w": gain(ks[9], (DEPTH, DN_DV)),
        "q_norm_w": gain(ks[10], (DEPTH, HEAD_DIM)),
        "k_norm_w": gain(ks[11], (DEPTH, HEAD_DIM)),
        "pool_w": nrm(ks[12], (DEPTH, len(POOL_WINDOWS), POOL_GROUP, POOL_GROUP), POOL_GROUP ** -0.5),
        "pool_scale": gain(ks[13], (DEPTH, GROUP_W)),
        "w_out": nrm(ks[14], (DEPTH, D_MIX, D_MODEL), D_MIX ** -0.5),
    }


def reference(x_prompt, x_sample, norm_w, w_in, sgu_w, sgu_b, conv_w, a_log, dt_bias, dn_norm_w,
              q_norm_w, k_norm_w, pool_w, pool_scale, w_out):
    cos_p, sin_p = _axial_rope_tables(x_prompt.shape[1])
    cos_s, sin_s = _axial_rope_tables(x_sample.shape[1])
    y_prompt = x_prompt
    y_sample = x_sample
    for l in range(DEPTH):
        layer_params = (norm_w[l], w_in[l], sgu_w[l], sgu_b[l], conv_w[l], a_log[l], dt_bias[l],
                        dn_norm_w[l], q_norm_w[l], k_norm_w[l], pool_w[l], pool_scale[l], w_out[l])
        y_prompt = _layer(y_prompt, cos_p, sin_p, *layer_params)
        y_sample = _layer(y_sample, cos_s, sin_s, *layer_params)
    return (y_prompt, y_sample)
```

```python
import functools
import math

import numpy as np
import jax
import jax.numpy as jnp
from jax import lax
from jax.experimental import pallas as pl
from jax.experimental.pallas import tpu as pltpu

F32 = jnp.float32
BF16 = jnp.bfloat16

D_MODEL = 1024
GROUP_W = 256
HEAD = 64
N_HEADS = 4
GRID_W = 64
EPS = 1e-6
SGU_CHUNK = 128
DN_CHUNK = 64
CONV_K = 5
ROPE_THETA = 10000.0
POOL_WINDOWS = (2, 4, 8, 16)
HALO = 16
NEG_BIG = -1e30

V7X_VMEM_LIMIT_BYTES = 56 * 1024 * 1024

_A_U, _A_V, _A_Z = 0, 256, 512
_B_Q, _B_K, _B_V, _B_Z, _B_BETA, _B_ALPHA = 768, 1024, 1280, 1536, 1792, 1800
_C_Q, _C_K, _C_V, _C_Z = 1808, 2064, 2192, 2320
_D_X, _D_Z = 2576, 2832
IN_COLS = 3088
MAIN_COLS = 2176


def _cparams(semantics):
    return pltpu.CompilerParams(dimension_semantics=semantics,
                                vmem_limit_bytes=V7X_VMEM_LIMIT_BYTES)


def _split2(x):
    hi = x.astype(BF16)
    lo = (x - hi.astype(F32)).astype(BF16)
    return hi, lo


def _split3(x):
    hi = x.astype(BF16)
    r = x - hi.astype(F32)
    mid = r.astype(BF16)
    lo = (r - mid.astype(F32)).astype(BF16)
    return hi, mid, lo


def _iota(shape, dim):
    return lax.broadcasted_iota(jnp.int32, shape, dim)


def _ones_where(cond):
    return jnp.where(cond, 1.0, 0.0).astype(BF16)


def _head_sum(x2):
    w = x2.shape[1]
    g = _ones_where(jnp.right_shift(_iota((2 * w, w), 0) & (w - 1), 6)
                    == jnp.right_shift(_iota((2 * w, w), 1), 6))
    hi, lo = _split2(x2)
    return jnp.dot(jnp.concatenate([hi, lo], axis=1), g, preferred_element_type=F32)


def _silu(z):
    return z / (1.0 + jnp.exp(-z))


def _bd_mask():
    return jnp.right_shift(_iota((256, 256), 0), 6) == jnp.right_shift(_iota((256, 256), 1), 6)


def _bd(x, mask01):
    return jnp.concatenate([x, x, x, x], axis=0) * mask01


def _proj_kernel(xp_ref, xc_ref, xn_ref, nw_ref, wh_ref, wm_ref, sguw_ref, sgub_ref, convw_ref,
                 dnrow_ref, qw_ref, kw_ref, rc_ref, rs_ref, poolw_ref, pools_ref,
                 ya_ref, yd_ref, dq_ref, dk_ref, dv_ref, bg_ref, gates_ref, aq_ref, akt_ref, av_ref,
                 hext_ref, *, tm, n_tiles, seq_len):
    i = pl.program_id(1)
    n_ext = tm + 2 * HALO
    nw = nw_ref[...]

    def norm(x):
        ms = jnp.mean(x * x, axis=-1, keepdims=True)
        return x * lax.rsqrt(ms + EPS) * nw

    hext_ref[0:HALO, :] = jnp.where(i > 0, norm(xp_ref[0]), 0.0).astype(BF16)
    hext_ref[HALO:HALO + tm, :] = norm(xc_ref[0]).astype(BF16)
    hext_ref[HALO + tm:n_ext, :] = jnp.where(i < n_tiles - 1, norm(xn_ref[0]), 0.0).astype(BF16)

    ph = jnp.dot(hext_ref[...], wh_ref[...], preferred_element_type=F32)
    pm = jnp.dot(hext_ref[HALO:HALO + tm, :], wm_ref[...], preferred_element_type=F32)

    def rows(x):
        return x[HALO:HALO + tm]

    a_u, a_v, a_z = pm[:, 0:256], pm[:, 256:512], pm[:, 512:768]
    vn = a_v * lax.rsqrt(_head_sum(a_v * a_v) * (1.0 / HEAD) + EPS)
    mask4 = _ones_where(jnp.right_shift(_iota((512, 256), 0), 7) == jnp.right_shift(_iota((512, 256), 1), 6))
    mixed = []
    for c in range(tm // SGU_CHUNK):
        vc = vn[c * SGU_CHUNK:(c + 1) * SGU_CHUNK].astype(BF16)
        bdv = jnp.concatenate([vc, vc, vc, vc], axis=0) * mask4
        mixed.append(jnp.dot(sguw_ref[...], bdv, preferred_element_type=F32) + sgub_ref[...])
    mixed = jnp.concatenate(mixed, axis=0)
    ya_ref[0] = (a_u * mixed * _silu(a_z)).astype(BF16)

    xd = ph[:, 768:1024]
    a1 = xd + pltpu.roll(xd, n_ext - 1, 0)
    a2 = a1 + pltpu.roll(a1, n_ext - 2, 0)
    a3 = a2 + pltpu.roll(a2, n_ext - 4, 0)
    a4 = a3 + pltpu.roll(a3, n_ext - 8, 0)
    w2 = rows(pltpu.roll(a1, 1, 0))
    w4 = rows(pltpu.roll(a2, 2, 0))
    w8 = rows(pltpu.roll(a3, 4, 0))
    w16 = rows(pltpu.roll(a4, 8, 0))
    grp = jnp.right_shift(_iota((tm, 256), 1), 6)
    half = jnp.left_shift(jnp.ones((tm, 256), jnp.int32), grp)
    t = i * tm + _iota((tm, 256), 0)
    cnt = (jnp.minimum(t + half, seq_len) - jnp.maximum(t - half, 0)).astype(F32)
    win = jnp.where(grp == 0, w2, jnp.where(grp == 1, w4, jnp.where(grp == 2, w8, w16)))
    diff = win / cnt - rows(xd)
    yd = jnp.dot(diff.astype(BF16), poolw_ref[...], preferred_element_type=F32) * pools_ref[...]
    yd_ref[0] = (yd * _silu(pm[:, 1792:2048])).astype(BF16)

    xb = ph[:, 0:768]
    cw = convw_ref[...]
    conv = (rows(pltpu.roll(xb, 2, 0)) * cw[0:1] + rows(pltpu.roll(xb, 1, 0)) * cw[1:2]
            + rows(xb) * cw[2:3] + rows(pltpu.roll(xb, n_ext - 1, 0)) * cw[3:4]
            + rows(pltpu.roll(xb, n_ext - 2, 0)) * cw[4:5])
    act = _silu(conv)
    bq, bk = act[:, 0:256], act[:, 256:512]
    dq_ref[0] = (bq * lax.rsqrt(_head_sum(bq * bq) + EPS) * (HEAD ** -0.5)).astype(BF16)
    dk_ref[0] = (bk * lax.rsqrt(_head_sum(bk * bk) + EPS)).astype(BF16)
    dv_ref[0] = act[:, 512:768].astype(BF16)
    ba = pm[:, 2048:2176]
    lane = _iota((tm, 128), 1)
    xa = ba + dnrow_ref[1:2, :]
    softplus = jnp.maximum(xa, 0.0) + jnp.log1p(jnp.exp(-jnp.abs(xa)))
    g = -jnp.exp(dnrow_ref[0:1, :]) * softplus
    bg_ref[0] = jnp.where(lane < 8, 1.0 / (1.0 + jnp.exp(-ba)), g)
    gates_ref[0] = jnp.concatenate([_silu(pm[:, 768:1024]), _silu(pm[:, 1536:1792])], axis=1).astype(BF16)

    rc, rs = rc_ref[...], rs_ref[...]
    first = (_iota((tm, 128), 1) & 16) == 0

    def rope(x):
        sw = jnp.where(first, pltpu.roll(x, 112, 1), pltpu.roll(x, 16, 1))
        return x * rc + sw * rs

    cq, ck = pm[:, 1024:1280], pm[:, 1280:1408]
    qn = cq * lax.rsqrt(_head_sum(cq * cq) * (1.0 / HEAD) + EPS) * qw_ref[...]
    qr = jnp.concatenate([rope(qn[:, 0:128]), rope(qn[:, 128:256])], axis=1) * (HEAD ** -0.5)
    aq_ref[0] = qr.astype(BF16)
    kn = ck * lax.rsqrt(_head_sum(ck * ck) * (1.0 / HEAD) + EPS) * kw_ref[...]
    kt = rope(kn).T
    akt_ref[0, 0, 0] = jnp.concatenate([kt[0:64], kt[0:64]], axis=0).astype(BF16)
    akt_ref[0, 1, 0] = jnp.concatenate([kt[64:128], kt[64:128]], axis=0).astype(BF16)
    av_ref[0] = pm[:, 1408:1536].astype(BF16)


def _proj_call(x, lw, rope_c, rope_s, tm):
    b, s, _ = x.shape
    n_tiles = s // tm
    hb = tm // HALO
    last_hb = s // HALO - 1
    full = lambda shape: pl.BlockSpec(shape, lambda bi, i: (0,) * len(shape))
    tok = lambda w: pl.BlockSpec((1, tm, w), lambda bi, i: (bi, i, 0))
    in_specs = [
        pl.BlockSpec((1, HALO, D_MODEL), lambda bi, i: (bi, jnp.maximum(i * hb - 1, 0), 0)),
        pl.BlockSpec((1, tm, D_MODEL), lambda bi, i: (bi, i, 0)),
        pl.BlockSpec((1, HALO, D_MODEL), lambda bi, i: (bi, jnp.minimum((i + 1) * hb, last_hb), 0)),
        full((1, D_MODEL)), full((D_MODEL, 1024)), full((D_MODEL, MAIN_COLS)),
        full((128, 512)), full((128, 256)), full((8, 768)), full((8, 128)),
        full((1, 256)), full((1, 128)),
        pl.BlockSpec((tm, 128), lambda bi, i: (i, 0)), pl.BlockSpec((tm, 128), lambda bi, i: (i, 0)),
        full((256, 256)), full((1, 256)),
    ]
    out_shape = [
        jax.ShapeDtypeStruct((b, s, 256), BF16),
        jax.ShapeDtypeStruct((b, s, 256), BF16),
        jax.ShapeDtypeStruct((b, s, 256), BF16),
        jax.ShapeDtypeStruct((b, s, 256), BF16),
        jax.ShapeDtypeStruct((b, s, 256), BF16),
        jax.ShapeDtypeStruct((b, s, 128), F32),
        jax.ShapeDtypeStruct((b, s, 512), BF16),
        jax.ShapeDtypeStruct((b, s, 256), BF16),
        jax.ShapeDtypeStruct((b, 2, n_tiles, 128, tm), BF16),
        jax.ShapeDtypeStruct((b, s, 128), BF16),
    ]
    out_specs = [tok(256), tok(256), tok(256), tok(256), tok(256), tok(128), tok(512), tok(256),
                 pl.BlockSpec((1, 2, 1, 128, tm), lambda bi, i: (bi, 0, i, 0, 0)), tok(128)]
    return pl.pallas_call(
        functools.partial(_proj_kernel, tm=tm, n_tiles=n_tiles, seq_len=s),
        grid=(b, n_tiles), in_specs=in_specs, out_specs=out_specs, out_shape=out_shape,
        scratch_shapes=[pltpu.VMEM((tm + 2 * HALO, D_MODEL), BF16)],
        compiler_params=_cparams(("parallel", "parallel")), name="proj",
    )(x, x, x, lw["norm_w"], lw["w_halo"], lw["w_main"], lw["sgu_w"], lw["sgu_b"], lw["conv_w"],
      lw["dn_row"], lw["q_norm_w"], lw["k_norm_w"], rope_c, rope_s, lw["pool_w"], lw["pool_scale"])


def _dn_pre_kernel(q_ref, k_ref, v_ref, bg_ref,
                   uf_ref, wf_ref, qdf_ref, inf_ref, kdtf_ref, eglf_ref,
                   ub_ref, wb_ref, qdb_ref, inb_ref, kdtb_ref, eglb_ref, *, ta):
    nc = ta // DN_CHUNK
    q = q_ref[0].astype(F32)
    k = k_ref[0].astype(F32)
    v = v_ref[0].astype(F32)
    bg = bg_ref[0]

    r = _iota((ta, ta), 0)
    c = _iota((ta, ta), 1)
    same = jnp.right_shift(r, 6) == jnp.right_shift(c, 6)
    m_lin = _ones_where(same & (c <= r))
    m_uin = _ones_where(same & (c >= r))
    m_ust = _ones_where(same & (c > r))
    m_lst = _ones_where(same & (c < r))
    m4 = jnp.concatenate([m_lin, m_uin, m_ust, m_lst], axis=0)
    cs = jnp.dot(jnp.concatenate([m4, m4, m4], axis=1), jnp.concatenate(_split3(bg), axis=0),
                 preferred_element_type=F32)
    lane = _iota((ta, 128), 1)
    nb1 = jnp.where(lane < 8, bg, jnp.where(lane < 12, cs[0:ta], cs[ta:2 * ta]))
    nb2 = jnp.where(lane < 12, cs[2 * ta:3 * ta], cs[3 * ta:4 * ta])
    src = _iota((384, 1024), 0) & 127
    e1 = _ones_where(src == jnp.right_shift(_iota((384, 1024), 1), 6))
    x1 = jnp.dot(jnp.concatenate(_split3(nb1), axis=1), e1, preferred_element_type=F32)
    src2 = _iota((384, 512), 0) & 127
    e2 = _ones_where(src2 == jnp.right_shift(_iota((384, 512), 1), 6) + 8)
    x2 = jnp.dot(jnp.concatenate(_split3(nb2), axis=1), e2, preferred_element_type=F32)

    li = _iota((ta, 256), 0) & 63
    lj = _iota((ta, 256), 1) & 63
    eyecat = li == lj
    eyef = jnp.where(eyecat, 1.0, 0.0)
    j3 = _ones_where(same)
    j3 = jnp.concatenate([j3, j3, j3], axis=1)
    bdm = _ones_where(_bd_mask())

    gram, qk = [], []
    for ci in range(nc):
        sl = slice(ci * DN_CHUNK, (ci + 1) * DN_CHUNK)
        kc = k_ref[0, sl, :]
        gq = lax.dot_general(jnp.concatenate([kc, q_ref[0, sl, :]], axis=0), _bd(kc, bdm),
                             (((1,), (1,)), ((), ())), preferred_element_type=F32)
        gram.append(gq[0:64])
        qk.append(gq[64:128])
    gram = jnp.concatenate(gram, axis=0)
    qk = jnp.concatenate(qk, axis=0)

    outs = ((uf_ref, wf_ref, qdf_ref, inf_ref, kdtf_ref, eglf_ref),
            (ub_ref, wb_ref, qdb_ref, inb_ref, kdtb_ref, eglb_ref))
    for d in range(2):
        u_ref, w_ref, qd_ref, in_ref, kdt_ref, egl_ref = outs[d]
        bx = x1[:, d * 256:(d + 1) * 256]
        gcx = x1[:, 512 + d * 256:768 + d * 256]
        dglx = x2[:, d * 256:(d + 1) * 256]
        rowf = jnp.dot(j3, jnp.concatenate(_split3(jnp.where(eyecat, gcx, 0.0)), axis=0),
                       preferred_element_type=F32)
        incl = (li >= lj) if d == 0 else (li <= lj)
        strict = (li > lj) if d == 0 else (li < lj)
        decay = jnp.exp(jnp.where(incl, gcx - rowf, NEG_BIG))
        eg = jnp.exp(gcx)
        a_all = jnp.where(strict, bx * gram * decay, 0.0)
        in_ref[0] = jnp.where(incl, qk * decay, 0.0).astype(BF16)
        qd_ref[0] = (q * eg).astype(BF16)
        kdt_ref[0] = (k * jnp.exp(dglx)).T.astype(BF16)
        vb = (v * bx).astype(BF16)
        kbg = (k * bx * eg).astype(BF16)
        for ci in range(nc):
            sl = slice(ci * DN_CHUNK, (ci + 1) * DN_CHUNK)
            a = a_all[sl]
            p = eyef[sl] - a
            ak = a
            for stage in range(5):
                akb = ak.astype(BF16)
                lhs = akb if stage == 0 else jnp.concatenate([akb, p.astype(BF16)], axis=0)
                res = jnp.dot(lhs, _bd(akb, bdm), preferred_element_type=F32)
                ak = res[0:64]
                if stage > 0:
                    p = p + res[64:128]
            p = p + jnp.dot(p.astype(BF16), _bd(ak.astype(BF16), bdm), preferred_element_type=F32)
            rhs = jnp.concatenate([_bd(vb[sl], bdm), _bd(kbg[sl], bdm)], axis=1)
            uw = jnp.dot(p.astype(BF16), rhs, preferred_element_type=F32)
            u_ref[0, sl, :] = uw[:, 0:256].astype(BF16)
            w_ref[0, sl, :] = uw[:, 256:512].astype(BF16)
            last = ci * DN_CHUNK + (DN_CHUNK - 1 if d == 0 else 0)
            egl_ref[0, ci * 8:(ci + 1) * 8, :] = jnp.broadcast_to(eg[last:last + 1, :], (8, 256))


def _dn_pre_call(dq, dk, dv, bg, ta):
    b, s, _ = dq.shape
    tok = lambda w: pl.BlockSpec((1, ta, w), lambda bi, i: (bi, i, 0))
    per_dir_shapes = [jax.ShapeDtypeStruct((b, s, 256), BF16)] * 4 + [
        jax.ShapeDtypeStruct((b, 256, s), BF16), jax.ShapeDtypeStruct((b, s // 8, 256), F32)]
    per_dir_specs = [tok(256)] * 4 + [pl.BlockSpec((1, 256, ta), lambda bi, i: (bi, 0, i)),
                                      pl.BlockSpec((1, ta // 8, 256), lambda bi, i: (bi, i, 0))]
    return pl.pallas_call(
        functools.partial(_dn_pre_kernel, ta=ta),
        grid=(b, s // ta), in_specs=[tok(256), tok(256), tok(256), tok(128)],
        out_specs=per_dir_specs * 2, out_shape=per_dir_shapes * 2,
        compiler_params=_cparams(("parallel", "parallel")), name="dn_pre",
    )(dq, dk, dv, bg)


def _dn_scan_kernel(uf_ref, wf_ref, qdf_ref, inf_ref, kdtf_ref, eglf_ref,
                    ub_ref, wb_ref, qdb_ref, inb_ref, kdtb_ref, eglb_ref,
                    of_ref, ob_ref, state_ref, *, nb, tb):
    nc = tb // DN_CHUNK

    @pl.when(pl.program_id(0) == 0)
    def _():
        state_ref[...] = jnp.zeros_like(state_ref)

    bdm = _bd_mask()
    bdm01 = _ones_where(bdm)
    zpad = jnp.zeros((DN_CHUNK, 256), BF16)
    dirs = ((uf_ref, wf_ref, qdf_ref, inf_ref, kdtf_ref, eglf_ref, of_ref),
            (ub_ref, wb_ref, qdb_ref, inb_ref, kdtb_ref, eglb_ref, ob_ref))
    for step in range(nc):
        for bi in range(nb):
            for d in range(2):
                u_ref, w_ref, qd_ref, in_ref, kdt_ref, egl_ref, o_ref = dirs[d]
                ci = step if d == 0 else nc - 1 - step
                sl = slice(ci * DN_CHUNK, (ci + 1) * DN_CHUNK)
                si = d * nb + bi
                state = state_ref[si]
                res = jnp.dot(jnp.concatenate([w_ref[bi, sl, :], qd_ref[bi, sl, :]], axis=0),
                              state.astype(BF16), preferred_element_type=F32)
                v_new = (u_ref[bi, sl, :].astype(F32) - res[0:64]).astype(BF16)
                o_ref[bi, sl, :] = res[64:128] + jnp.dot(in_ref[bi, sl, :], _bd(v_new, bdm01),
                                                         preferred_element_type=F32)
                pair = ci // 2
                kpair = kdt_ref[bi, :, pair * 128:(pair + 1) * 128]
                vpad = jnp.concatenate([v_new, zpad] if ci % 2 == 0 else [zpad, v_new], axis=0)
                ds = jnp.dot(kpair, vpad, preferred_element_type=F32)
                state_ref[si] = state * egl_ref[bi, ci * 8:ci * 8 + 1, :] + jnp.where(bdm, ds, 0.0)


def _dn_scan_call(pre, tb):
    b, s, _ = pre[0].shape
    n = s // tb
    fwd = lambda w: pl.BlockSpec((b, tb, w), lambda i: (0, i, 0))
    bwd = lambda w: pl.BlockSpec((b, tb, w), lambda i: (0, n - 1 - i, 0))
    specs_f = [fwd(256)] * 4 + [pl.BlockSpec((b, 256, tb), lambda i: (0, 0, i)),
                                pl.BlockSpec((b, tb // 8, 256), lambda i: (0, i, 0))]
    specs_b = [bwd(256)] * 4 + [pl.BlockSpec((b, 256, tb), lambda i: (0, 0, n - 1 - i)),
                                pl.BlockSpec((b, tb // 8, 256), lambda i: (0, n - 1 - i, 0))]
    return pl.pallas_call(
        functools.partial(_dn_scan_kernel, nb=b, tb=tb),
        grid=(n,), in_specs=specs_f + specs_b, out_specs=[fwd(256), bwd(256)],
        out_shape=[jax.ShapeDtypeStruct((b, s, 256), F32)] * 2,
        scratch_shapes=[pltpu.VMEM((2 * b, 256, 256), F32)],
        compiler_params=_cparams(("arbitrary",)), name="dn_scan",
    )(*pre)


def _attn_kernel(q_ref, kt_ref, v_ref, o_ref, *, tq, tk, n_kv):
    g = pl.program_id(1)
    q2 = q_ref[0]
    lo = _iota((tq, 128), 1) < HEAD
    qs = jnp.concatenate([q2 * _ones_where(lo), q2 * _ones_where(~lo)], axis=0)

    def body(j, carry):
        m, l, acc = carry
        s = jnp.dot(qs, kt_ref[0, 0, j], preferred_element_type=F32)
        m_new = jnp.maximum(m, jnp.max(s, axis=-1, keepdims=True))
        p = jnp.exp(s - m_new)
        alpha = jnp.exp(m - m_new)
        l_new = alpha * l + jnp.sum(p, axis=-1, keepdims=True)
        vj = v_ref[0, pl.ds(pl.multiple_of(j * tk, tk), tk), :]
        acc_new = alpha * acc + jnp.dot(p.astype(BF16), vj, preferred_element_type=F32)
        return m_new, l_new, acc_new

    m0 = jnp.full((2 * tq, 1), -jnp.inf, F32)
    l0 = jnp.zeros((2 * tq, 1), F32)
    a0 = jnp.zeros((2 * tq, 128), F32)
    m, l, acc = lax.fori_loop(0, n_kv, body, (m0, l0, a0))
    o = acc / l
    o = jnp.where(g == 0, o[:, 0:HEAD], o[:, HEAD:128])
    o_ref[0] = jnp.concatenate([o[0:tq], o[tq:2 * tq]], axis=1).astype(BF16)


def _attn_call(aq, akt, av, tq):
    b, s, _ = aq.shape
    _, _, n_kv, _, tk = akt.shape
    return pl.pallas_call(
        functools.partial(_attn_kernel, tq=tq, tk=tk, n_kv=n_kv),
        grid=(b, 2, s // tq),
        in_specs=[pl.BlockSpec((1, tq, 128), lambda bi, g, i: (bi, i, g)),
                  pl.BlockSpec((1, 1, n_kv, 128, tk), lambda bi, g, i: (bi, g, 0, 0, 0)),
                  pl.BlockSpec((1, s, 128), lambda bi, g, i: (bi, 0, 0))],
        out_specs=pl.BlockSpec((1, tq, 128), lambda bi, g, i: (bi, i, g)),
        out_shape=jax.ShapeDtypeStruct((b, s, 256), BF16),
        compiler_params=_cparams(("parallel", "parallel", "parallel")), name="attn",
    )(aq, akt, av)


def _out_kernel(x_ref, ya_ref, yd_ref, yc_ref, of_ref, ob_ref, gates_ref, dnw_ref, wo_ref, o_ref):
    o = of_ref[0] + ob_ref[0]
    gates = gates_ref[0].astype(F32)
    on = o * lax.rsqrt(_head_sum(o * o) * (1.0 / HEAD) + EPS) * dnw_ref[...]
    yb = (on * gates[:, 0:256]).astype(BF16)
    yc = (yc_ref[0].astype(F32) * gates[:, 256:512]).astype(BF16)
    mix = jnp.concatenate([ya_ref[0], yb, yc, yd_ref[0]], axis=1)
    o_ref[0] = x_ref[0] + jnp.dot(mix, wo_ref[...], preferred_element_type=F32)


def _out_call(x, ya, yd, yc, o_f, o_b, gates, lw, tm):
    b, s, _ = x.shape
    tok = lambda w: pl.BlockSpec((1, tm, w), lambda bi, i: (bi, i, 0))
    full = lambda shape: pl.BlockSpec(shape, lambda bi, i: (0,) * len(shape))
    return pl.pallas_call(
        _out_kernel, grid=(b, s // tm),
        in_specs=[tok(D_MODEL), tok(256), tok(256), tok(256), tok(256), tok(256), tok(512),
                  full((1, 256)), full((D_MODEL, D_MODEL))],
        out_specs=tok(D_MODEL), out_shape=jax.ShapeDtypeStruct((b, s, D_MODEL), F32),
        compiler_params=_cparams(("parallel", "parallel")), name="out",
    )(x, ya, yd, yc, o_f, o_b, gates, lw["dn_norm_w"], lw["w_out"])


def _rope_tables(seq_len):
    t = jnp.arange(seq_len)
    pos = jnp.stack([t // GRID_W, t % GRID_W], axis=-1).astype(F32)
    n_freq = HEAD // 4
    inv_freq = jnp.power(ROPE_THETA, -2.0 * jnp.arange(n_freq, dtype=F32) / (HEAD // 2))
    ang = pos[:, :, None] * inv_freq
    cos = jnp.repeat(jnp.cos(ang)[:, :, None, :], 2, axis=2).reshape(seq_len, HEAD)
    sin = jnp.sin(ang)
    sin = jnp.stack([-sin, sin], axis=2).reshape(seq_len, HEAD)
    return jnp.tile(cos, (1, 2)), jnp.tile(sin, (1, 2))


def _layer_weights(l, norm_w, w_in, sgu_w, sgu_b, conv_w, a_log, dt_bias, dn_norm_w, q_norm_w,
                   k_norm_w, pool_w, pool_scale, w_out):
    w = w_in[l]
    cols = lambda a, n: w[:, a:a + n]
    w_halo = jnp.concatenate([cols(_B_Q, 768), cols(_D_X, 256)], axis=1).astype(BF16)
    w_main = jnp.concatenate([
        cols(_A_U, 768), cols(_B_Z, 256), cols(_C_Q, 768), cols(_D_Z, 256), cols(_B_BETA, 16),
        jnp.zeros((D_MODEL, 112), F32)], axis=1).astype(BF16)
    dn_row = jnp.zeros((8, 128), F32)
    dn_row = dn_row.at[0, 8:16].set(a_log[l].reshape(8)).at[1, 8:16].set(dt_bias[l].reshape(8))
    pool_bd = jnp.zeros((256, 256), F32)
    for gi in range(len(POOL_WINDOWS)):
        pool_bd = pool_bd.at[gi * 64:(gi + 1) * 64, gi * 64:(gi + 1) * 64].set(pool_w[l, gi])
    return {
        "norm_w": norm_w[l].reshape(1, D_MODEL),
        "w_halo": w_halo, "w_main": w_main,
        "sgu_w": jnp.transpose(sgu_w[l], (1, 0, 2)).reshape(SGU_CHUNK, 4 * SGU_CHUNK).astype(BF16),
        "sgu_b": jnp.repeat(sgu_b[l].T, HEAD, axis=1),
        "conv_w": jnp.concatenate([conv_w[l], jnp.zeros((3, 768), F32)], axis=0),
        "dn_row": dn_row,
        "q_norm_w": jnp.tile(q_norm_w[l], 4).reshape(1, 256),
        "k_norm_w": jnp.tile(k_norm_w[l], 2).reshape(1, 128),
        "pool_w": pool_bd.astype(BF16),
        "pool_scale": pool_scale[l].reshape(1, 256),
        "dn_norm_w": jnp.tile(dn_norm_w[l], 4).reshape(1, 256),
        "w_out": w_out[l].astype(BF16),
    }


def _tiles(seq_len):
    tm = min(512, seq_len)
    return dict(tm=tm, ta=min(256, seq_len), tb=min(256, seq_len), tq=min(256, seq_len))


def _layer(x, lw, rope_c, rope_s):
    t = _tiles(x.shape[1])
    ya, yd, dq, dk, dv, bg, gates, aq, akt, av = _proj_call(x, lw, rope_c, rope_s, t["tm"])
    pre = _dn_pre_call(dq, dk, dv, bg, t["ta"])
    o_f, o_b = _dn_scan_call(pre, t["tb"])
    yc = _attn_call(aq, akt, av, t["tq"])
    return _out_call(x, ya, yd, yc, o_f, o_b, gates, lw, t["tm"])


def kernel(x_prompt, x_sample, norm_w, w_in, sgu_w, sgu_b, conv_w, a_log, dt_bias, dn_norm_w,
           q_norm_w, k_norm_w, pool_w, pool_scale, w_out):
    depth = norm_w.shape[0]
    rope_p = _rope_tables(x_prompt.shape[1])
    rope_s = _rope_tables(x_sample.shape[1])
    y_prompt, y_sample = x_prompt, x_sample
    for l in range(depth):
        lw = _layer_weights(l, norm_w, w_in, sgu_w, sgu_b, conv_w, a_log, dt_bias, dn_norm_w,
                            q_norm_w, k_norm_w, pool_w, pool_scale, w_out)
        y_prompt = _layer(y_prompt, lw, *rope_p)
        y_sample = _layer(y_sample, lw, *rope_s)
    return (y_prompt, y_sample)
```

```python
import functools
import math

import numpy as np
import jax
import jax.numpy as jnp
from jax import lax
from jax.experimental import pallas as pl
from jax.experimental.pallas import tpu as pltpu

F32 = jnp.float32
BF16 = jnp.bfloat16

D_MODEL = 1024
GROUP_W = 256
HEAD = 64
N_HEADS = 4
GRID_W = 64
EPS = 1e-6
SGU_CHUNK = 128
DN_CHUNK = 64
CONV_K = 5
ROPE_THETA = 10000.0
POOL_WINDOWS = (2, 4, 8, 16)
HALO = 16
NEG_BIG = -1e30
LOG2_E = 1.4426950408889634

V7X_VMEM_LIMIT_BYTES = 56 * 1024 * 1024

_A_U, _A_V, _A_Z = 0, 256, 512
_B_Q, _B_K, _B_V, _B_Z, _B_BETA, _B_ALPHA = 768, 1024, 1280, 1536, 1792, 1800
_C_Q, _C_K, _C_V, _C_Z = 1808, 2064, 2192, 2320
_D_X, _D_Z = 2576, 2832
IN_COLS = 3088
MAIN_COLS = 2176


def _cparams(semantics):
    return pltpu.CompilerParams(dimension_semantics=semantics,
                                vmem_limit_bytes=V7X_VMEM_LIMIT_BYTES)


def _split2(x):
    hi = x.astype(BF16)
    lo = (x - hi.astype(F32)).astype(BF16)
    return hi, lo


def _split3(x):
    hi = x.astype(BF16)
    r = x - hi.astype(F32)
    mid = r.astype(BF16)
    lo = (r - mid.astype(F32)).astype(BF16)
    return hi, mid, lo


def _iota(shape, dim):
    return lax.broadcasted_iota(jnp.int32, shape, dim)


def _ones_where(cond):
    return jnp.where(cond, 1.0, 0.0).astype(BF16)


def _head_sum(x2):
    w = x2.shape[1]
    g = _ones_where(jnp.right_shift(_iota((2 * w, w), 0) & (w - 1), 6)
                    == jnp.right_shift(_iota((2 * w, w), 1), 6))
    hi, lo = _split2(x2)
    return jnp.dot(jnp.concatenate([hi, lo], axis=1), g, preferred_element_type=F32)


def _silu(z):
    return z / (1.0 + jnp.exp(-z))


def _bd_mask():
    return jnp.right_shift(_iota((256, 256), 0), 6) == jnp.right_shift(_iota((256, 256), 1), 6)


def _bd(x, mask01):
    return jnp.concatenate([x, x, x, x], axis=0) * mask01


def _proj_kernel(xp_ref, xc_ref, xn_ref, nw_ref, wh_ref, wm_ref, sguw_ref, sgub_ref, convw_ref,
                 dnrow_ref, qw_ref, kw_ref, rc_ref, rs_ref, poolw_ref, pools_ref,
                 ya_ref, yd_ref, dq_ref, dk_ref, dv_ref, bg_ref, gates_ref, aqt_ref, ak_ref, avt_ref,
                 hext_ref, *, tm, n_tiles, seq_len):
    i = pl.program_id(1)
    n_ext = tm + 2 * HALO
    nw = nw_ref[...]

    def norm(x):
        ms = jnp.mean(x * x, axis=-1, keepdims=True)
        return x * lax.rsqrt(ms + EPS) * nw

    hext_ref[0:HALO, :] = jnp.where(i > 0, norm(xp_ref[0]), 0.0).astype(BF16)
    hext_ref[HALO:HALO + tm, :] = norm(xc_ref[0]).astype(BF16)
    hext_ref[HALO + tm:n_ext, :] = jnp.where(i < n_tiles - 1, norm(xn_ref[0]), 0.0).astype(BF16)

    ph = jnp.dot(hext_ref[...], wh_ref[...], preferred_element_type=F32)
    pm = jnp.dot(hext_ref[HALO:HALO + tm, :], wm_ref[...], preferred_element_type=F32)

    def rows(x):
        return x[HALO:HALO + tm]

    a_u, a_v, a_z = pm[:, 0:256], pm[:, 256:512], pm[:, 512:768]
    vn = a_v * lax.rsqrt(_head_sum(a_v * a_v) * (1.0 / HEAD) + EPS)
    mask4 = _ones_where(jnp.right_shift(_iota((512, 256), 0), 7) == jnp.right_shift(_iota((512, 256), 1), 6))
    mixed = []
    for c in range(tm // SGU_CHUNK):
        vc = vn[c * SGU_CHUNK:(c + 1) * SGU_CHUNK].astype(BF16)
        bdv = jnp.concatenate([vc, vc, vc, vc], axis=0) * mask4
        mixed.append(jnp.dot(sguw_ref[...], bdv, preferred_element_type=F32) + sgub_ref[...])
    mixed = jnp.concatenate(mixed, axis=0)
    ya_ref[0] = (a_u * mixed * _silu(a_z)).astype(BF16)

    xd = ph[:, 768:1024]
    a1 = xd + pltpu.roll(xd, n_ext - 1, 0)
    a2 = a1 + pltpu.roll(a1, n_ext - 2, 0)
    a3 = a2 + pltpu.roll(a2, n_ext - 4, 0)
    a4 = a3 + pltpu.roll(a3, n_ext - 8, 0)
    w2 = rows(pltpu.roll(a1, 1, 0))
    w4 = rows(pltpu.roll(a2, 2, 0))
    w8 = rows(pltpu.roll(a3, 4, 0))
    w16 = rows(pltpu.roll(a4, 8, 0))
    grp = jnp.right_shift(_iota((tm, 256), 1), 6)
    half = jnp.left_shift(jnp.ones((tm, 256), jnp.int32), grp)
    t = i * tm + _iota((tm, 256), 0)
    cnt = (jnp.minimum(t + half, seq_len) - jnp.maximum(t - half, 0)).astype(F32)
    win = jnp.where(grp == 0, w2, jnp.where(grp == 1, w4, jnp.where(grp == 2, w8, w16)))
    diff = win / cnt - rows(xd)
    yd = jnp.dot(diff.astype(BF16), poolw_ref[...], preferred_element_type=F32) * pools_ref[...]
    yd_ref[0] = (yd * _silu(pm[:, 1792:2048])).astype(BF16)

    xb = ph[:, 0:768]
    cw = convw_ref[...]
    conv = (rows(pltpu.roll(xb, 2, 0)) * cw[0:1] + rows(pltpu.roll(xb, 1, 0)) * cw[1:2]
            + rows(xb) * cw[2:3] + rows(pltpu.roll(xb, n_ext - 1, 0)) * cw[3:4]
            + rows(pltpu.roll(xb, n_ext - 2, 0)) * cw[4:5])
    act = _silu(conv)
    bq, bk = act[:, 0:256], act[:, 256:512]
    dq_ref[0] = (bq * lax.rsqrt(_head_sum(bq * bq) + EPS) * (HEAD ** -0.5)).astype(BF16)
    dk_ref[0] = (bk * lax.rsqrt(_head_sum(bk * bk) + EPS)).astype(BF16)
    dv_ref[0] = act[:, 512:768].astype(BF16)
    ba = pm[:, 2048:2176]
    lane = _iota((tm, 128), 1)
    xa = ba + dnrow_ref[1:2, :]
    softplus = jnp.maximum(xa, 0.0) + jnp.log1p(jnp.exp(-jnp.abs(xa)))
    g = -jnp.exp(dnrow_ref[0:1, :]) * softplus
    bg_ref[0] = jnp.where(lane < 8, 1.0 / (1.0 + jnp.exp(-ba)), g)
    gates_ref[0] = jnp.concatenate([_silu(pm[:, 768:1024]), _silu(pm[:, 1536:1792])], axis=1).astype(BF16)

    rc, rs = rc_ref[...], rs_ref[...]
    first = (_iota((tm, 128), 1) & 16) == 0

    def rope(x):
        sw = jnp.where(first, pltpu.roll(x, 112, 1), pltpu.roll(x, 16, 1))
        return x * rc + sw * rs

    cq, ck = pm[:, 1024:1280], pm[:, 1280:1408]
    qn = cq * lax.rsqrt(_head_sum(cq * cq) * (1.0 / HEAD) + EPS) * qw_ref[...]
    qr = jnp.concatenate([rope(qn[:, 0:128]), rope(qn[:, 128:256])], axis=1) * (HEAD ** -0.5 * LOG2_E)
    qt = qr.T
    aqt_ref[0, 0] = qt[0:128].astype(BF16)
    aqt_ref[0, 1] = qt[128:256].astype(BF16)
    kn = ck * lax.rsqrt(_head_sum(ck * ck) * (1.0 / HEAD) + EPS) * kw_ref[...]
    ak_ref[0] = rope(kn).astype(BF16)
    vt = pm[:, 1408:1536].T
    ones = jnp.ones((16, tm), F32)
    avt_ref[0, 0, 0] = jnp.concatenate([vt[0:64], ones], axis=0).astype(BF16)
    avt_ref[0, 1, 0] = jnp.concatenate([vt[64:128], ones], axis=0).astype(BF16)


def _proj_call(x, lw, rope_c, rope_s, tm):
    b, s, _ = x.shape
    n_tiles = s // tm
    hb = tm // HALO
    last_hb = s // HALO - 1
    full = lambda shape: pl.BlockSpec(shape, lambda bi, i: (0,) * len(shape))
    tok = lambda w: pl.BlockSpec((1, tm, w), lambda bi, i: (bi, i, 0))
    in_specs = [
        pl.BlockSpec((1, HALO, D_MODEL), lambda bi, i: (bi, jnp.maximum(i * hb - 1, 0), 0)),
        pl.BlockSpec((1, tm, D_MODEL), lambda bi, i: (bi, i, 0)),
        pl.BlockSpec((1, HALO, D_MODEL), lambda bi, i: (bi, jnp.minimum((i + 1) * hb, last_hb), 0)),
        full((1, D_MODEL)), full((D_MODEL, 1024)), full((D_MODEL, MAIN_COLS)),
        full((128, 512)), full((128, 256)), full((8, 768)), full((8, 128)),
        full((1, 256)), full((1, 128)),
        pl.BlockSpec((tm, 128), lambda bi, i: (i, 0)), pl.BlockSpec((tm, 128), lambda bi, i: (i, 0)),
        full((256, 256)), full((1, 256)),
    ]
    out_shape = [
        jax.ShapeDtypeStruct((b, s, 256), BF16),
        jax.ShapeDtypeStruct((b, s, 256), BF16),
        jax.ShapeDtypeStruct((b, s, 256), BF16),
        jax.ShapeDtypeStruct((b, s, 256), BF16),
        jax.ShapeDtypeStruct((b, s, 256), BF16),
        jax.ShapeDtypeStruct((b, s, 128), F32),
        jax.ShapeDtypeStruct((b, s, 512), BF16),
        jax.ShapeDtypeStruct((b, 2, 128, s), BF16),
        jax.ShapeDtypeStruct((b, s, 128), BF16),
        jax.ShapeDtypeStruct((b, 2, n_tiles, HEAD + 16, tm), BF16),
    ]
    out_specs = [tok(256), tok(256), tok(256), tok(256), tok(256), tok(128), tok(512),
                 pl.BlockSpec((1, 2, 128, tm), lambda bi, i: (bi, 0, 0, i)), tok(128),
                 pl.BlockSpec((1, 2, 1, HEAD + 16, tm), lambda bi, i: (bi, 0, i, 0, 0))]
    return pl.pallas_call(
        functools.partial(_proj_kernel, tm=tm, n_tiles=n_tiles, seq_len=s),
        grid=(b, n_tiles), in_specs=in_specs, out_specs=out_specs, out_shape=out_shape,
        scratch_shapes=[pltpu.VMEM((tm + 2 * HALO, D_MODEL), BF16)],
        compiler_params=_cparams(("parallel", "parallel")), name="proj",
    )(x, x, x, lw["norm_w"], lw["w_halo"], lw["w_main"], lw["sgu_w"], lw["sgu_b"], lw["conv_w"],
      lw["dn_row"], lw["q_norm_w"], lw["k_norm_w"], rope_c, rope_s, lw["pool_w"], lw["pool_scale"])


def _dn_pre_kernel(q_ref, k_ref, v_ref, bg_ref,
                   uf_ref, wf_ref, qdf_ref, inf_ref, kdtf_ref, eglf_ref,
                   ub_ref, wb_ref, qdb_ref, inb_ref, kdtb_ref, eglb_ref, *, ta):
    nc = ta // DN_CHUNK
    q = q_ref[0].astype(F32)
    k = k_ref[0].astype(F32)
    v = v_ref[0].astype(F32)
    bg = bg_ref[0]

    r = _iota((ta, ta), 0)
    c = _iota((ta, ta), 1)
    same = jnp.right_shift(r, 6) == jnp.right_shift(c, 6)
    m_lin = _ones_where(same & (c <= r))
    m_uin = _ones_where(same & (c >= r))
    m_ust = _ones_where(same & (c > r))
    m_lst = _ones_where(same & (c < r))
    m4 = jnp.concatenate([m_lin, m_uin, m_ust, m_lst], axis=0)
    cs = jnp.dot(jnp.concatenate([m4, m4, m4], axis=1), jnp.concatenate(_split3(bg), axis=0),
                 preferred_element_type=F32)
    lane = _iota((ta, 128), 1)
    nb1 = jnp.where(lane < 8, bg, jnp.where(lane < 12, cs[0:ta], cs[ta:2 * ta]))
    nb2 = jnp.where(lane < 12, cs[2 * ta:3 * ta], cs[3 * ta:4 * ta])
    src = _iota((384, 1024), 0) & 127
    e1 = _ones_where(src == jnp.right_shift(_iota((384, 1024), 1), 6))
    x1 = jnp.dot(jnp.concatenate(_split3(nb1), axis=1), e1, preferred_element_type=F32)
    src2 = _iota((384, 512), 0) & 127
    e2 = _ones_where(src2 == jnp.right_shift(_iota((384, 512), 1), 6) + 8)
    x2 = jnp.dot(jnp.concatenate(_split3(nb2), axis=1), e2, preferred_element_type=F32)

    li = _iota((ta, 256), 0) & 63
    lj = _iota((ta, 256), 1) & 63
    eyecat = li == lj
    eyef = jnp.where(eyecat, 1.0, 0.0)
    j3 = _ones_where(same)
    j3 = jnp.concatenate([j3, j3, j3], axis=1)
    bdm = _ones_where(_bd_mask())

    gram, qk = [], []
    for ci in range(nc):
        sl = slice(ci * DN_CHUNK, (ci + 1) * DN_CHUNK)
        kc = k_ref[0, sl, :]
        gq = lax.dot_general(jnp.concatenate([kc, q_ref[0, sl, :]], axis=0), _bd(kc, bdm),
                             (((1,), (1,)), ((), ())), preferred_element_type=F32)
        gram.append(gq[0:64])
        qk.append(gq[64:128])
    gram = jnp.concatenate(gram, axis=0)
    qk = jnp.concatenate(qk, axis=0)

    outs = ((uf_ref, wf_ref, qdf_ref, inf_ref, kdtf_ref, eglf_ref),
            (ub_ref, wb_ref, qdb_ref, inb_ref, kdtb_ref, eglb_ref))
    chains = []
    for d in range(2):
        u_ref, w_ref, qd_ref, in_ref, kdt_ref, egl_ref = outs[d]
        bx = x1[:, d * 256:(d + 1) * 256]
        gcx = x1[:, 512 + d * 256:768 + d * 256]
        dglx = x2[:, d * 256:(d + 1) * 256]
        rowf = jnp.dot(j3, jnp.concatenate(_split3(jnp.where(eyecat, gcx, 0.0)), axis=0),
                       preferred_element_type=F32)
        incl = (li >= lj) if d == 0 else (li <= lj)
        strict = (li > lj) if d == 0 else (li < lj)
        decay = jnp.exp(jnp.where(incl, gcx - rowf, NEG_BIG))
        eg = jnp.exp(gcx)
        a_all = jnp.where(strict, bx * gram * decay, 0.0)
        in_ref[0] = jnp.where(incl, qk * decay, 0.0).astype(BF16)
        qd_ref[0] = (q * eg).astype(BF16)
        kdt_ref[0] = (k * jnp.exp(dglx)).T.astype(BF16)
        vb = (v * bx).astype(BF16)
        kbg = (k * bx * eg).astype(BF16)
        for ci in range(nc):
            sl = slice(ci * DN_CHUNK, (ci + 1) * DN_CHUNK)
            last = ci * DN_CHUNK + (DN_CHUNK - 1 if d == 0 else 0)
            egl_ref[0, ci * 8:(ci + 1) * 8, :] = jnp.broadcast_to(eg[last:last + 1, :], (8, 256))
            chains.append(dict(sl=sl, u_ref=u_ref, w_ref=w_ref, ak=a_all[sl], p=eyef[sl] - a_all[sl],
                               vb=vb[sl], kbg=kbg[sl]))

    for stage in range(5):
        for ch in chains:
            akb = ch["ak"].astype(BF16)
            lhs = akb if stage == 0 else jnp.concatenate([akb, ch["p"].astype(BF16)], axis=0)
            res = jnp.dot(lhs, _bd(akb, bdm), preferred_element_type=F32)
            ch["ak"] = res[0:64]
            if stage > 0:
                ch["p"] = ch["p"] + res[64:128]
    for ch in chains:
        ch["p"] = ch["p"] + jnp.dot(ch["p"].astype(BF16), _bd(ch["ak"].astype(BF16), bdm),
                                    preferred_element_type=F32)
    for ch in chains:
        rhs = jnp.concatenate([_bd(ch["vb"], bdm), _bd(ch["kbg"], bdm)], axis=1)
        uw = jnp.dot(ch["p"].astype(BF16), rhs, preferred_element_type=F32)
        ch["u_ref"][0, ch["sl"], :] = uw[:, 0:256].astype(BF16)
        ch["w_ref"][0, ch["sl"], :] = uw[:, 256:512].astype(BF16)


def _dn_pre_call(dq, dk, dv, bg, ta):
    b, s, _ = dq.shape
    tok = lambda w: pl.BlockSpec((1, ta, w), lambda bi, i: (bi, i, 0))
    per_dir_shapes = [jax.ShapeDtypeStruct((b, s, 256), BF16)] * 4 + [
        jax.ShapeDtypeStruct((b, 256, s), BF16), jax.ShapeDtypeStruct((b, s // 8, 256), F32)]
    per_dir_specs = [tok(256)] * 4 + [pl.BlockSpec((1, 256, ta), lambda bi, i: (bi, 0, i)),
                                      pl.BlockSpec((1, ta // 8, 256), lambda bi, i: (bi, i, 0))]
    return pl.pallas_call(
        functools.partial(_dn_pre_kernel, ta=ta),
        grid=(b, s // ta), in_specs=[tok(256), tok(256), tok(256), tok(128)],
        out_specs=per_dir_specs * 2, out_shape=per_dir_shapes * 2,
        compiler_params=_cparams(("parallel", "parallel")), name="dn_pre",
    )(dq, dk, dv, bg)


def _dn_scan_kernel(uf_ref, wf_ref, qdf_ref, inf_ref, kdtf_ref, eglf_ref,
                    ub_ref, wb_ref, qdb_ref, inb_ref, kdtb_ref, eglb_ref,
                    of_ref, ob_ref, state_ref, *, nb, tb):
    nc = tb // DN_CHUNK

    @pl.when(pl.program_id(0) == 0)
    def _():
        state_ref[...] = jnp.zeros_like(state_ref)

    bdm = _bd_mask()
    bdm01 = _ones_where(bdm)
    zpad = jnp.zeros((DN_CHUNK, 256), BF16)
    dirs = ((uf_ref, wf_ref, qdf_ref, inf_ref, kdtf_ref, eglf_ref, of_ref),
            (ub_ref, wb_ref, qdb_ref, inb_ref, kdtb_ref, eglb_ref, ob_ref))
    for step in range(nc):
        for bi in range(nb):
            for d in range(2):
                u_ref, w_ref, qd_ref, in_ref, kdt_ref, egl_ref, o_ref = dirs[d]
                ci = step if d == 0 else nc - 1 - step
                sl = slice(ci * DN_CHUNK, (ci + 1) * DN_CHUNK)
                si = d * nb + bi
                state = state_ref[si]
                res = jnp.dot(jnp.concatenate([w_ref[bi, sl, :], qd_ref[bi, sl, :]], axis=0),
                              state.astype(BF16), preferred_element_type=F32)
                v_new = (u_ref[bi, sl, :].astype(F32) - res[0:64]).astype(BF16)
                o_ref[bi, sl, :] = res[64:128] + jnp.dot(in_ref[bi, sl, :], _bd(v_new, bdm01),
                                                         preferred_element_type=F32)
                pair = ci // 2
                kpair = kdt_ref[bi, :, pair * 128:(pair + 1) * 128]
                vpad = jnp.concatenate([v_new, zpad] if ci % 2 == 0 else [zpad, v_new], axis=0)
                ds = jnp.dot(kpair, vpad, preferred_element_type=F32)
                state_ref[si] = state * egl_ref[bi, ci * 8:ci * 8 + 1, :] + jnp.where(bdm, ds, 0.0)


def _dn_scan_call(pre, tb):
    b, s, _ = pre[0].shape
    n = s // tb
    fwd = lambda w: pl.BlockSpec((b, tb, w), lambda i: (0, i, 0))
    bwd = lambda w: pl.BlockSpec((b, tb, w), lambda i: (0, n - 1 - i, 0))
    specs_f = [fwd(256)] * 4 + [pl.BlockSpec((b, 256, tb), lambda i: (0, 0, i)),
                                pl.BlockSpec((b, tb // 8, 256), lambda i: (0, i, 0))]
    specs_b = [bwd(256)] * 4 + [pl.BlockSpec((b, 256, tb), lambda i: (0, 0, n - 1 - i)),
                                pl.BlockSpec((b, tb // 8, 256), lambda i: (0, n - 1 - i, 0))]
    return pl.pallas_call(
        functools.partial(_dn_scan_kernel, nb=b, tb=tb),
        grid=(n,), in_specs=specs_f + specs_b, out_specs=[fwd(256), bwd(256)],
        out_shape=[jax.ShapeDtypeStruct((b, s, 256), F32)] * 2,
        scratch_shapes=[pltpu.VMEM((2 * b, 256, 256), F32)],
        compiler_params=_cparams(("arbitrary",)), name="dn_scan",
    )(*pre)


def _attn_kernel(qt_ref, k_ref, vt_ref, o_ref, st_ref, *, tq, tk, n_kv):
    g = pl.program_id(1)
    qt = qt_ref[0, 0].astype(F32)
    top = jnp.concatenate([qt[0:HEAD], qt[HEAD:128]], axis=1)
    sel = jnp.where(g == 0, 1.0, 0.0)
    rhs = jnp.concatenate([top * sel, top * (1.0 - sel)], axis=0).astype(BF16)

    def scores(j):
        kj = k_ref[0, pl.ds(pl.multiple_of(j * tk, tk), tk), :]
        return jnp.dot(kj, rhs, preferred_element_type=F32)

    def update(j, st, m, acc):
        m_new = jnp.maximum(m, jnp.max(st, axis=0, keepdims=True))
        p = jnp.exp2(st - m_new).astype(BF16)
        alpha = jnp.exp2(m - m_new)
        return m_new, alpha * acc + jnp.dot(vt_ref[0, 0, j], p, preferred_element_type=F32)

    st_ref[0] = scores(0)

    def body(i, carry):
        st_ref[1] = scores(2 * i + 1)
        carry = update(2 * i, st_ref[0], *carry)
        st_ref[0] = scores(2 * i + 2)
        return update(2 * i + 1, st_ref[1], *carry)

    m0 = jnp.full((1, 2 * tq), -jnp.inf, F32)
    a0 = jnp.zeros((HEAD + 16, 2 * tq), F32)
    m, acc = lax.fori_loop(0, n_kv // 2 - 1, body, (m0, a0))
    st_ref[1] = scores(n_kv - 1)
    m, acc = update(n_kv - 2, st_ref[0], m, acc)
    _, acc = update(n_kv - 1, st_ref[1], m, acc)
    o = acc[0:HEAD] / acc[HEAD:HEAD + 1]
    ot = jnp.concatenate([o[:, 0:tq], o[:, tq:2 * tq]], axis=0)
    o_ref[0] = ot.T.astype(BF16)


def _attn_call(aqt, ak, avt, tq):
    b, s, _ = ak.shape
    _, _, n_kv, vrows, tk = avt.shape
    return pl.pallas_call(
        functools.partial(_attn_kernel, tq=tq, tk=tk, n_kv=n_kv),
        grid=(b, 2, s // tq),
        in_specs=[pl.BlockSpec((1, 1, 128, tq), lambda bi, g, i: (bi, g, 0, i)),
                  pl.BlockSpec((1, s, 128), lambda bi, g, i: (bi, 0, 0)),
                  pl.BlockSpec((1, 1, n_kv, vrows, tk), lambda bi, g, i: (bi, g, 0, 0, 0))],
        out_specs=pl.BlockSpec((1, tq, 128), lambda bi, g, i: (bi, i, g)),
        out_shape=jax.ShapeDtypeStruct((b, s, 256), BF16),
        scratch_shapes=[pltpu.VMEM((2, tk, 2 * tq), F32)],
        compiler_params=_cparams(("parallel", "parallel", "parallel")), name="attn",
    )(aqt, ak, avt)


def _out_kernel(x_ref, ya_ref, yd_ref, yc_ref, of_ref, ob_ref, gates_ref, dnw_ref, wo_ref, o_ref):
    o = of_ref[0] + ob_ref[0]
    gates = gates_ref[0].astype(F32)
    on = o * lax.rsqrt(_head_sum(o * o) * (1.0 / HEAD) + EPS) * dnw_ref[...]
    yb = (on * gates[:, 0:256]).astype(BF16)
    yc = (yc_ref[0].astype(F32) * gates[:, 256:512]).astype(BF16)
    mix = jnp.concatenate([ya_ref[0], yb, yc, yd_ref[0]], axis=1)
    o_ref[0] = x_ref[0] + jnp.dot(mix, wo_ref[...], preferred_element_type=F32)


def _out_call(x, ya, yd, yc, o_f, o_b, gates, lw, tm):
    b, s, _ = x.shape
    tok = lambda w: pl.BlockSpec((1, tm, w), lambda bi, i: (bi, i, 0))
    full = lambda shape: pl.BlockSpec(shape, lambda bi, i: (0,) * len(shape))
    return pl.pallas_call(
        _out_kernel, grid=(b, s // tm),
        in_specs=[tok(D_MODEL), tok(256), tok(256), tok(256), tok(256), tok(256), tok(512),
                  full((1, 256)), full((D_MODEL, D_MODEL))],
        out_specs=tok(D_MODEL), out_shape=jax.ShapeDtypeStruct((b, s, D_MODEL), F32),
        compiler_params=_cparams(("parallel", "parallel")), name="out",
    )(x, ya, yd, yc, o_f, o_b, gates, lw["dn_norm_w"], lw["w_out"])


def _rope_tables(seq_len):
    t = jnp.arange(seq_len)
    pos = jnp.stack([t // GRID_W, t % GRID_W], axis=-1).astype(F32)
    n_freq = HEAD // 4
    inv_freq = jnp.power(ROPE_THETA, -2.0 * jnp.arange(n_freq, dtype=F32) / (HEAD // 2))
    ang = pos[:, :, None] * inv_freq
    cos = jnp.repeat(jnp.cos(ang)[:, :, None, :], 2, axis=2).reshape(seq_len, HEAD)
    sin = jnp.sin(ang)
    sin = jnp.stack([-sin, sin], axis=2).reshape(seq_len, HEAD)
    return jnp.tile(cos, (1, 2)), jnp.tile(sin, (1, 2))


def _layer_weights(l, norm_w, w_in, sgu_w, sgu_b, conv_w, a_log, dt_bias, dn_norm_w, q_norm_w,
                   k_norm_w, pool_w, pool_scale, w_out):
    w = w_in[l]
    cols = lambda a, n: w[:, a:a + n]
    w_halo = jnp.concatenate([cols(_B_Q, 768), cols(_D_X, 256)], axis=1).astype(BF16)
    w_main = jnp.concatenate([
        cols(_A_U, 768), cols(_B_Z, 256), cols(_C_Q, 768), cols(_D_Z, 256), cols(_B_BETA, 16),
        jnp.zeros((D_MODEL, 112), F32)], axis=1).astype(BF16)
    dn_row = jnp.zeros((8, 128), F32)
    dn_row = dn_row.at[0, 8:16].set(a_log[l].reshape(8)).at[1, 8:16].set(dt_bias[l].reshape(8))
    pool_bd = jnp.zeros((256, 256), F32)
    for gi in range(len(POOL_WINDOWS)):
        pool_bd = pool_bd.at[gi * 64:(gi + 1) * 64, gi * 64:(gi + 1) * 64].set(pool_w[l, gi])
    return {
        "norm_w": norm_w[l].reshape(1, D_MODEL),
        "w_halo": w_halo, "w_main": w_main,
        "sgu_w": jnp.transpose(sgu_w[l], (1, 0, 2)).reshape(SGU_CHUNK, 4 * SGU_CHUNK).astype(BF16),
        "sgu_b": jnp.repeat(sgu_b[l].T, HEAD, axis=1),
        "conv_w": jnp.concatenate([conv_w[l], jnp.zeros((3, 768), F32)], axis=0),
        "dn_row": dn_row,
        "q_norm_w": jnp.tile(q_norm_w[l], 4).reshape(1, 256),
        "k_norm_w": jnp.tile(k_norm_w[l], 2).reshape(1, 128),
        "pool_w": pool_bd.astype(BF16),
        "pool_scale": pool_scale[l].reshape(1, 256),
        "dn_norm_w": jnp.tile(dn_norm_w[l], 4).reshape(1, 256),
        "w_out": w_out[l].astype(BF16),
    }


def _tiles(seq_len):
    tm = min(512, seq_len)
    return dict(tm=tm, ta=min(256, seq_len), tb=min(256, seq_len), tq=min(256, seq_len))


def _layer(x, lw, rope_c, rope_s):
    t = _tiles(x.shape[1])
    ya, yd, dq, dk, dv, bg, gates, aqt, ak, avt = _proj_call(x, lw, rope_c, rope_s, t["tm"])
    pre = _dn_pre_call(dq, dk, dv, bg, t["ta"])
    o_f, o_b = _dn_scan_call(pre, t["tb"])
    yc = _attn_call(aqt, ak, avt, t["tq"])
    return _out_call(x, ya, yd, yc, o_f, o_b, gates, lw, t["tm"])


def kernel(x_prompt, x_sample, norm_w, w_in, sgu_w, sgu_b, conv_w, a_log, dt_bias, dn_norm_w,
           q_norm_w, k_norm_w, pool_w, pool_scale, w_out):
    depth = norm_w.shape[0]
    rope_p = _rope_tables(x_prompt.shape[1])
    rope_s = _rope_tables(x_sample.shape[1])
    y_prompt, y_sample = x_prompt, x_sample
    for l in range(depth):
        lw = _layer_weights(l, norm_w, w_in, sgu_w, sgu_b, conv_w, a_log, dt_bias, dn_norm_w,
                            q_norm_w, k_norm_w, pool_w, pool_scale, w_out)
        y_prompt = _layer(y_prompt, lw, *rope_p)
        y_sample = _layer(y_sample, lw, *rope_s)
    return (y_prompt, y_sample)
```

```python
import functools
import math

import numpy as np
import jax
import jax.numpy as jnp
from jax import lax
from jax.experimental import pallas as pl
from jax.experimental.pallas import tpu as pltpu

F32 = jnp.float32
BF16 = jnp.bfloat16

D_MODEL = 1024
GROUP_W = 256
HEAD = 64
N_HEADS = 4
GRID_W = 64
EPS = 1e-6
SGU_CHUNK = 128
DN_CHUNK = 64
CONV_K = 5
ROPE_THETA = 10000.0
POOL_WINDOWS = (2, 4, 8, 16)
HALO = 16
NEG_BIG = -1e30
LOG2_E = 1.4426950408889634
ATTN_TK = 512

V7X_VMEM_LIMIT_BYTES = 56 * 1024 * 1024

_A_U, _A_V, _A_Z = 0, 256, 512
_B_Q, _B_K, _B_V, _B_Z, _B_BETA, _B_ALPHA = 768, 1024, 1280, 1536, 1792, 1800
_C_Q, _C_K, _C_V, _C_Z = 1808, 2064, 2192, 2320
_D_X, _D_Z = 2576, 2832
IN_COLS = 3088
MAIN_COLS = 2176


def _cparams(semantics):
    return pltpu.CompilerParams(dimension_semantics=semantics,
                                vmem_limit_bytes=V7X_VMEM_LIMIT_BYTES)


def _split2(x):
    hi = x.astype(BF16)
    lo = (x - hi.astype(F32)).astype(BF16)
    return hi, lo


def _split3(x):
    hi = x.astype(BF16)
    r = x - hi.astype(F32)
    mid = r.astype(BF16)
    lo = (r - mid.astype(F32)).astype(BF16)
    return hi, mid, lo


def _iota(shape, dim):
    return lax.broadcasted_iota(jnp.int32, shape, dim)


def _ones_where(cond):
    return jnp.where(cond, 1.0, 0.0).astype(BF16)


def _head_sum(x2):
    w = x2.shape[1]
    g = _ones_where(jnp.right_shift(_iota((2 * w, w), 0) & (w - 1), 6)
                    == jnp.right_shift(_iota((2 * w, w), 1), 6))
    hi, lo = _split2(x2)
    return jnp.dot(jnp.concatenate([hi, lo], axis=1), g, preferred_element_type=F32)


def _silu(z):
    return z / (1.0 + jnp.exp(-z))


def _bd_mask():
    return jnp.right_shift(_iota((256, 256), 0), 6) == jnp.right_shift(_iota((256, 256), 1), 6)


def _bd(x, mask01):
    return jnp.concatenate([x, x, x, x], axis=0) * mask01


def _proj_kernel(xp_ref, xc_ref, xn_ref, nw_ref, wh_ref, wm_ref, sguw_ref, sgub_ref, convw_ref,
                 dnrow_ref, qw_ref, kw_ref, rc_ref, rs_ref, poolw_ref, pools_ref,
                 ya_ref, yd_ref, dq_ref, dk_ref, dv_ref, bg_ref, gates_ref, aqt_ref, ak_ref, avt_ref,
                 hext_ref, *, tm, tq, n_tiles, seq_len):
    i = pl.program_id(1)
    n_ext = tm + 2 * HALO
    nw = nw_ref[...]

    def norm(x):
        ms = jnp.mean(x * x, axis=-1, keepdims=True)
        return x * lax.rsqrt(ms + EPS) * nw

    hext_ref[0:HALO, :] = jnp.where(i > 0, norm(xp_ref[0]), 0.0).astype(BF16)
    hext_ref[HALO:HALO + tm, :] = norm(xc_ref[0]).astype(BF16)
    hext_ref[HALO + tm:n_ext, :] = jnp.where(i < n_tiles - 1, norm(xn_ref[0]), 0.0).astype(BF16)

    ph = jnp.dot(hext_ref[...], wh_ref[...], preferred_element_type=F32)
    pm = jnp.dot(hext_ref[HALO:HALO + tm, :], wm_ref[...], preferred_element_type=F32)

    def rows(x):
        return x[HALO:HALO + tm]

    a_u, a_v, a_z = pm[:, 0:256], pm[:, 256:512], pm[:, 512:768]
    vn = a_v * lax.rsqrt(_head_sum(a_v * a_v) * (1.0 / HEAD) + EPS)
    mask4 = _ones_where(jnp.right_shift(_iota((512, 256), 0), 7) == jnp.right_shift(_iota((512, 256), 1), 6))
    mixed = []
    for c in range(tm // SGU_CHUNK):
        vc = vn[c * SGU_CHUNK:(c + 1) * SGU_CHUNK].astype(BF16)
        bdv = jnp.concatenate([vc, vc, vc, vc], axis=0) * mask4
        mixed.append(jnp.dot(sguw_ref[...], bdv, preferred_element_type=F32) + sgub_ref[...])
    mixed = jnp.concatenate(mixed, axis=0)
    ya_ref[0] = (a_u * mixed * _silu(a_z)).astype(BF16)

    xd = ph[:, 768:1024]
    a1 = xd + pltpu.roll(xd, n_ext - 1, 0)
    a2 = a1 + pltpu.roll(a1, n_ext - 2, 0)
    a3 = a2 + pltpu.roll(a2, n_ext - 4, 0)
    a4 = a3 + pltpu.roll(a3, n_ext - 8, 0)
    w2 = rows(pltpu.roll(a1, 1, 0))
    w4 = rows(pltpu.roll(a2, 2, 0))
    w8 = rows(pltpu.roll(a3, 4, 0))
    w16 = rows(pltpu.roll(a4, 8, 0))
    grp = jnp.right_shift(_iota((tm, 256), 1), 6)
    half = jnp.left_shift(jnp.ones((tm, 256), jnp.int32), grp)
    t = i * tm + _iota((tm, 256), 0)
    cnt = (jnp.minimum(t + half, seq_len) - jnp.maximum(t - half, 0)).astype(F32)
    win = jnp.where(grp == 0, w2, jnp.where(grp == 1, w4, jnp.where(grp == 2, w8, w16)))
    diff = win / cnt - rows(xd)
    yd = jnp.dot(diff.astype(BF16), poolw_ref[...], preferred_element_type=F32) * pools_ref[...]
    yd_ref[0] = (yd * _silu(pm[:, 1792:2048])).astype(BF16)

    xb = ph[:, 0:768]
    cw = convw_ref[...]
    conv = (rows(pltpu.roll(xb, 2, 0)) * cw[0:1] + rows(pltpu.roll(xb, 1, 0)) * cw[1:2]
            + rows(xb) * cw[2:3] + rows(pltpu.roll(xb, n_ext - 1, 0)) * cw[3:4]
            + rows(pltpu.roll(xb, n_ext - 2, 0)) * cw[4:5])
    act = _silu(conv)
    bq, bk = act[:, 0:256], act[:, 256:512]
    dq_ref[0] = (bq * lax.rsqrt(_head_sum(bq * bq) + EPS) * (HEAD ** -0.5)).astype(BF16)
    dk_ref[0] = (bk * lax.rsqrt(_head_sum(bk * bk) + EPS)).astype(BF16)
    dv_ref[0] = act[:, 512:768].astype(BF16)
    ba = pm[:, 2048:2176]
    lane = _iota((tm, 128), 1)
    xa = ba + dnrow_ref[1:2, :]
    softplus = jnp.maximum(xa, 0.0) + jnp.log1p(jnp.exp(-jnp.abs(xa)))
    g = -jnp.exp(dnrow_ref[0:1, :]) * softplus
    bg_ref[0] = jnp.where(lane < 8, 1.0 / (1.0 + jnp.exp(-ba)), g)
    gates_ref[0] = jnp.concatenate([_silu(pm[:, 768:1024]), _silu(pm[:, 1536:1792])], axis=1).astype(BF16)

    rc, rs = rc_ref[...], rs_ref[...]
    first = (_iota((tm, 128), 1) & 16) == 0

    def rope(x):
        sw = jnp.where(first, pltpu.roll(x, 112, 1), pltpu.roll(x, 16, 1))
        return x * rc + sw * rs

    cq, ck = pm[:, 1024:1280], pm[:, 1280:1408]
    qn = cq * lax.rsqrt(_head_sum(cq * cq) * (1.0 / HEAD) + EPS) * qw_ref[...]
    qr = jnp.concatenate([rope(qn[:, 0:128]), rope(qn[:, 128:256])], axis=1) * (HEAD ** -0.5 * LOG2_E)
    qt = qr.T.astype(BF16)
    zq = jnp.zeros((HEAD, 2 * tq), BF16)
    for r in range(tm // tq):
        cs = slice(r * tq, (r + 1) * tq)
        top0 = jnp.concatenate([qt[0:64, cs], qt[64:128, cs]], axis=1)
        top1 = jnp.concatenate([qt[128:192, cs], qt[192:256, cs]], axis=1)
        aqt_ref[0, 0, r] = jnp.concatenate([top0, zq], axis=0)
        aqt_ref[0, 1, r] = jnp.concatenate([zq, top1], axis=0)
    kn = ck * lax.rsqrt(_head_sum(ck * ck) * (1.0 / HEAD) + EPS) * kw_ref[...]
    ak_ref[0] = rope(kn).astype(BF16)
    vt = pm[:, 1408:1536].T
    ones = jnp.ones((16, tm), F32)
    for g in range(2):
        vg = jnp.concatenate([vt[g * HEAD:(g + 1) * HEAD], ones], axis=0).astype(BF16)
        for r in range(tm // ATTN_TK):
            avt_ref[0, g, r] = vg[:, r * ATTN_TK:(r + 1) * ATTN_TK]


def _proj_call(x, lw, rope_c, rope_s, tm, tq):
    b, s, _ = x.shape
    n_tiles = s // tm
    hb = tm // HALO
    last_hb = s // HALO - 1
    full = lambda shape: pl.BlockSpec(shape, lambda bi, i: (0,) * len(shape))
    tok = lambda w: pl.BlockSpec((1, tm, w), lambda bi, i: (bi, i, 0))
    in_specs = [
        pl.BlockSpec((1, HALO, D_MODEL), lambda bi, i: (bi, jnp.maximum(i * hb - 1, 0), 0)),
        pl.BlockSpec((1, tm, D_MODEL), lambda bi, i: (bi, i, 0)),
        pl.BlockSpec((1, HALO, D_MODEL), lambda bi, i: (bi, jnp.minimum((i + 1) * hb, last_hb), 0)),
        full((1, D_MODEL)), full((D_MODEL, 1024)), full((D_MODEL, MAIN_COLS)),
        full((128, 512)), full((128, 256)), full((8, 768)), full((8, 128)),
        full((1, 256)), full((1, 128)),
        pl.BlockSpec((tm, 128), lambda bi, i: (i, 0)), pl.BlockSpec((tm, 128), lambda bi, i: (i, 0)),
        full((256, 256)), full((1, 256)),
    ]
    out_shape = [
        jax.ShapeDtypeStruct((b, s, 256), BF16),
        jax.ShapeDtypeStruct((b, s, 256), BF16),
        jax.ShapeDtypeStruct((b, s, 256), BF16),
        jax.ShapeDtypeStruct((b, s, 256), BF16),
        jax.ShapeDtypeStruct((b, s, 256), BF16),
        jax.ShapeDtypeStruct((b, s, 128), F32),
        jax.ShapeDtypeStruct((b, s, 512), BF16),
        jax.ShapeDtypeStruct((b, 2, s // tq, 128, 2 * tq), BF16),
        jax.ShapeDtypeStruct((b, s, 128), BF16),
        jax.ShapeDtypeStruct((b, 2, s // ATTN_TK, HEAD + 16, ATTN_TK), BF16),
    ]
    out_specs = [tok(256), tok(256), tok(256), tok(256), tok(256), tok(128), tok(512),
                 pl.BlockSpec((1, 2, tm // tq, 128, 2 * tq), lambda bi, i: (bi, 0, i, 0, 0)), tok(128),
                 pl.BlockSpec((1, 2, tm // ATTN_TK, HEAD + 16, ATTN_TK), lambda bi, i: (bi, 0, i, 0, 0))]
    return pl.pallas_call(
        functools.partial(_proj_kernel, tm=tm, tq=tq, n_tiles=n_tiles, seq_len=s),
        grid=(b, n_tiles), in_specs=in_specs, out_specs=out_specs, out_shape=out_shape,
        scratch_shapes=[pltpu.VMEM((tm + 2 * HALO, D_MODEL), BF16)],
        compiler_params=_cparams(("parallel", "parallel")), name="proj",
    )(x, x, x, lw["norm_w"], lw["w_halo"], lw["w_main"], lw["sgu_w"], lw["sgu_b"], lw["conv_w"],
      lw["dn_row"], lw["q_norm_w"], lw["k_norm_w"], rope_c, rope_s, lw["pool_w"], lw["pool_scale"])


def _dn_pre_kernel(q_ref, k_ref, v_ref, bg_ref,
                   uf_ref, wf_ref, qdf_ref, inf_ref, kdtf_ref, eglf_ref,
                   ub_ref, wb_ref, qdb_ref, inb_ref, kdtb_ref, eglb_ref, *, ta):
    nc = ta // DN_CHUNK
    q = q_ref[0].astype(F32)
    k = k_ref[0].astype(F32)
    v = v_ref[0].astype(F32)
    bg = bg_ref[0]

    r = _iota((ta, ta), 0)
    c = _iota((ta, ta), 1)
    same = jnp.right_shift(r, 6) == jnp.right_shift(c, 6)
    m_lin = _ones_where(same & (c <= r))
    m_uin = _ones_where(same & (c >= r))
    m_ust = _ones_where(same & (c > r))
    m_lst = _ones_where(same & (c < r))
    m4 = jnp.concatenate([m_lin, m_uin, m_ust, m_lst], axis=0)
    cs = jnp.dot(jnp.concatenate([m4, m4, m4], axis=1), jnp.concatenate(_split3(bg), axis=0),
                 preferred_element_type=F32)
    lane = _iota((ta, 128), 1)
    nb1 = jnp.where(lane < 8, bg, jnp.where(lane < 12, cs[0:ta], cs[ta:2 * ta]))
    nb2 = jnp.where(lane < 12, cs[2 * ta:3 * ta], cs[3 * ta:4 * ta])
    src = _iota((384, 1024), 0) & 127
    e1 = _ones_where(src == jnp.right_shift(_iota((384, 1024), 1), 6))
    x1 = jnp.dot(jnp.concatenate(_split3(nb1), axis=1), e1, preferred_element_type=F32)
    src2 = _iota((384, 512), 0) & 127
    e2 = _ones_where(src2 == jnp.right_shift(_iota((384, 512), 1), 6) + 8)
    x2 = jnp.dot(jnp.concatenate(_split3(nb2), axis=1), e2, preferred_element_type=F32)

    li = _iota((ta, 256), 0) & 63
    lj = _iota((ta, 256), 1) & 63
    eyecat = li == lj
    eyef = jnp.where(eyecat, 1.0, 0.0)
    j3 = _ones_where(same)
    j3 = jnp.concatenate([j3, j3, j3], axis=1)
    bdm = _ones_where(_bd_mask())

    gram, qk = [], []
    for ci in range(nc):
        sl = slice(ci * DN_CHUNK, (ci + 1) * DN_CHUNK)
        kc = k_ref[0, sl, :]
        gq = lax.dot_general(jnp.concatenate([kc, q_ref[0, sl, :]], axis=0), _bd(kc, bdm),
                             (((1,), (1,)), ((), ())), preferred_element_type=F32)
        gram.append(gq[0:64])
        qk.append(gq[64:128])
    gram = jnp.concatenate(gram, axis=0)
    qk = jnp.concatenate(qk, axis=0)

    outs = ((uf_ref, wf_ref, qdf_ref, inf_ref, kdtf_ref, eglf_ref),
            (ub_ref, wb_ref, qdb_ref, inb_ref, kdtb_ref, eglb_ref))
    chains = []
    for d in range(2):
        u_ref, w_ref, qd_ref, in_ref, kdt_ref, egl_ref = outs[d]
        bx = x1[:, d * 256:(d + 1) * 256]
        gcx = x1[:, 512 + d * 256:768 + d * 256]
        dglx = x2[:, d * 256:(d + 1) * 256]
        rowf = jnp.dot(j3, jnp.concatenate(_split3(jnp.where(eyecat, gcx, 0.0)), axis=0),
                       preferred_element_type=F32)
        incl = (li >= lj) if d == 0 else (li <= lj)
        strict = (li > lj) if d == 0 else (li < lj)
        decay = jnp.exp(jnp.where(incl, gcx - rowf, NEG_BIG))
        eg = jnp.exp(gcx)
        a_all = jnp.where(strict, bx * gram * decay, 0.0)
        in_ref[0] = jnp.where(incl, qk * decay, 0.0).astype(BF16)
        qd_ref[0] = (q * eg).astype(BF16)
        kdt_ref[0] = (k * jnp.exp(dglx)).T.astype(BF16)
        vb = (v * bx).astype(BF16)
        kbg = (k * bx * eg).astype(BF16)
        for ci in range(nc):
            sl = slice(ci * DN_CHUNK, (ci + 1) * DN_CHUNK)
            last = ci * DN_CHUNK + (DN_CHUNK - 1 if d == 0 else 0)
            egl_ref[0, ci * 8:(ci + 1) * 8, :] = jnp.broadcast_to(eg[last:last + 1, :], (8, 256))
            chains.append(dict(sl=sl, u_ref=u_ref, w_ref=w_ref, ak=a_all[sl], p=eyef[sl] - a_all[sl],
                               vb=vb[sl], kbg=kbg[sl]))

    for stage in range(5):
        for ch in chains:
            akb = ch["ak"].astype(BF16)
            lhs = akb if stage == 0 else jnp.concatenate([akb, ch["p"].astype(BF16)], axis=0)
            res = jnp.dot(lhs, _bd(akb, bdm), preferred_element_type=F32)
            ch["ak"] = res[0:64]
            if stage > 0:
                ch["p"] = ch["p"] + res[64:128]
    for ch in chains:
        ch["p"] = ch["p"] + jnp.dot(ch["p"].astype(BF16), _bd(ch["ak"].astype(BF16), bdm),
                                    preferred_element_type=F32)
    for ch in chains:
        rhs = jnp.concatenate([_bd(ch["vb"], bdm), _bd(ch["kbg"], bdm)], axis=1)
        uw = jnp.dot(ch["p"].astype(BF16), rhs, preferred_element_type=F32)
        ch["u_ref"][0, ch["sl"], :] = uw[:, 0:256].astype(BF16)
        ch["w_ref"][0, ch["sl"], :] = uw[:, 256:512].astype(BF16)


def _dn_pre_call(dq, dk, dv, bg, ta):
    b, s, _ = dq.shape
    tok = lambda w: pl.BlockSpec((1, ta, w), lambda bi, i: (bi, i, 0))
    per_dir_shapes = [jax.ShapeDtypeStruct((b, s, 256), BF16)] * 4 + [
        jax.ShapeDtypeStruct((b, 256, s), BF16), jax.ShapeDtypeStruct((b, s // 8, 256), F32)]
    per_dir_specs = [tok(256)] * 4 + [pl.BlockSpec((1, 256, ta), lambda bi, i: (bi, 0, i)),
                                      pl.BlockSpec((1, ta // 8, 256), lambda bi, i: (bi, i, 0))]
    return pl.pallas_call(
        functools.partial(_dn_pre_kernel, ta=ta),
        grid=(b, s // ta), in_specs=[tok(256), tok(256), tok(256), tok(128)],
        out_specs=per_dir_specs * 2, out_shape=per_dir_shapes * 2,
        compiler_params=_cparams(("parallel", "parallel")), name="dn_pre",
    )(dq, dk, dv, bg)


def _dn_scan_kernel(uf_ref, wf_ref, qdf_ref, inf_ref, kdtf_ref, eglf_ref,
                    ub_ref, wb_ref, qdb_ref, inb_ref, kdtb_ref, eglb_ref,
                    of_ref, ob_ref, state_ref, *, nb, tb):
    nc = tb // DN_CHUNK

    @pl.when(pl.program_id(0) == 0)
    def _():
        state_ref[...] = jnp.zeros_like(state_ref)

    bdm = _bd_mask()
    bdm01 = _ones_where(bdm)
    zpad = jnp.zeros((DN_CHUNK, 256), BF16)
    dirs = ((uf_ref, wf_ref, qdf_ref, inf_ref, kdtf_ref, eglf_ref, of_ref),
            (ub_ref, wb_ref, qdb_ref, inb_ref, kdtb_ref, eglb_ref, ob_ref))
    for step in range(nc):
        for bi in range(nb):
            for d in range(2):
                u_ref, w_ref, qd_ref, in_ref, kdt_ref, egl_ref, o_ref = dirs[d]
                ci = step if d == 0 else nc - 1 - step
                sl = slice(ci * DN_CHUNK, (ci + 1) * DN_CHUNK)
                si = d * nb + bi
                state = state_ref[si]
                res = jnp.dot(jnp.concatenate([w_ref[bi, sl, :], qd_ref[bi, sl, :]], axis=0),
                              state.astype(BF16), preferred_element_type=F32)
                v_new = (u_ref[bi, sl, :].astype(F32) - res[0:64]).astype(BF16)
                o_ref[bi, sl, :] = res[64:128] + jnp.dot(in_ref[bi, sl, :], _bd(v_new, bdm01),
                                                         preferred_element_type=F32)
                pair = ci // 2
                kpair = kdt_ref[bi, :, pair * 128:(pair + 1) * 128]
                vpad = jnp.concatenate([v_new, zpad] if ci % 2 == 0 else [zpad, v_new], axis=0)
                ds = jnp.dot(kpair, vpad, preferred_element_type=F32)
                state_ref[si] = state * egl_ref[bi, ci * 8:ci * 8 + 1, :] + jnp.where(bdm, ds, 0.0)


def _dn_scan_call(pre, tb):
    b, s, _ = pre[0].shape
    n = s // tb
    fwd = lambda w: pl.BlockSpec((b, tb, w), lambda i: (0, i, 0))
    bwd = lambda w: pl.BlockSpec((b, tb, w), lambda i: (0, n - 1 - i, 0))
    specs_f = [fwd(256)] * 4 + [pl.BlockSpec((b, 256, tb), lambda i: (0, 0, i)),
                                pl.BlockSpec((b, tb // 8, 256), lambda i: (0, i, 0))]
    specs_b = [bwd(256)] * 4 + [pl.BlockSpec((b, 256, tb), lambda i: (0, 0, n - 1 - i)),
                                pl.BlockSpec((b, tb // 8, 256), lambda i: (0, n - 1 - i, 0))]
    return pl.pallas_call(
        functools.partial(_dn_scan_kernel, nb=b, tb=tb),
        grid=(n,), in_specs=specs_f + specs_b, out_specs=[fwd(256), bwd(256)],
        out_shape=[jax.ShapeDtypeStruct((b, s, 256), F32)] * 2,
        scratch_shapes=[pltpu.VMEM((2 * b, 256, 256), F32)],
        compiler_params=_cparams(("arbitrary",)), name="dn_scan",
    )(*pre)


def _attn_kernel(qt_ref, k_ref, vt_ref, o_ref, st_ref, *, tq, tk, n_q, n_kv, unroll):
    per_q = n_kv // unroll
    n_trips = n_q * per_q

    def scores(qi, j):
        kj = k_ref[0, pl.ds(pl.multiple_of(j * tk, tk), tk), :]
        return jnp.dot(kj, qt_ref[0, 0, qi], preferred_element_type=F32)

    def update(j, st, m, acc):
        m_new = jnp.maximum(m, jnp.max(st, axis=0, keepdims=True))
        p = jnp.exp2(st - m_new).astype(BF16)
        alpha = jnp.exp2(m - m_new)
        return m_new, alpha * acc + jnp.dot(vt_ref[0, 0, j], p, preferred_element_type=F32)

    st_ref[0] = scores(0, 0)

    def body(t, carry):
        qi = t // per_q
        base = (t - qi * per_q) * unroll
        first = base == 0
        m = jnp.where(first, -jnp.inf, carry[0])
        acc = jnp.where(first, 0.0, carry[1])
        t_next = jnp.minimum(t + 1, n_trips - 1)
        qi_next = t_next // per_q
        base_next = (t_next - qi_next * per_q) * unroll
        for r in range(unroll):
            if r < unroll - 1:
                st_ref[(r + 1) % 2] = scores(qi, base + r + 1)
            else:
                st_ref[0] = scores(qi_next, base_next)
            m, acc = update(base + r, st_ref[r % 2], m, acc)

        @pl.when(base == n_kv - unroll)
        def _():
            o = acc[0:HEAD] / acc[HEAD:HEAD + 1]
            ot = jnp.concatenate([o[:, 0:tq], o[:, tq:2 * tq]], axis=0)
            o_ref[0, pl.ds(pl.multiple_of(qi * tq, tq), tq), :] = ot.T.astype(BF16)

        return m, acc

    m0 = jnp.full((1, 2 * tq), -jnp.inf, F32)
    a0 = jnp.zeros((HEAD + 16, 2 * tq), F32)
    lax.fori_loop(0, n_trips, body, (m0, a0))


def _attn_call(aqt, ak, avt):
    b, s, _ = ak.shape
    _, _, n_q, _, tq2 = aqt.shape
    _, _, n_kv, vrows, tk = avt.shape
    unroll = 8 if n_kv % 8 == 0 else (4 if n_kv % 4 == 0 else 2)
    return pl.pallas_call(
        functools.partial(_attn_kernel, tq=tq2 // 2, tk=tk, n_q=n_q, n_kv=n_kv, unroll=unroll),
        grid=(b, 2),
        in_specs=[pl.BlockSpec((1, 1, n_q, 128, tq2), lambda bi, g: (bi, g, 0, 0, 0)),
                  pl.BlockSpec((1, s, 128), lambda bi, g: (bi, 0, 0)),
                  pl.BlockSpec((1, 1, n_kv, vrows, tk), lambda bi, g: (bi, g, 0, 0, 0))],
        out_specs=pl.BlockSpec((1, s, 128), lambda bi, g: (bi, 0, g)),
        out_shape=jax.ShapeDtypeStruct((b, s, 256), BF16),
        scratch_shapes=[pltpu.VMEM((2, tk, tq2), F32)],
        compiler_params=_cparams(("parallel", "parallel")), name="attn",
    )(aqt, ak, avt)


def _out_kernel(x_ref, ya_ref, yd_ref, yc_ref, of_ref, ob_ref, gates_ref, dnw_ref, wo_ref, o_ref):
    o = of_ref[0] + ob_ref[0]
    gates = gates_ref[0].astype(F32)
    on = o * lax.rsqrt(_head_sum(o * o) * (1.0 / HEAD) + EPS) * dnw_ref[...]
    yb = (on * gates[:, 0:256]).astype(BF16)
    yc = (yc_ref[0].astype(F32) * gates[:, 256:512]).astype(BF16)
    mix = jnp.concatenate([ya_ref[0], yb, yc, yd_ref[0]], axis=1)
    o_ref[0] = x_ref[0] + jnp.dot(mix, wo_ref[...], preferred_element_type=F32)


def _out_call(x, ya, yd, yc, o_f, o_b, gates, lw, tm):
    b, s, _ = x.shape
    tok = lambda w: pl.BlockSpec((1, tm, w), lambda bi, i: (bi, i, 0))
    full = lambda shape: pl.BlockSpec(shape, lambda bi, i: (0,) * len(shape))
    return pl.pallas_call(
        _out_kernel, grid=(b, s // tm),
        in_specs=[tok(D_MODEL), tok(256), tok(256), tok(256), tok(256), tok(256), tok(512),
                  full((1, 256)), full((D_MODEL, D_MODEL))],
        out_specs=tok(D_MODEL), out_shape=jax.ShapeDtypeStruct((b, s, D_MODEL), F32),
        compiler_params=_cparams(("parallel", "parallel")), name="out",
    )(x, ya, yd, yc, o_f, o_b, gates, lw["dn_norm_w"], lw["w_out"])


def _rope_tables(seq_len):
    t = jnp.arange(seq_len)
    pos = jnp.stack([t // GRID_W, t % GRID_W], axis=-1).astype(F32)
    n_freq = HEAD // 4
    inv_freq = jnp.power(ROPE_THETA, -2.0 * jnp.arange(n_freq, dtype=F32) / (HEAD // 2))
    ang = pos[:, :, None] * inv_freq
    cos = jnp.repeat(jnp.cos(ang)[:, :, None, :], 2, axis=2).reshape(seq_len, HEAD)
    sin = jnp.sin(ang)
    sin = jnp.stack([-sin, sin], axis=2).reshape(seq_len, HEAD)
    return jnp.tile(cos, (1, 2)), jnp.tile(sin, (1, 2))


def _layer_weights(l, norm_w, w_in, sgu_w, sgu_b, conv_w, a_log, dt_bias, dn_norm_w, q_norm_w,
                   k_norm_w, pool_w, pool_scale, w_out):
    w = w_in[l]
    cols = lambda a, n: w[:, a:a + n]
    w_halo = jnp.concatenate([cols(_B_Q, 768), cols(_D_X, 256)], axis=1).astype(BF16)
    w_main = jnp.concatenate([
        cols(_A_U, 768), cols(_B_Z, 256), cols(_C_Q, 768), cols(_D_Z, 256), cols(_B_BETA, 16),
        jnp.zeros((D_MODEL, 112), F32)], axis=1).astype(BF16)
    dn_row = jnp.zeros((8, 128), F32)
    dn_row = dn_row.at[0, 8:16].set(a_log[l].reshape(8)).at[1, 8:16].set(dt_bias[l].reshape(8))
    pool_bd = jnp.zeros((256, 256), F32)
    for gi in range(len(POOL_WINDOWS)):
        pool_bd = pool_bd.at[gi * 64:(gi + 1) * 64, gi * 64:(gi + 1) * 64].set(pool_w[l, gi])
    return {
        "norm_w": norm_w[l].reshape(1, D_MODEL),
        "w_halo": w_halo, "w_main": w_main,
        "sgu_w": jnp.transpose(sgu_w[l], (1, 0, 2)).reshape(SGU_CHUNK, 4 * SGU_CHUNK).astype(BF16),
        "sgu_b": jnp.repeat(sgu_b[l].T, HEAD, axis=1),
        "conv_w": jnp.concatenate([conv_w[l], jnp.zeros((3, 768), F32)], axis=0),
        "dn_row": dn_row,
        "q_norm_w": jnp.tile(q_norm_w[l], 4).reshape(1, 256),
        "k_norm_w": jnp.tile(k_norm_w[l], 2).reshape(1, 128),
        "pool_w": pool_bd.astype(BF16),
        "pool_scale": pool_scale[l].reshape(1, 256),
        "dn_norm_w": jnp.tile(dn_norm_w[l], 4).reshape(1, 256),
        "w_out": w_out[l].astype(BF16),
    }


def _tiles(seq_len):
    tm = min(512, seq_len)
    return dict(tm=tm, ta=min(256, seq_len), tb=min(256, seq_len), tq=min(256, seq_len))


def _layer(x, lw, rope_c, rope_s):
    t = _tiles(x.shape[1])
    ya, yd, dq, dk, dv, bg, gates, aqt, ak, avt = _proj_call(x, lw, rope_c, rope_s, t["tm"], t["tq"])
    pre = _dn_pre_call(dq, dk, dv, bg, t["ta"])
    o_f, o_b = _dn_scan_call(pre, t["tb"])
    yc = _attn_call(aqt, ak, avt)
    return _out_call(x, ya, yd, yc, o_f, o_b, gates, lw, t["tm"])


def kernel(x_prompt, x_sample, norm_w, w_in, sgu_w, sgu_b, conv_w, a_log, dt_bias, dn_norm_w,
           q_norm_w, k_norm_w, pool_w, pool_scale, w_out):
    depth = norm_w.shape[0]
    rope_p = _rope_tables(x_prompt.shape[1])
    rope_s = _rope_tables(x_sample.shape[1])
    y_prompt, y_sample = x_prompt, x_sample
    for l in range(depth):
        lw = _layer_weights(l, norm_w, w_in, sgu_w, sgu_b, conv_w, a_log, dt_bias, dn_norm_w,
                            q_norm_w, k_norm_w, pool_w, pool_scale, w_out)
        y_prompt = _layer(y_prompt, lw, *rope_p)
        y_sample = _layer(y_sample, lw, *rope_s)
    return (y_prompt, y_sample)
```

```python
import functools
import math

import numpy as np
import jax
import jax.numpy as jnp
from jax import lax
from jax.experimental import pallas as pl
from jax.experimental.pallas import tpu as pltpu

F32 = jnp.float32
BF16 = jnp.bfloat16

D_MODEL = 1024
GROUP_W = 256
HEAD = 64
N_HEADS = 4
GRID_W = 64
EPS = 1e-6
SGU_CHUNK = 128
DN_CHUNK = 64
CONV_K = 5
ROPE_THETA = 10000.0
POOL_WINDOWS = (2, 4, 8, 16)
HALO = 16
NEG_BIG = -1e30
LOG2_E = 1.4426950408889634
ATTN_TK = 512

V7X_VMEM_LIMIT_BYTES = 56 * 1024 * 1024

_A_U, _A_V, _A_Z = 0, 256, 512
_B_Q, _B_K, _B_V, _B_Z, _B_BETA, _B_ALPHA = 768, 1024, 1280, 1536, 1792, 1800
_C_Q, _C_K, _C_V, _C_Z = 1808, 2064, 2192, 2320
_D_X, _D_Z = 2576, 2832
IN_COLS = 3088
MAIN_COLS = 2176


def _cparams(semantics):
    return pltpu.CompilerParams(dimension_semantics=semantics,
                                vmem_limit_bytes=V7X_VMEM_LIMIT_BYTES)


def _split2(x):
    hi = x.astype(BF16)
    lo = (x - hi.astype(F32)).astype(BF16)
    return hi, lo


def _split3(x):
    hi = x.astype(BF16)
    r = x - hi.astype(F32)
    mid = r.astype(BF16)
    lo = (r - mid.astype(F32)).astype(BF16)
    return hi, mid, lo


def _iota(shape, dim):
    return lax.broadcasted_iota(jnp.int32, shape, dim)


def _ones_where(cond):
    return jnp.where(cond, 1.0, 0.0).astype(BF16)


def _head_sum(x2):
    w = x2.shape[1]
    g = _ones_where(jnp.right_shift(_iota((2 * w, w), 0) & (w - 1), 6)
                    == jnp.right_shift(_iota((2 * w, w), 1), 6))
    hi, lo = _split2(x2)
    return jnp.dot(jnp.concatenate([hi, lo], axis=1), g, preferred_element_type=F32)


def _silu(z):
    return z / (1.0 + jnp.exp(-z))


def _bd_mask():
    return jnp.right_shift(_iota((256, 256), 0), 6) == jnp.right_shift(_iota((256, 256), 1), 6)


def _bd(x, mask01):
    return jnp.concatenate([x, x, x, x], axis=0) * mask01


def _proj_kernel(xp_ref, xc_ref, xn_ref, nw_ref, wh_ref, wm_ref, sguw_ref, sgub_ref, convw_ref,
                 dnrow_ref, qw_ref, kw_ref, rc_ref, rs_ref, poolw_ref, pools_ref,
                 ya_ref, yd_ref, dq_ref, dk_ref, dv_ref, bg_ref, gates_ref, aqt_ref, ak_ref, avt_ref,
                 hext_ref, *, tm, tq, n_tiles, seq_len):
    i = pl.program_id(1)
    n_ext = tm + 2 * HALO
    nw = nw_ref[...]

    def norm(x):
        ms = jnp.mean(x * x, axis=-1, keepdims=True)
        return x * lax.rsqrt(ms + EPS) * nw

    hext_ref[0:HALO, :] = jnp.where(i > 0, norm(xp_ref[0]), 0.0).astype(BF16)
    hext_ref[HALO:HALO + tm, :] = norm(xc_ref[0]).astype(BF16)
    hext_ref[HALO + tm:n_ext, :] = jnp.where(i < n_tiles - 1, norm(xn_ref[0]), 0.0).astype(BF16)

    ph = jnp.dot(hext_ref[...], wh_ref[...], preferred_element_type=F32)
    pm = jnp.dot(hext_ref[HALO:HALO + tm, :], wm_ref[...], preferred_element_type=F32)

    def rows(x):
        return x[HALO:HALO + tm]

    a_u, a_v, a_z = pm[:, 0:256], pm[:, 256:512], pm[:, 512:768]
    vn = a_v * lax.rsqrt(_head_sum(a_v * a_v) * (1.0 / HEAD) + EPS)
    mask4 = _ones_where(jnp.right_shift(_iota((512, 256), 0), 7) == jnp.right_shift(_iota((512, 256), 1), 6))
    mixed = []
    for c in range(tm // SGU_CHUNK):
        vc = vn[c * SGU_CHUNK:(c + 1) * SGU_CHUNK].astype(BF16)
        bdv = jnp.concatenate([vc, vc, vc, vc], axis=0) * mask4
        mixed.append(jnp.dot(sguw_ref[...], bdv, preferred_element_type=F32) + sgub_ref[...])
    mixed = jnp.concatenate(mixed, axis=0)
    ya_ref[0] = (a_u * mixed * _silu(a_z)).astype(BF16)

    xd = ph[:, 768:1024]
    a1 = xd + pltpu.roll(xd, n_ext - 1, 0)
    a2 = a1 + pltpu.roll(a1, n_ext - 2, 0)
    a3 = a2 + pltpu.roll(a2, n_ext - 4, 0)
    a4 = a3 + pltpu.roll(a3, n_ext - 8, 0)
    w2 = rows(pltpu.roll(a1, 1, 0))
    w4 = rows(pltpu.roll(a2, 2, 0))
    w8 = rows(pltpu.roll(a3, 4, 0))
    w16 = rows(pltpu.roll(a4, 8, 0))
    grp = jnp.right_shift(_iota((tm, 256), 1), 6)
    half = jnp.left_shift(jnp.ones((tm, 256), jnp.int32), grp)
    t = i * tm + _iota((tm, 256), 0)
    cnt = (jnp.minimum(t + half, seq_len) - jnp.maximum(t - half, 0)).astype(F32)
    win = jnp.where(grp == 0, w2, jnp.where(grp == 1, w4, jnp.where(grp == 2, w8, w16)))
    diff = win / cnt - rows(xd)
    yd = jnp.dot(diff.astype(BF16), poolw_ref[...], preferred_element_type=F32) * pools_ref[...]
    yd_ref[0] = (yd * _silu(pm[:, 1792:2048])).astype(BF16)

    xb = ph[:, 0:768]
    cw = convw_ref[...]
    conv = (rows(pltpu.roll(xb, 2, 0)) * cw[0:1] + rows(pltpu.roll(xb, 1, 0)) * cw[1:2]
            + rows(xb) * cw[2:3] + rows(pltpu.roll(xb, n_ext - 1, 0)) * cw[3:4]
            + rows(pltpu.roll(xb, n_ext - 2, 0)) * cw[4:5])
    act = _silu(conv)
    bq, bk = act[:, 0:256], act[:, 256:512]
    dq_ref[0] = (bq * lax.rsqrt(_head_sum(bq * bq) + EPS) * (HEAD ** -0.5)).astype(BF16)
    dk_ref[0] = (bk * lax.rsqrt(_head_sum(bk * bk) + EPS)).astype(BF16)
    dv_ref[0] = act[:, 512:768].astype(BF16)
    ba = pm[:, 2048:2176]
    lane = _iota((tm, 128), 1)
    xa = ba + dnrow_ref[1:2, :]
    softplus = jnp.maximum(xa, 0.0) + jnp.log1p(jnp.exp(-jnp.abs(xa)))
    g = -jnp.exp(dnrow_ref[0:1, :]) * softplus
    bg_ref[0] = jnp.where(lane < 8, 1.0 / (1.0 + jnp.exp(-ba)), g)
    gates_ref[0] = jnp.concatenate([_silu(pm[:, 768:1024]), _silu(pm[:, 1536:1792])], axis=1).astype(BF16)

    rc, rs = rc_ref[...], rs_ref[...]
    first = (_iota((tm, 128), 1) & 16) == 0

    def rope(x):
        sw = jnp.where(first, pltpu.roll(x, 112, 1), pltpu.roll(x, 16, 1))
        return x * rc + sw * rs

    cq, ck = pm[:, 1024:1280], pm[:, 1280:1408]
    qn = cq * lax.rsqrt(_head_sum(cq * cq) * (1.0 / HEAD) + EPS) * qw_ref[...]
    qr = jnp.concatenate([rope(qn[:, 0:128]), rope(qn[:, 128:256])], axis=1) * (HEAD ** -0.5 * LOG2_E)
    qt = qr.T.astype(BF16)
    zq = jnp.zeros((HEAD, 2 * tq), BF16)
    for r in range(tm // tq):
        cs = slice(r * tq, (r + 1) * tq)
        top0 = jnp.concatenate([qt[0:64, cs], qt[64:128, cs]], axis=1)
        top1 = jnp.concatenate([qt[128:192, cs], qt[192:256, cs]], axis=1)
        aqt_ref[0, 0, r] = jnp.concatenate([top0, zq], axis=0)
        aqt_ref[0, 1, r] = jnp.concatenate([zq, top1], axis=0)
    kn = ck * lax.rsqrt(_head_sum(ck * ck) * (1.0 / HEAD) + EPS) * kw_ref[...]
    ak_ref[0] = rope(kn).astype(BF16)
    vt = pm[:, 1408:1536].T
    ones = jnp.ones((16, tm), F32)
    for g in range(2):
        vg = jnp.concatenate([vt[g * HEAD:(g + 1) * HEAD], ones], axis=0).astype(BF16)
        for r in range(tm // ATTN_TK):
            avt_ref[0, g, r] = vg[:, r * ATTN_TK:(r + 1) * ATTN_TK]


def _proj_call(x, lw, rope_c, rope_s, tm, tq):
    b, s, _ = x.shape
    n_tiles = s // tm
    hb = tm // HALO
    last_hb = s // HALO - 1
    full = lambda shape: pl.BlockSpec(shape, lambda bi, i: (0,) * len(shape))
    tok = lambda w: pl.BlockSpec((1, tm, w), lambda bi, i: (bi, i, 0))
    in_specs = [
        pl.BlockSpec((1, HALO, D_MODEL), lambda bi, i: (bi, jnp.maximum(i * hb - 1, 0), 0)),
        pl.BlockSpec((1, tm, D_MODEL), lambda bi, i: (bi, i, 0)),
        pl.BlockSpec((1, HALO, D_MODEL), lambda bi, i: (bi, jnp.minimum((i + 1) * hb, last_hb), 0)),
        full((1, D_MODEL)), full((D_MODEL, 1024)), full((D_MODEL, MAIN_COLS)),
        full((128, 512)), full((128, 256)), full((8, 768)), full((8, 128)),
        full((1, 256)), full((1, 128)),
        pl.BlockSpec((tm, 128), lambda bi, i: (i, 0)), pl.BlockSpec((tm, 128), lambda bi, i: (i, 0)),
        full((256, 256)), full((1, 256)),
    ]
    out_shape = [
        jax.ShapeDtypeStruct((b, s, 256), BF16),
        jax.ShapeDtypeStruct((b, s, 256), BF16),
        jax.ShapeDtypeStruct((b, s, 256), BF16),
        jax.ShapeDtypeStruct((b, s, 256), BF16),
        jax.ShapeDtypeStruct((b, s, 256), BF16),
        jax.ShapeDtypeStruct((b, s, 128), F32),
        jax.ShapeDtypeStruct((b, s, 512), BF16),
        jax.ShapeDtypeStruct((b, 2, s // tq, 128, 2 * tq), BF16),
        jax.ShapeDtypeStruct((b, s, 128), BF16),
        jax.ShapeDtypeStruct((b, 2, s // ATTN_TK, HEAD + 16, ATTN_TK), BF16),
    ]
    out_specs = [tok(256), tok(256), tok(256), tok(256), tok(256), tok(128), tok(512),
                 pl.BlockSpec((1, 2, tm // tq, 128, 2 * tq), lambda bi, i: (bi, 0, i, 0, 0)), tok(128),
                 pl.BlockSpec((1, 2, tm // ATTN_TK, HEAD + 16, ATTN_TK), lambda bi, i: (bi, 0, i, 0, 0))]
    return pl.pallas_call(
        functools.partial(_proj_kernel, tm=tm, tq=tq, n_tiles=n_tiles, seq_len=s),
        grid=(b, n_tiles), in_specs=in_specs, out_specs=out_specs, out_shape=out_shape,
        scratch_shapes=[pltpu.VMEM((tm + 2 * HALO, D_MODEL), BF16)],
        compiler_params=_cparams(("parallel", "parallel")), name="proj",
    )(x, x, x, lw["norm_w"], lw["w_halo"], lw["w_main"], lw["sgu_w"], lw["sgu_b"], lw["conv_w"],
      lw["dn_row"], lw["q_norm_w"], lw["k_norm_w"], rope_c, rope_s, lw["pool_w"], lw["pool_scale"])


def _dn_pre_kernel(q_ref, k_ref, v_ref, bg_ref,
                   uf_ref, wf_ref, qdf_ref, inf_ref, kdtf_ref, eglf_ref,
                   ub_ref, wb_ref, qdb_ref, inb_ref, kdtb_ref, eglb_ref, *, ta):
    nc = ta // DN_CHUNK
    q = q_ref[0].astype(F32)
    k = k_ref[0].astype(F32)
    v = v_ref[0].astype(F32)
    bg = bg_ref[0]

    r = _iota((ta, ta), 0)
    c = _iota((ta, ta), 1)
    same = jnp.right_shift(r, 6) == jnp.right_shift(c, 6)
    m_lin = _ones_where(same & (c <= r))
    m_uin = _ones_where(same & (c >= r))
    m_ust = _ones_where(same & (c > r))
    m_lst = _ones_where(same & (c < r))
    m4 = jnp.concatenate([m_lin, m_uin, m_ust, m_lst], axis=0)
    cs = jnp.dot(jnp.concatenate([m4, m4, m4], axis=1), jnp.concatenate(_split3(bg), axis=0),
                 preferred_element_type=F32)
    lane = _iota((ta, 128), 1)
    nb1 = jnp.where(lane < 8, bg, jnp.where(lane < 12, cs[0:ta], cs[ta:2 * ta]))
    nb2 = jnp.where(lane < 12, cs[2 * ta:3 * ta], cs[3 * ta:4 * ta])
    src = _iota((384, 1024), 0) & 127
    e1 = _ones_where(src == jnp.right_shift(_iota((384, 1024), 1), 6))
    x1 = jnp.dot(jnp.concatenate(_split3(nb1), axis=1), e1, preferred_element_type=F32)
    src2 = _iota((384, 512), 0) & 127
    e2 = _ones_where(src2 == jnp.right_shift(_iota((384, 512), 1), 6) + 8)
    x2 = jnp.dot(jnp.concatenate(_split3(nb2), axis=1), e2, preferred_element_type=F32)

    li = _iota((ta, 256), 0) & 63
    lj = _iota((ta, 256), 1) & 63
    eyecat = li == lj
    eyef = jnp.where(eyecat, 1.0, 0.0)
    j3 = _ones_where(same)
    j3 = jnp.concatenate([j3, j3, j3], axis=1)
    bdm = _ones_where(_bd_mask())

    gram, qk = [], []
    for ci in range(nc):
        sl = slice(ci * DN_CHUNK, (ci + 1) * DN_CHUNK)
        kc = k_ref[0, sl, :]
        gq = lax.dot_general(jnp.concatenate([kc, q_ref[0, sl, :]], axis=0), _bd(kc, bdm),
                             (((1,), (1,)), ((), ())), preferred_element_type=F32)
        gram.append(gq[0:64])
        qk.append(gq[64:128])
    gram = jnp.concatenate(gram, axis=0)
    qk = jnp.concatenate(qk, axis=0)

    outs = ((uf_ref, wf_ref, qdf_ref, inf_ref, kdtf_ref, eglf_ref),
            (ub_ref, wb_ref, qdb_ref, inb_ref, kdtb_ref, eglb_ref))
    chains = []
    for d in range(2):
        u_ref, w_ref, qd_ref, in_ref, kdt_ref, egl_ref = outs[d]
        bx = x1[:, d * 256:(d + 1) * 256]
        gcx = x1[:, 512 + d * 256:768 + d * 256]
        dglx = x2[:, d * 256:(d + 1) * 256]
        rowf = jnp.dot(j3, jnp.concatenate(_split3(jnp.where(eyecat, gcx, 0.0)), axis=0),
                       preferred_element_type=F32)
        incl = (li >= lj) if d == 0 else (li <= lj)
        strict = (li > lj) if d == 0 else (li < lj)
        decay = jnp.exp(jnp.where(incl, gcx - rowf, NEG_BIG))
        eg = jnp.exp(gcx)
        a_all = jnp.where(strict, bx * gram * decay, 0.0)
        in_ref[0] = jnp.where(incl, qk * decay, 0.0).astype(BF16)
        qd_ref[0] = (q * eg).astype(BF16)
        kdt_ref[0] = (k * jnp.exp(dglx)).T.astype(BF16)
        vb = (v * bx).astype(BF16)
        kbg = (k * bx * eg).astype(BF16)
        for ci in range(nc):
            sl = slice(ci * DN_CHUNK, (ci + 1) * DN_CHUNK)
            last = ci * DN_CHUNK + (DN_CHUNK - 1 if d == 0 else 0)
            egl_ref[0, ci * 8:(ci + 1) * 8, :] = jnp.broadcast_to(eg[last:last + 1, :], (8, 256))
            chains.append(dict(sl=sl, u_ref=u_ref, w_ref=w_ref, ak=a_all[sl], p=eyef[sl] - a_all[sl],
                               vb=vb[sl], kbg=kbg[sl]))

    for stage in range(5):
        for ch in chains:
            akb = ch["ak"].astype(BF16)
            lhs = akb if stage == 0 else jnp.concatenate([akb, ch["p"].astype(BF16)], axis=0)
            res = jnp.dot(lhs, _bd(akb, bdm), preferred_element_type=F32)
            ch["ak"] = res[0:64]
            if stage > 0:
                ch["p"] = ch["p"] + res[64:128]
    for ch in chains:
        ch["p"] = ch["p"] + jnp.dot(ch["p"].astype(BF16), _bd(ch["ak"].astype(BF16), bdm),
                                    preferred_element_type=F32)
    for ch in chains:
        rhs = jnp.concatenate([_bd(ch["vb"], bdm), _bd(ch["kbg"], bdm)], axis=1)
        uw = jnp.dot(ch["p"].astype(BF16), rhs, preferred_element_type=F32)
        ch["u_ref"][0, ch["sl"], :] = uw[:, 0:256].astype(BF16)
        ch["w_ref"][0, ch["sl"], :] = uw[:, 256:512].astype(BF16)


def _dn_pre_call(dq, dk, dv, bg, ta):
    b, s, _ = dq.shape
    tok = lambda w: pl.BlockSpec((1, ta, w), lambda bi, i: (bi, i, 0))
    per_dir_shapes = [jax.ShapeDtypeStruct((b, s, 256), BF16)] * 4 + [
        jax.ShapeDtypeStruct((b, 256, s), BF16), jax.ShapeDtypeStruct((b, s // 8, 256), F32)]
    per_dir_specs = [tok(256)] * 4 + [pl.BlockSpec((1, 256, ta), lambda bi, i: (bi, 0, i)),
                                      pl.BlockSpec((1, ta // 8, 256), lambda bi, i: (bi, i, 0))]
    return pl.pallas_call(
        functools.partial(_dn_pre_kernel, ta=ta),
        grid=(b, s // ta), in_specs=[tok(256), tok(256), tok(256), tok(128)],
        out_specs=per_dir_specs * 2, out_shape=per_dir_shapes * 2,
        compiler_params=_cparams(("parallel", "parallel")), name="dn_pre",
    )(dq, dk, dv, bg)


def _dn_scan_kernel(uf_ref, wf_ref, qdf_ref, inf_ref, kdtf_ref, eglf_ref,
                    ub_ref, wb_ref, qdb_ref, inb_ref, kdtb_ref, eglb_ref,
                    of_ref, ob_ref, state_ref, *, nb, tb):
    nc = tb // DN_CHUNK

    @pl.when(pl.program_id(0) == 0)
    def _():
        state_ref[...] = jnp.zeros_like(state_ref)

    bdm = _bd_mask()
    bdm01 = _ones_where(bdm)
    zpad = jnp.zeros((DN_CHUNK, 256), BF16)
    dirs = ((uf_ref, wf_ref, qdf_ref, inf_ref, kdtf_ref, eglf_ref, of_ref),
            (ub_ref, wb_ref, qdb_ref, inb_ref, kdtb_ref, eglb_ref, ob_ref))
    for step in range(nc):
        for bi in range(nb):
            for d in range(2):
                u_ref, w_ref, qd_ref, in_ref, kdt_ref, egl_ref, o_ref = dirs[d]
                ci = step if d == 0 else nc - 1 - step
                sl = slice(ci * DN_CHUNK, (ci + 1) * DN_CHUNK)
                si = d * nb + bi
                state = state_ref[si]
                res = jnp.dot(jnp.concatenate([w_ref[bi, sl, :], qd_ref[bi, sl, :]], axis=0),
                              state.astype(BF16), preferred_element_type=F32)
                v_new = (u_ref[bi, sl, :].astype(F32) - res[0:64]).astype(BF16)
                o_ref[bi, sl, :] = res[64:128] + jnp.dot(in_ref[bi, sl, :], _bd(v_new, bdm01),
                                                         preferred_element_type=F32)
                pair = ci // 2
                kpair = kdt_ref[bi, :, pair * 128:(pair + 1) * 128]
                vpad = jnp.concatenate([v_new, zpad] if ci % 2 == 0 else [zpad, v_new], axis=0)
                ds = jnp.dot(kpair, vpad, preferred_element_type=F32)
                state_ref[si] = state * egl_ref[bi, ci * 8:ci * 8 + 1, :] + jnp.where(bdm, ds, 0.0)


def _dn_scan_call(pre, tb):
    b, s, _ = pre[0].shape
    n = s // tb
    fwd = lambda w: pl.BlockSpec((b, tb, w), lambda i: (0, i, 0))
    bwd = lambda w: pl.BlockSpec((b, tb, w), lambda i: (0, n - 1 - i, 0))
    specs_f = [fwd(256)] * 4 + [pl.BlockSpec((b, 256, tb), lambda i: (0, 0, i)),
                                pl.BlockSpec((b, tb // 8, 256), lambda i: (0, i, 0))]
    specs_b = [bwd(256)] * 4 + [pl.BlockSpec((b, 256, tb), lambda i: (0, 0, n - 1 - i)),
                                pl.BlockSpec((b, tb // 8, 256), lambda i: (0, n - 1 - i, 0))]
    return pl.pallas_call(
        functools.partial(_dn_scan_kernel, nb=b, tb=tb),
        grid=(n,), in_specs=specs_f + specs_b, out_specs=[fwd(256), bwd(256)],
        out_shape=[jax.ShapeDtypeStruct((b, s, 256), F32)] * 2,
        scratch_shapes=[pltpu.VMEM((2 * b, 256, 256), F32)],
        compiler_params=_cparams(("arbitrary",)), name="dn_scan",
    )(*pre)


def _dn_kernel(qf_ref, kf_ref, vf_ref, bgf_ref, qb_ref, kb_ref, vb_ref, bgb_ref,
               of_ref, ob_ref, state_ref, sums_ref, expand_ref, ones_ref, bd_ref, *, nb, tb):
    nc = tb // DN_CHUNK

    @pl.when(pl.program_id(0) == 0)
    def _():
        state_ref[...] = jnp.zeros_like(state_ref)
        r = _iota((tb, tb), 0)
        c = _iota((tb, tb), 1)
        same = jnp.right_shift(r, 6) == jnp.right_shift(c, 6)
        src = _iota((128, 768), 0)
        col = _iota((128, 768), 1)
        for d in range(2):
            m_in = _ones_where(same & ((c <= r) if d == 0 else (c >= r)))
            m_st = _ones_where(same & ((c > r) if d == 0 else (c < r)))
            m2 = jnp.concatenate([m_in, m_st], axis=0)
            sums_ref[d] = jnp.concatenate([m2, m2], axis=1)
            expand_ref[d] = _ones_where((src & 63) == jnp.left_shift(jnp.right_shift(col, 8), 3) + 4 * d
                                        + (jnp.right_shift(col, 6) & 3))
        ones_ref[...] = jnp.concatenate([_ones_where(same), _ones_where(same)], axis=1)
        bd_ref[...] = _ones_where(_bd_mask())

    li = _iota((tb, 256), 0) & 63
    lj = _iota((tb, 256), 1) & 63
    eyecat = li == lj
    eyef = jnp.where(eyecat, 1.0, 0.0)
    j2 = ones_ref[...]
    bdm = _bd_mask()
    bdm01 = bd_ref[...]
    lane = _iota((tb, 128), 1)
    zpad = jnp.zeros((DN_CHUNK, 256), BF16)

    dir_consts = []
    for d in range(2):
        dir_consts.append(dict(m2=sums_ref[d], expand=expand_ref[d],
                               incl=(li >= lj) if d == 0 else (li <= lj),
                               strict=(li > lj) if d == 0 else (li < lj)))
    blocks = []
    for d in range(2):
        q_ref, k_ref, v_ref, bg_ref = ((qf_ref, kf_ref, vf_ref, bgf_ref), (qb_ref, kb_ref, vb_ref, bgb_ref))[d]
        for bi in range(nb):
            blocks.append(dict(d=d, bi=bi, q_ref=q_ref, k_ref=k_ref, v_ref=v_ref, bg=bg_ref[bi], **dir_consts[d]))
    for blk in blocks:
        blk["cs"] = jnp.dot(blk["m2"], jnp.concatenate(_split2(blk["bg"]), axis=0),
                            preferred_element_type=F32)
    for blk in blocks:
        cs = blk["cs"]
        nbx = jnp.where(lane < 8, blk["bg"], jnp.where(lane < 16, cs[0:tb], pltpu.roll(cs[tb:2 * tb], 8, 1)))
        hi = nbx.astype(BF16)
        lo = pltpu.roll(nbx - hi.astype(F32), 64, 1)
        packed = jnp.where(lane < 64, hi.astype(F32), lo).astype(BF16)
        blk["x"] = jnp.dot(packed, blk["expand"], preferred_element_type=F32)
    for blk in blocks:
        gcx = blk["x"][:, 256:512]
        blk["rowf"] = jnp.dot(j2, jnp.concatenate(_split2(jnp.where(eyecat, gcx, 0.0)), axis=0),
                              preferred_element_type=F32)
    for blk in blocks:
        gram, qk = [], []
        for ci in range(nc):
            sl = slice(ci * DN_CHUNK, (ci + 1) * DN_CHUNK)
            kc = blk["k_ref"][blk["bi"], sl, :]
            gq = lax.dot_general(jnp.concatenate([kc, blk["q_ref"][blk["bi"], sl, :]], axis=0), _bd(kc, bdm01),
                                 (((1,), (1,)), ((), ())), preferred_element_type=F32)
            gram.append(gq[0:64])
            qk.append(gq[64:128])
        blk["gram"] = jnp.concatenate(gram, axis=0)
        blk["qk"] = jnp.concatenate(qk, axis=0)
    chains = {}
    for blk in blocks:
        d, bi = blk["d"], blk["bi"]
        q = blk["q_ref"][bi].astype(F32)
        k = blk["k_ref"][bi].astype(F32)
        v = blk["v_ref"][bi].astype(F32)
        x = blk["x"]
        bx, gcx, dglx = x[:, 0:256], x[:, 256:512], x[:, 512:768]
        decay = jnp.exp(jnp.where(blk["incl"], gcx - blk["rowf"], NEG_BIG))
        eg = jnp.exp(gcx)
        a_all = jnp.where(blk["strict"], bx * blk["gram"] * decay, 0.0)
        p_all = eyef - a_all
        intra = jnp.where(blk["incl"], blk["qk"] * decay, 0.0).astype(BF16)
        qd = (q * eg).astype(BF16)
        kdt = (k * jnp.exp(dglx)).T.astype(BF16)
        vbeta = (v * bx).astype(BF16)
        kbg = (k * bx * eg).astype(BF16)
        for ci in range(nc):
            sl = slice(ci * DN_CHUNK, (ci + 1) * DN_CHUNK)
            last = ci * DN_CHUNK + (DN_CHUNK - 1 if d == 0 else 0)
            pair = ci // 2
            chains[(d, bi, ci)] = dict(
                ak=a_all[sl], p=p_all[sl], vb=vbeta[sl], kbg=kbg[sl], intra=intra[sl], qd=qd[sl],
                kpair=kdt[:, pair * 128:(pair + 1) * 128], egl=eg[last:last + 1, :])

    def prepare(group, phase):
        for ch in group:
            if phase < 5:
                akb = ch["ak"].astype(BF16)
                lhs = akb if phase == 0 else jnp.concatenate([akb, ch["p"].astype(BF16)], axis=0)
                res = jnp.dot(lhs, _bd(akb, bdm01), preferred_element_type=F32)
                ch["ak"] = res[0:64]
                if phase > 0:
                    ch["p"] = ch["p"] + res[64:128]
            elif phase == 5:
                ch["p"] = ch["p"] + jnp.dot(ch["p"].astype(BF16), _bd(ch["ak"].astype(BF16), bdm01),
                                            preferred_element_type=F32)
            else:
                rhs = jnp.concatenate([_bd(ch["vb"], bdm01), _bd(ch["kbg"], bdm01)], axis=1)
                uw = jnp.dot(ch["p"].astype(BF16), rhs, preferred_element_type=F32)
                ch["u"] = uw[:, 0:256]
                ch["w"] = uw[:, 256:512].astype(BF16)

    def recur_a(group):
        for ch in group:
            ch["state"] = state_ref[ch["si"]]
            res = jnp.dot(jnp.concatenate([ch["w"], ch["qd"]], axis=0), ch["state"].astype(BF16),
                          preferred_element_type=F32)
            ch["v_new"] = (ch["u"] - res[0:64]).astype(BF16)
            ch["o_inter"] = res[64:128]

    def recur_b(group):
        for ch in group:
            v_new = ch["v_new"]
            ch["o_ref"][ch["bi"], ch["sl"], :] = ch["o_inter"] + jnp.dot(
                ch["intra"], _bd(v_new, bdm01), preferred_element_type=F32)
            vpad = jnp.concatenate([v_new, zpad] if ch["even"] else [zpad, v_new], axis=0)
            ds = jnp.dot(ch["kpair"], vpad, preferred_element_type=F32)
            state_ref[ch["si"]] = ch["state"] * ch["egl"] + jnp.where(bdm, ds, 0.0)

    groups = []
    for step in range(nc):
        group = []
        for bi in range(nb):
            for d in range(2):
                ci = step if d == 0 else nc - 1 - step
                ch = chains[(d, bi, ci)]
                ch.update(si=d * nb + bi, bi=bi, o_ref=(of_ref, ob_ref)[d], even=ci % 2 == 0,
                          sl=slice(ci * DN_CHUNK, (ci + 1) * DN_CHUNK))
                group.append(ch)
        groups.append(group)

    for phase in range(7):
        prepare(groups[0], phase)
    for step in range(nc):
        nxt = groups[step + 1] if step + 1 < nc else []
        prepare(nxt, 0)
        prepare(nxt, 1)
        recur_a(groups[step])
        prepare(nxt, 2)
        prepare(nxt, 3)
        recur_b(groups[step])
        prepare(nxt, 4)
        prepare(nxt, 5)
        prepare(nxt, 6)


def _dn_call(dq, dk, dv, bg, tb):
    b, s, _ = dq.shape
    n = s // tb
    fwd = lambda w: pl.BlockSpec((b, tb, w), lambda i: (0, i, 0))
    bwd = lambda w: pl.BlockSpec((b, tb, w), lambda i: (0, n - 1 - i, 0))
    return pl.pallas_call(
        functools.partial(_dn_kernel, nb=b, tb=tb),
        grid=(n,),
        in_specs=[fwd(256), fwd(256), fwd(256), fwd(128), bwd(256), bwd(256), bwd(256), bwd(128)],
        out_specs=[fwd(256), bwd(256)],
        out_shape=[jax.ShapeDtypeStruct((b, s, 256), F32)] * 2,
        scratch_shapes=[pltpu.VMEM((2 * b, 256, 256), F32), pltpu.VMEM((2, 2 * tb, 2 * tb), BF16),
                        pltpu.VMEM((2, 128, 768), BF16), pltpu.VMEM((tb, 2 * tb), BF16),
                        pltpu.VMEM((256, 256), BF16)],
        compiler_params=_cparams(("arbitrary",)), name="dn",
    )(dq, dk, dv, bg, dq, dk, dv, bg)


def _attn_kernel(qt_ref, k_ref, vt_ref, o_ref, st_ref, *, tq, tk, n_q, n_kv, unroll):
    per_q = n_kv // unroll
    n_trips = n_q * per_q

    def scores(qi, j):
        kj = k_ref[0, pl.ds(pl.multiple_of(j * tk, tk), tk), :]
        return jnp.dot(kj, qt_ref[0, 0, qi], preferred_element_type=F32)

    def update(j, st, m, acc):
        m_new = jnp.maximum(m, jnp.max(st, axis=0, keepdims=True))
        p = jnp.exp2(st - m_new).astype(BF16)
        alpha = jnp.exp2(m - m_new)
        return m_new, alpha * acc + jnp.dot(vt_ref[0, 0, j], p, preferred_element_type=F32)

    st_ref[0] = scores(0, 0)

    def body(t, carry):
        qi = t // per_q
        base = (t - qi * per_q) * unroll
        first = base == 0
        m = jnp.where(first, -jnp.inf, carry[0])
        acc = jnp.where(first, 0.0, carry[1])
        t_next = jnp.minimum(t + 1, n_trips - 1)
        qi_next = t_next // per_q
        base_next = (t_next - qi_next * per_q) * unroll
        for r in range(unroll):
            if r < unroll - 1:
                st_ref[(r + 1) % 2] = scores(qi, base + r + 1)
            else:
                st_ref[0] = scores(qi_next, base_next)
            m, acc = update(base + r, st_ref[r % 2], m, acc)

        @pl.when(base == n_kv - unroll)
        def _():
            o = acc[0:HEAD] / acc[HEAD:HEAD + 1]
            ot = jnp.concatenate([o[:, 0:tq], o[:, tq:2 * tq]], axis=0)
            o_ref[0, pl.ds(pl.multiple_of(qi * tq, tq), tq), :] = ot.T.astype(BF16)

        return m, acc

    m0 = jnp.full((1, 2 * tq), -jnp.inf, F32)
    a0 = jnp.zeros((HEAD + 16, 2 * tq), F32)
    lax.fori_loop(0, n_trips, body, (m0, a0))


def _attn_call(aqt, ak, avt):
    b, s, _ = ak.shape
    _, _, n_q, _, tq2 = aqt.shape
    _, _, n_kv, vrows, tk = avt.shape
    unroll = 8 if n_kv % 8 == 0 else (4 if n_kv % 4 == 0 else 2)
    return pl.pallas_call(
        functools.partial(_attn_kernel, tq=tq2 // 2, tk=tk, n_q=n_q, n_kv=n_kv, unroll=unroll),
        grid=(b, 2),
        in_specs=[pl.BlockSpec((1, 1, n_q, 128, tq2), lambda bi, g: (bi, g, 0, 0, 0)),
                  pl.BlockSpec((1, s, 128), lambda bi, g: (bi, 0, 0)),
                  pl.BlockSpec((1, 1, n_kv, vrows, tk), lambda bi, g: (bi, g, 0, 0, 0))],
        out_specs=pl.BlockSpec((1, s, 128), lambda bi, g: (bi, 0, g)),
        out_shape=jax.ShapeDtypeStruct((b, s, 256), BF16),
        scratch_shapes=[pltpu.VMEM((2, tk, tq2), F32)],
        compiler_params=_cparams(("parallel", "parallel")), name="attn",
    )(aqt, ak, avt)


def _out_kernel(x_ref, ya_ref, yd_ref, yc_ref, of_ref, ob_ref, gates_ref, dnw_ref, wo_ref, o_ref):
    o = of_ref[0] + ob_ref[0]
    gates = gates_ref[0].astype(F32)
    on = o * lax.rsqrt(_head_sum(o * o) * (1.0 / HEAD) + EPS) * dnw_ref[...]
    yb = (on * gates[:, 0:256]).astype(BF16)
    yc = (yc_ref[0].astype(F32) * gates[:, 256:512]).astype(BF16)
    mix = jnp.concatenate([ya_ref[0], yb, yc, yd_ref[0]], axis=1)
    o_ref[0] = x_ref[0] + jnp.dot(mix, wo_ref[...], preferred_element_type=F32)


def _out_call(x, ya, yd, yc, o_f, o_b, gates, lw, tm):
    b, s, _ = x.shape
    tok = lambda w: pl.BlockSpec((1, tm, w), lambda bi, i: (bi, i, 0))
    full = lambda shape: pl.BlockSpec(shape, lambda bi, i: (0,) * len(shape))
    return pl.pallas_call(
        _out_kernel, grid=(b, s // tm),
        in_specs=[tok(D_MODEL), tok(256), tok(256), tok(256), tok(256), tok(256), tok(512),
                  full((1, 256)), full((D_MODEL, D_MODEL))],
        out_specs=tok(D_MODEL), out_shape=jax.ShapeDtypeStruct((b, s, D_MODEL), F32),
        compiler_params=_cparams(("parallel", "parallel")), name="out",
    )(x, ya, yd, yc, o_f, o_b, gates, lw["dn_norm_w"], lw["w_out"])


def _rope_tables(seq_len):
    t = jnp.arange(seq_len)
    pos = jnp.stack([t // GRID_W, t % GRID_W], axis=-1).astype(F32)
    n_freq = HEAD // 4
    inv_freq = jnp.power(ROPE_THETA, -2.0 * jnp.arange(n_freq, dtype=F32) / (HEAD // 2))
    ang = pos[:, :, None] * inv_freq
    cos = jnp.repeat(jnp.cos(ang)[:, :, None, :], 2, axis=2).reshape(seq_len, HEAD)
    sin = jnp.sin(ang)
    sin = jnp.stack([-sin, sin], axis=2).reshape(seq_len, HEAD)
    return jnp.tile(cos, (1, 2)), jnp.tile(sin, (1, 2))


def _layer_weights(l, norm_w, w_in, sgu_w, sgu_b, conv_w, a_log, dt_bias, dn_norm_w, q_norm_w,
                   k_norm_w, pool_w, pool_scale, w_out):
    w = w_in[l]
    cols = lambda a, n: w[:, a:a + n]
    w_halo = jnp.concatenate([cols(_B_Q, 768), cols(_D_X, 256)], axis=1).astype(BF16)
    w_main = jnp.concatenate([
        cols(_A_U, 768), cols(_B_Z, 256), cols(_C_Q, 768), cols(_D_Z, 256), cols(_B_BETA, 16),
        jnp.zeros((D_MODEL, 112), F32)], axis=1).astype(BF16)
    dn_row = jnp.zeros((8, 128), F32)
    dn_row = dn_row.at[0, 8:16].set(a_log[l].reshape(8)).at[1, 8:16].set(dt_bias[l].reshape(8))
    pool_bd = jnp.zeros((256, 256), F32)
    for gi in range(len(POOL_WINDOWS)):
        pool_bd = pool_bd.at[gi * 64:(gi + 1) * 64, gi * 64:(gi + 1) * 64].set(pool_w[l, gi])
    return {
        "norm_w": norm_w[l].reshape(1, D_MODEL),
        "w_halo": w_halo, "w_main": w_main,
        "sgu_w": jnp.transpose(sgu_w[l], (1, 0, 2)).reshape(SGU_CHUNK, 4 * SGU_CHUNK).astype(BF16),
        "sgu_b": jnp.repeat(sgu_b[l].T, HEAD, axis=1),
        "conv_w": jnp.concatenate([conv_w[l], jnp.zeros((3, 768), F32)], axis=0),
        "dn_row": dn_row,
        "q_norm_w": jnp.tile(q_norm_w[l], 4).reshape(1, 256),
        "k_norm_w": jnp.tile(k_norm_w[l], 2).reshape(1, 128),
        "pool_w": pool_bd.astype(BF16),
        "pool_scale": pool_scale[l].reshape(1, 256),
        "dn_norm_w": jnp.tile(dn_norm_w[l], 4).reshape(1, 256),
        "w_out": w_out[l].astype(BF16),
    }


def _tiles(batch, seq_len):
    tb = min(DN_CHUNK * max(1, 8 // batch), seq_len)
    return dict(tm=min(512, seq_len), tb=tb, tq=min(256, seq_len))


def _layer(x, lw, rope_c, rope_s):
    t = _tiles(x.shape[0], x.shape[1])
    ya, yd, dq, dk, dv, bg, gates, aqt, ak, avt = _proj_call(x, lw, rope_c, rope_s, t["tm"], t["tq"])
    o_f, o_b = _dn_call(dq, dk, dv, bg, t["tb"])
    yc = _attn_call(aqt, ak, avt)
    return _out_call(x, ya, yd, yc, o_f, o_b, gates, lw, t["tm"])


def kernel(x_prompt, x_sample, norm_w, w_in, sgu_w, sgu_b, conv_w, a_log, dt_bias, dn_norm_w,
           q_norm_w, k_norm_w, pool_w, pool_scale, w_out):
    depth = norm_w.shape[0]
    rope_p = _rope_tables(x_prompt.shape[1])
    rope_s = _rope_tables(x_sample.shape[1])
    y_prompt, y_sample = x_prompt, x_sample
    for l in range(depth):
        lw = _layer_weights(l, norm_w, w_in, sgu_w, sgu_b, conv_w, a_log, dt_bias, dn_norm_w,
                            q_norm_w, k_norm_w, pool_w, pool_scale, w_out)
        y_prompt = _layer(y_prompt, lw, *rope_p)
        y_sample = _layer(y_sample, lw, *rope_s)
    return (y_prompt, y_sample)
```

```python
import functools

import jax
import jax.numpy as jnp
from jax import lax
from jax.experimental import pallas as pl
from jax.experimental.pallas import tpu as pltpu

F32 = jnp.float32
BF16 = jnp.bfloat16

D_MODEL = 1024
HEAD = 64
N_HEADS = 4
GRID_W = 64
EPS = 1e-6
SGU_CHUNK = 128
DN_CHUNK = 64
ROPE_THETA = 10000.0
POOL_WINDOWS = (2, 4, 8, 16)
HALO = 16
NEG_BIG = -1e30
LOG2_E = 1.4426950408889634
ATTN_TK = 512

V7X_VMEM_LIMIT_BYTES = 56 * 1024 * 1024

_A_U = 0
_B_Q, _B_Z, _B_BETA = 768, 1536, 1792
_C_Q = 1808
_D_X, _D_Z = 2576, 2832
MAIN_COLS = 2176


def _cparams(semantics):
    return pltpu.CompilerParams(dimension_semantics=semantics,
                                vmem_limit_bytes=V7X_VMEM_LIMIT_BYTES)


def _split2(x):
    hi = x.astype(BF16)
    lo = (x - hi.astype(F32)).astype(BF16)
    return hi, lo


def _iota(shape, dim):
    return lax.broadcasted_iota(jnp.int32, shape, dim)


def _ones_where(cond):
    return jnp.where(cond, 1.0, 0.0).astype(BF16)


def _head_sum(x2):
    w = x2.shape[1]
    g = _ones_where(jnp.right_shift(_iota((w, w), 0), 6) == jnp.right_shift(_iota((w, w), 1), 6))
    return jnp.dot(x2.astype(BF16), g, preferred_element_type=F32)


def _silu(z):
    return 0.5 * z * (1.0 + jnp.tanh(0.5 * z))


def _bd_mask():
    return jnp.right_shift(_iota((256, 256), 0), 6) == jnp.right_shift(_iota((256, 256), 1), 6)


def _bd(x, mask01):
    z = jnp.zeros((HEAD, 128), x.dtype)
    blocks = []
    for h in range(N_HEADS):
        t = h // 2
        m = x[:, t * 128:(t + 1) * 128] * mask01[h * HEAD:(h + 1) * HEAD, t * 128:(t + 1) * 128]
        blocks.append(jnp.concatenate([m, z] if t == 0 else [z, m], axis=1))
    return jnp.concatenate(blocks, axis=0)


def _proj_kernel(xp_ref, xc_ref, xn_ref, nw_ref, wh_ref, wm_ref, sguw_ref, sgub_ref, convw_ref,
                 dnrow_ref, qw_ref, kw_ref, rc_ref, rs_ref, poolw_ref, pools_ref,
                 ya_ref, yd_ref, dq_ref, dk_ref, dv_ref, bg_ref, gates_ref, aqt_ref, ak_ref, avt_ref,
                 hext_ref, *, tm, tq, n_tiles, seq_len):
    i = pl.program_id(1)
    n_ext = tm + 2 * HALO
    nw = nw_ref[...]

    def norm(x):
        ms = jnp.mean(x * x, axis=-1, keepdims=True)
        return x * lax.rsqrt(ms + EPS) * nw

    hext_ref[0:HALO, :] = jnp.where(i > 0, norm(xp_ref[0]), 0.0).astype(BF16)
    hext_ref[HALO:HALO + tm, :] = norm(xc_ref[0]).astype(BF16)
    hext_ref[HALO + tm:n_ext, :] = jnp.where(i < n_tiles - 1, norm(xn_ref[0]), 0.0).astype(BF16)

    ph = jnp.dot(hext_ref[...], wh_ref[...], preferred_element_type=F32)
    pm = jnp.dot(hext_ref[HALO:HALO + tm, :], wm_ref[...], preferred_element_type=F32)

    def rows(x):
        return x[HALO:HALO + tm]

    a_u, a_v, a_z = pm[:, 0:256], pm[:, 256:512], pm[:, 512:768]
    vn = a_v * lax.rsqrt(_head_sum(a_v * a_v) * (1.0 / HEAD) + EPS)
    mask4 = _ones_where(jnp.right_shift(_iota((512, 256), 0), 7) == jnp.right_shift(_iota((512, 256), 1), 6))
    mixed = []
    for c in range(tm // SGU_CHUNK):
        vc = vn[c * SGU_CHUNK:(c + 1) * SGU_CHUNK].astype(BF16)
        bdv = jnp.concatenate([vc, vc, vc, vc], axis=0) * mask4
        mixed.append(jnp.dot(sguw_ref[...], bdv, preferred_element_type=F32) + sgub_ref[...])
    mixed = jnp.concatenate(mixed, axis=0)
    ya_ref[0] = (a_u * mixed * _silu(a_z)).astype(BF16)

    xd = ph[:, 768:1024]
    a1 = xd + pltpu.roll(xd, n_ext - 1, 0)
    a2 = a1 + pltpu.roll(a1, n_ext - 2, 0)
    a3 = a2 + pltpu.roll(a2, n_ext - 4, 0)
    a4 = a3 + pltpu.roll(a3, n_ext - 8, 0)
    w2 = rows(pltpu.roll(a1, 1, 0))
    w4 = rows(pltpu.roll(a2, 2, 0))
    w8 = rows(pltpu.roll(a3, 4, 0))
    w16 = rows(pltpu.roll(a4, 8, 0))
    grp = jnp.right_shift(_iota((tm, 256), 1), 6)
    half = jnp.left_shift(jnp.ones((tm, 256), jnp.int32), grp)
    t = i * tm + _iota((tm, 256), 0)
    cnt = (jnp.minimum(t + half, seq_len) - jnp.maximum(t - half, 0)).astype(F32)
    win = jnp.where(grp == 0, w2, jnp.where(grp == 1, w4, jnp.where(grp == 2, w8, w16)))
    diff = win / cnt - rows(xd)
    yd = jnp.dot(diff.astype(BF16), poolw_ref[...], preferred_element_type=F32) * pools_ref[...]
    yd_ref[0] = (yd * _silu(pm[:, 1792:2048])).astype(BF16)

    xb = ph[:, 0:768]
    cw = convw_ref[...]
    conv = (rows(pltpu.roll(xb, 2, 0)) * cw[0:1] + rows(pltpu.roll(xb, 1, 0)) * cw[1:2]
            + rows(xb) * cw[2:3] + rows(pltpu.roll(xb, n_ext - 1, 0)) * cw[3:4]
            + rows(pltpu.roll(xb, n_ext - 2, 0)) * cw[4:5])
    act = _silu(conv)
    bq, bk = act[:, 0:256], act[:, 256:512]
    dq_ref[0] = (bq * lax.rsqrt(_head_sum(bq * bq) + EPS) * (HEAD ** -0.5)).astype(BF16)
    dk_ref[0] = (bk * lax.rsqrt(_head_sum(bk * bk) + EPS)).astype(BF16)
    dv_ref[0] = act[:, 512:768].astype(BF16)
    ba = pm[:, 2048:2176]
    lane = _iota((tm, 128), 1)
    xa = ba + dnrow_ref[1:2, :]
    softplus = jnp.maximum(xa, 0.0) + jnp.log1p(jnp.exp(-jnp.abs(xa)))
    g = -jnp.exp(dnrow_ref[0:1, :]) * softplus
    bg_ref[0] = jnp.where(lane < 8, 1.0 / (1.0 + jnp.exp(-ba)), g)
    gates_ref[0] = jnp.concatenate([_silu(pm[:, 768:1024]), _silu(pm[:, 1536:1792])], axis=1).astype(BF16)

    rc, rs = rc_ref[...], rs_ref[...]
    first = (_iota((tm, 128), 1) & 16) == 0

    def rope(x):
        sw = jnp.where(first, pltpu.roll(x, 112, 1), pltpu.roll(x, 16, 1))
        return x * rc + sw * rs

    cq, ck = pm[:, 1024:1280], pm[:, 1280:1408]
    qn = cq * lax.rsqrt(_head_sum(cq * cq) * (1.0 / HEAD) + EPS) * qw_ref[...]
    qr = jnp.concatenate([rope(qn[:, 0:128]), rope(qn[:, 128:256])], axis=1) * (HEAD ** -0.5 * LOG2_E)
    qt = qr.T.astype(BF16)
    zq = jnp.zeros((HEAD, 2 * tq), BF16)
    for r in range(tm // tq):
        cs = slice(r * tq, (r + 1) * tq)
        top0 = jnp.concatenate([qt[0:64, cs], qt[64:128, cs]], axis=1)
        top1 = jnp.concatenate([qt[128:192, cs], qt[192:256, cs]], axis=1)
        aqt_ref[0, 0, r] = jnp.concatenate([top0, zq], axis=0)
        aqt_ref[0, 1, r] = jnp.concatenate([zq, top1], axis=0)
    kn = ck * lax.rsqrt(_head_sum(ck * ck) * (1.0 / HEAD) + EPS) * kw_ref[...]
    ak_ref[0] = rope(kn).astype(BF16)
    vt = pm[:, 1408:1536].T
    ones = jnp.ones((16, tm), F32)
    for g in range(2):
        vg = jnp.concatenate([vt[g * HEAD:(g + 1) * HEAD], ones], axis=0).astype(BF16)
        for r in range(tm // ATTN_TK):
            avt_ref[0, g, r] = vg[:, r * ATTN_TK:(r + 1) * ATTN_TK]


def _proj_call(x, lw, rope_c, rope_s, tm, tq):
    b, s, _ = x.shape
    n_tiles = s // tm
    hb = tm // HALO
    last_hb = s // HALO - 1
    full = lambda shape: pl.BlockSpec(shape, lambda bi, i: (0,) * len(shape))
    tok = lambda w: pl.BlockSpec((1, tm, w), lambda bi, i: (bi, i, 0))
    in_specs = [
        pl.BlockSpec((1, HALO, D_MODEL), lambda bi, i: (bi, jnp.maximum(i * hb - 1, 0), 0)),
        pl.BlockSpec((1, tm, D_MODEL), lambda bi, i: (bi, i, 0)),
        pl.BlockSpec((1, HALO, D_MODEL), lambda bi, i: (bi, jnp.minimum((i + 1) * hb, last_hb), 0)),
        full((1, D_MODEL)), full((D_MODEL, 1024)), full((D_MODEL, MAIN_COLS)),
        full((128, 512)), full((128, 256)), full((8, 768)), full((8, 128)),
        full((1, 256)), full((1, 128)),
        pl.BlockSpec((tm, 128), lambda bi, i: (i, 0)), pl.BlockSpec((tm, 128), lambda bi, i: (i, 0)),
        full((256, 256)), full((1, 256)),
    ]
    out_shape = [
        jax.ShapeDtypeStruct((b, s, 256), BF16),
        jax.ShapeDtypeStruct((b, s, 256), BF16),
        jax.ShapeDtypeStruct((b, s, 256), BF16),
        jax.ShapeDtypeStruct((b, s, 256), BF16),
        jax.ShapeDtypeStruct((b, s, 256), BF16),
        jax.ShapeDtypeStruct((b, s, 128), F32),
        jax.ShapeDtypeStruct((b, s, 512), BF16),
        jax.ShapeDtypeStruct((b, 2, s // tq, 128, 2 * tq), BF16),
        jax.ShapeDtypeStruct((b, s, 128), BF16),
        jax.ShapeDtypeStruct((b, 2, s // ATTN_TK, HEAD + 16, ATTN_TK), BF16),
    ]
    out_specs = [tok(256), tok(256), tok(256), tok(256), tok(256), tok(128), tok(512),
                 pl.BlockSpec((1, 2, tm // tq, 128, 2 * tq), lambda bi, i: (bi, 0, i, 0, 0)), tok(128),
                 pl.BlockSpec((1, 2, tm // ATTN_TK, HEAD + 16, ATTN_TK), lambda bi, i: (bi, 0, i, 0, 0))]
    return pl.pallas_call(
        functools.partial(_proj_kernel, tm=tm, tq=tq, n_tiles=n_tiles, seq_len=s),
        grid=(b, n_tiles), in_specs=in_specs, out_specs=out_specs, out_shape=out_shape,
        scratch_shapes=[pltpu.VMEM((tm + 2 * HALO, D_MODEL), BF16)],
        compiler_params=_cparams(("parallel", "parallel")), name="proj",
    )(x, x, x, lw["norm_w"], lw["w_halo"], lw["w_main"], lw["sgu_w"], lw["sgu_b"], lw["conv_w"],
      lw["dn_row"], lw["q_norm_w"], lw["k_norm_w"], rope_c, rope_s, lw["pool_w"], lw["pool_scale"])


def _dn_kernel(qf_ref, kf_ref, vf_ref, bgf_ref, qb_ref, kb_ref, vb_ref, bgb_ref,
               of_ref, ob_ref, state_ref, sums_ref, expand_ref, ones_ref, bd_ref, *, nb, tb):
    nc = tb // DN_CHUNK

    @pl.when(pl.program_id(0) == 0)
    def _():
        state_ref[...] = jnp.zeros_like(state_ref)
        r = _iota((tb, tb), 0)
        c = _iota((tb, tb), 1)
        same = jnp.right_shift(r, 6) == jnp.right_shift(c, 6)
        src = _iota((128, 768), 0)
        col = _iota((128, 768), 1)
        for d in range(2):
            m_in = _ones_where(same & ((c <= r) if d == 0 else (c >= r)))
            m_st = _ones_where(same & ((c > r) if d == 0 else (c < r)))
            m2 = jnp.concatenate([m_in, m_st], axis=0)
            sums_ref[d] = jnp.concatenate([m2, m2], axis=1)
            expand_ref[d] = _ones_where((src & 63) == jnp.left_shift(jnp.right_shift(col, 8), 3) + 4 * d
                                        + (jnp.right_shift(col, 6) & 3))
        ones_ref[...] = jnp.concatenate([_ones_where(same), _ones_where(same)], axis=1)
        bd_ref[...] = _ones_where(_bd_mask())

    li = _iota((tb, 256), 0) & 63
    lj = _iota((tb, 256), 1) & 63
    eyecat = li == lj
    eyef = jnp.where(eyecat, 1.0, 0.0)
    j2 = ones_ref[...]
    bdm = _bd_mask()
    bdm01 = bd_ref[...]
    lane = _iota((tb, 128), 1)
    zpad = jnp.zeros((DN_CHUNK, 256), BF16)

    dir_consts = []
    for d in range(2):
        dir_consts.append(dict(m2=sums_ref[d], expand=expand_ref[d],
                               incl=(li >= lj) if d == 0 else (li <= lj),
                               strict=(li > lj) if d == 0 else (li < lj)))
    blocks = []
    for d in range(2):
        q_ref, k_ref, v_ref, bg_ref = ((qf_ref, kf_ref, vf_ref, bgf_ref), (qb_ref, kb_ref, vb_ref, bgb_ref))[d]
        for bi in range(nb):
            blocks.append(dict(d=d, bi=bi, q_ref=q_ref, k_ref=k_ref, v_ref=v_ref, bg=bg_ref[bi], **dir_consts[d]))
    for blk in blocks:
        blk["cs"] = jnp.dot(blk["m2"], jnp.concatenate(_split2(blk["bg"]), axis=0),
                            preferred_element_type=F32)
    for blk in blocks:
        cs = blk["cs"]
        nbx = jnp.where(lane < 8, blk["bg"], jnp.where(lane < 16, cs[0:tb], pltpu.roll(cs[tb:2 * tb], 8, 1)))
        hi = nbx.astype(BF16)
        lo = pltpu.roll(nbx - hi.astype(F32), 64, 1)
        packed = jnp.where(lane < 64, hi.astype(F32), lo).astype(BF16)
        blk["x"] = jnp.dot(packed, blk["expand"], preferred_element_type=F32)
    for blk in blocks:
        gcx = blk["x"][:, 256:512]
        blk["rowf"] = jnp.dot(j2, jnp.concatenate(_split2(jnp.where(eyecat, gcx, 0.0)), axis=0),
                              preferred_element_type=F32)
    for blk in blocks:
        gram, qk = [], []
        for ci in range(nc):
            sl = slice(ci * DN_CHUNK, (ci + 1) * DN_CHUNK)
            kc = blk["k_ref"][blk["bi"], sl, :]
            gq = lax.dot_general(jnp.concatenate([kc, blk["q_ref"][blk["bi"], sl, :]], axis=0), _bd(kc, bdm01),
                                 (((1,), (1,)), ((), ())), preferred_element_type=F32)
            gram.append(gq[0:64])
            qk.append(gq[64:128])
        blk["gram"] = jnp.concatenate(gram, axis=0)
        blk["qk"] = jnp.concatenate(qk, axis=0)
    chains = {}
    for blk in blocks:
        d, bi = blk["d"], blk["bi"]
        q = blk["q_ref"][bi].astype(F32)
        k = blk["k_ref"][bi].astype(F32)
        v = blk["v_ref"][bi].astype(F32)
        x = blk["x"]
        bx, gcx, dglx = x[:, 0:256], x[:, 256:512], x[:, 512:768]
        decay = jnp.exp(jnp.where(blk["incl"], gcx - blk["rowf"], NEG_BIG))
        eg = jnp.exp(gcx)
        a_all = jnp.where(blk["strict"], bx * blk["gram"] * decay, 0.0)
        p_all = eyef - a_all
        intra = jnp.where(blk["incl"], blk["qk"] * decay, 0.0).astype(BF16)
        qd = (q * eg).astype(BF16)
        kdt = (k * jnp.exp(dglx)).T.astype(BF16)
        vbeta = (v * bx).astype(BF16)
        kbg = (k * bx * eg).astype(BF16)
        for ci in range(nc):
            sl = slice(ci * DN_CHUNK, (ci + 1) * DN_CHUNK)
            last = ci * DN_CHUNK + (DN_CHUNK - 1 if d == 0 else 0)
            pair = ci // 2
            chains[(d, bi, ci)] = dict(
                ak=a_all[sl], p=p_all[sl], vb=vbeta[sl], kbg=kbg[sl], intra=intra[sl], qd=qd[sl],
                kpair=kdt[:, pair * 128:(pair + 1) * 128], egl=eg[last:last + 1, :])

    def prepare(group, phase):
        for ch in group:
            if phase < 5:
                akb = ch["ak"].astype(BF16)
                lhs = akb if phase == 0 else jnp.concatenate([akb, ch["p"].astype(BF16)], axis=0)
                res = jnp.dot(lhs, _bd(akb, bdm01), preferred_element_type=F32)
                ch["ak"] = res[0:64]
                if phase > 0:
                    ch["p"] = ch["p"] + res[64:128]
            elif phase == 5:
                ch["p"] = ch["p"] + jnp.dot(ch["p"].astype(BF16), _bd(ch["ak"].astype(BF16), bdm01),
                                            preferred_element_type=F32)
            else:
                rhs = jnp.concatenate([_bd(ch["vb"], bdm01), _bd(ch["kbg"], bdm01)], axis=1)
                uw = jnp.dot(ch["p"].astype(BF16), rhs, preferred_element_type=F32)
                ch["u"] = uw[:, 0:256]
                ch["w"] = uw[:, 256:512].astype(BF16)

    def recur_a(group):
        for ch in group:
            ch["state"] = state_ref[ch["si"]]
            res = jnp.dot(jnp.concatenate([ch["w"], ch["qd"]], axis=0), ch["state"].astype(BF16),
                          preferred_element_type=F32)
            ch["v_new"] = (ch["u"] - res[0:64]).astype(BF16)
            ch["o_inter"] = res[64:128]

    def recur_b(group):
        for ch in group:
            v_new = ch["v_new"]
            ch["o_ref"][ch["bi"], ch["sl"], :] = (ch["o_inter"] + jnp.dot(
                ch["intra"], _bd(v_new, bdm01), preferred_element_type=F32)).astype(BF16)
            vpad = jnp.concatenate([v_new, zpad] if ch["even"] else [zpad, v_new], axis=0)
            ds = jnp.dot(ch["kpair"], vpad, preferred_element_type=F32)
            state_ref[ch["si"]] = ch["state"] * ch["egl"] + jnp.where(bdm, ds, 0.0)

    groups = []
    for step in range(nc):
        group = []
        for bi in range(nb):
            for d in range(2):
                ci = step if d == 0 else nc - 1 - step
                ch = chains[(d, bi, ci)]
                ch.update(si=d * nb + bi, bi=bi, o_ref=(of_ref, ob_ref)[d], even=ci % 2 == 0,
                          sl=slice(ci * DN_CHUNK, (ci + 1) * DN_CHUNK))
                group.append(ch)
        groups.append(group)

    for phase in range(7):
        prepare(groups[0], phase)
    for step in range(nc):
        nxt = groups[step + 1] if step + 1 < nc else []
        prepare(nxt, 0)
        prepare(nxt, 1)
        recur_a(groups[step])
        prepare(nxt, 2)
        prepare(nxt, 3)
        recur_b(groups[step])
        prepare(nxt, 4)
        prepare(nxt, 5)
        prepare(nxt, 6)


def _dn_call(dq, dk, dv, bg, tb):
    b, s, _ = dq.shape
    n = s // tb
    fwd = lambda w: pl.BlockSpec((b, tb, w), lambda i: (0, i, 0))
    bwd = lambda w: pl.BlockSpec((b, tb, w), lambda i: (0, n - 1 - i, 0))
    return pl.pallas_call(
        functools.partial(_dn_kernel, nb=b, tb=tb),
        grid=(n,),
        in_specs=[fwd(256), fwd(256), fwd(256), fwd(128), bwd(256), bwd(256), bwd(256), bwd(128)],
        out_specs=[fwd(256), bwd(256)],
        out_shape=[jax.ShapeDtypeStruct((b, s, 256), BF16)] * 2,
        scratch_shapes=[pltpu.VMEM((2 * b, 256, 256), F32), pltpu.VMEM((2, 2 * tb, 2 * tb), BF16),
                        pltpu.VMEM((2, 128, 768), BF16), pltpu.VMEM((tb, 2 * tb), BF16),
                        pltpu.VMEM((256, 256), BF16)],
        compiler_params=_cparams(("arbitrary",)), name="dn",
    )(dq, dk, dv, bg, dq, dk, dv, bg)


def _attn_kernel(qt_ref, k_ref, vt_ref, o_ref, st_ref, *, tq, tk, n_q, n_kv, unroll):
    per_q = n_kv // unroll
    n_trips = n_q * per_q

    def scores(qi, j):
        kj = k_ref[0, pl.ds(pl.multiple_of(j * tk, tk), tk), :]
        return jnp.dot(kj, qt_ref[0, 0, qi], preferred_element_type=F32)

    def update(j, st, m, acc):
        m_new = jnp.maximum(m, jnp.max(st, axis=0, keepdims=True))
        p = jnp.exp2(st - m_new).astype(BF16)
        alpha = jnp.exp2(m - m_new)
        return m_new, alpha * acc + jnp.dot(vt_ref[0, 0, j], p, preferred_element_type=F32)

    st_ref[0] = scores(0, 0)

    def body(t, carry):
        qi = t // per_q
        base = (t - qi * per_q) * unroll
        first = base == 0
        m = jnp.where(first, -jnp.inf, carry[0])
        acc = jnp.where(first, 0.0, carry[1])
        t_next = jnp.minimum(t + 1, n_trips - 1)
        qi_next = t_next // per_q
        base_next = (t_next - qi_next * per_q) * unroll
        for r in range(unroll):
            if r < unroll - 1:
                st_ref[(r + 1) % 2] = scores(qi, base + r + 1)
            else:
                st_ref[0] = scores(qi_next, base_next)
            m, acc = update(base + r, st_ref[r % 2], m, acc)

        @pl.when(base == n_kv - unroll)
        def _():
            o = acc[0:HEAD] / acc[HEAD:HEAD + 1]
            ot = jnp.concatenate([o[:, 0:tq], o[:, tq:2 * tq]], axis=0)
            o_ref[0, pl.ds(pl.multiple_of(qi * tq, tq), tq), :] = ot.T.astype(BF16)

        return m, acc

    m0 = jnp.full((1, 2 * tq), -jnp.inf, F32)
    a0 = jnp.zeros((HEAD + 16, 2 * tq), F32)
    lax.fori_loop(0, n_trips, body, (m0, a0))


def _attn_call(aqt, ak, avt):
    b, s, _ = ak.shape
    _, _, n_q, _, tq2 = aqt.shape
    _, _, n_kv, vrows, tk = avt.shape
    unroll = 8 if n_kv % 8 == 0 else (4 if n_kv % 4 == 0 else 2)
    return pl.pallas_call(
        functools.partial(_attn_kernel, tq=tq2 // 2, tk=tk, n_q=n_q, n_kv=n_kv, unroll=unroll),
        grid=(b, 2),
        in_specs=[pl.BlockSpec((1, 1, n_q, 128, tq2), lambda bi, g: (bi, g, 0, 0, 0)),
                  pl.BlockSpec((1, s, 128), lambda bi, g: (bi, 0, 0)),
                  pl.BlockSpec((1, 1, n_kv, vrows, tk), lambda bi, g: (bi, g, 0, 0, 0))],
        out_specs=pl.BlockSpec((1, s, 128), lambda bi, g: (bi, 0, g)),
        out_shape=jax.ShapeDtypeStruct((b, s, 256), BF16),
        scratch_shapes=[pltpu.VMEM((2, tk, tq2), F32)],
        compiler_params=_cparams(("parallel", "parallel")), name="attn",
    )(aqt, ak, avt)


def _out_kernel(x_ref, ya_ref, yd_ref, yc_ref, of_ref, ob_ref, gates_ref, dnw_ref, wo_ref, o_ref):
    o = of_ref[0].astype(F32) + ob_ref[0].astype(F32)
    gates = gates_ref[0].astype(F32)
    on = o * lax.rsqrt(_head_sum(o * o) * (1.0 / HEAD) + EPS) * dnw_ref[...]
    yb = (on * gates[:, 0:256]).astype(BF16)
    yc = (yc_ref[0].astype(F32) * gates[:, 256:512]).astype(BF16)
    mix = jnp.concatenate([ya_ref[0], yb, yc, yd_ref[0]], axis=1)
    o_ref[0] = x_ref[0] + jnp.dot(mix, wo_ref[...], preferred_element_type=F32)


def _out_call(x, ya, yd, yc, o_f, o_b, gates, lw, tm):
    b, s, _ = x.shape
    tok = lambda w: pl.BlockSpec((1, tm, w), lambda bi, i: (bi, i, 0))
    full = lambda shape: pl.BlockSpec(shape, lambda bi, i: (0,) * len(shape))
    return pl.pallas_call(
        _out_kernel, grid=(b, s // tm),
        in_specs=[tok(D_MODEL), tok(256), tok(256), tok(256), tok(256), tok(256), tok(512),
                  full((1, 256)), full((D_MODEL, D_MODEL))],
        out_specs=tok(D_MODEL), out_shape=jax.ShapeDtypeStruct((b, s, D_MODEL), F32),
        compiler_params=_cparams(("parallel", "parallel")), name="out",
    )(x, ya, yd, yc, o_f, o_b, gates, lw["dn_norm_w"], lw["w_out"])


def _rope_tables(seq_len):
    t = jnp.arange(seq_len)
    pos = jnp.stack([t // GRID_W, t % GRID_W], axis=-1).astype(F32)
    n_freq = HEAD // 4
    inv_freq = jnp.power(ROPE_THETA, -2.0 * jnp.arange(n_freq, dtype=F32) / (HEAD // 2))
    ang = pos[:, :, None] * inv_freq
    cos = jnp.repeat(jnp.cos(ang)[:, :, None, :], 2, axis=2).reshape(seq_len, HEAD)
    sin = jnp.sin(ang)
    sin = jnp.stack([-sin, sin], axis=2).reshape(seq_len, HEAD)
    return jnp.tile(cos, (1, 2)), jnp.tile(sin, (1, 2))


def _layer_weights(l, norm_w, w_in, sgu_w, sgu_b, conv_w, a_log, dt_bias, dn_norm_w, q_norm_w,
                   k_norm_w, pool_w, pool_scale, w_out):
    w = w_in[l]
    cols = lambda a, n: w[:, a:a + n]
    w_halo = jnp.concatenate([cols(_B_Q, 768), cols(_D_X, 256)], axis=1).astype(BF16)
    w_main = jnp.concatenate([
        cols(_A_U, 768), cols(_B_Z, 256), cols(_C_Q, 768), cols(_D_Z, 256), cols(_B_BETA, 16),
        jnp.zeros((D_MODEL, 112), F32)], axis=1).astype(BF16)
    dn_row = jnp.zeros((8, 128), F32)
    dn_row = dn_row.at[0, 8:16].set(a_log[l].reshape(8)).at[1, 8:16].set(dt_bias[l].reshape(8))
    pool_bd = jnp.zeros((256, 256), F32)
    for gi in range(len(POOL_WINDOWS)):
        pool_bd = pool_bd.at[gi * 64:(gi + 1) * 64, gi * 64:(gi + 1) * 64].set(pool_w[l, gi])
    return {
        "norm_w": norm_w[l].reshape(1, D_MODEL),
        "w_halo": w_halo, "w_main": w_main,
        "sgu_w": jnp.transpose(sgu_w[l], (1, 0, 2)).reshape(SGU_CHUNK, 4 * SGU_CHUNK).astype(BF16),
        "sgu_b": jnp.repeat(sgu_b[l].T, HEAD, axis=1),
        "conv_w": jnp.concatenate([conv_w[l], jnp.zeros((3, 768), F32)], axis=0),
        "dn_row": dn_row,
        "q_norm_w": jnp.tile(q_norm_w[l], 4).reshape(1, 256),
        "k_norm_w": jnp.tile(k_norm_w[l], 2).reshape(1, 128),
        "pool_w": pool_bd.astype(BF16),
        "pool_scale": pool_scale[l].reshape(1, 256),
        "dn_norm_w": jnp.tile(dn_norm_w[l], 4).reshape(1, 256),
        "w_out": w_out[l].astype(BF16),
    }


def _tiles(batch, seq_len):
    tb = min(DN_CHUNK * max(1, 8 // batch), seq_len)
    return dict(tm=min(512, seq_len), tb=tb, tq=min(256, seq_len))


def _layer(x, lw, rope_c, rope_s):
    t = _tiles(x.shape[0], x.shape[1])
    ya, yd, dq, dk, dv, bg, gates, aqt, ak, avt = _proj_call(x, lw, rope_c, rope_s, t["tm"], t["tq"])
    o_f, o_b = _dn_call(dq, dk, dv, bg, t["tb"])
    yc = _attn_call(aqt, ak, avt)
    return _out_call(x, ya, yd, yc, o_f, o_b, gates, lw, t["tm"])


def kernel(x_prompt, x_sample, norm_w, w_in, sgu_w, sgu_b, conv_w, a_log, dt_bias, dn_norm_w,
           q_norm_w, k_norm_w, pool_w, pool_scale, w_out):
    depth = norm_w.shape[0]
    rope_p = _rope_tables(x_prompt.shape[1])
    rope_s = _rope_tables(x_sample.shape[1])
    y_prompt, y_sample = x_prompt, x_sample
    for l in range(depth):
        lw = _layer_weights(l, norm_w, w_in, sgu_w, sgu_b, conv_w, a_log, dt_bias, dn_norm_w,
                            q_norm_w, k_norm_w, pool_w, pool_scale, w_out)
        y_prompt = _layer(y_prompt, lw, *rope_p)
        y_sample = _layer(y_sample, lw, *rope_s)
    return (y_prompt, y_sample)
```

```python
import functools

import jax
import jax.numpy as jnp
from jax import lax
from jax.experimental import pallas as pl
from jax.experimental.pallas import tpu as pltpu

F32 = jnp.float32
BF16 = jnp.bfloat16

D_MODEL = 1024
HEAD = 64
N_HEADS = 4
GRID_W = 64
EPS = 1e-6
SGU_CHUNK = 128
DN_CHUNK = 64
ROPE_THETA = 10000.0
POOL_WINDOWS = (2, 4, 8, 16)
HALO = 16
NEG_BIG = -1e30
LOG2_E = 1.4426950408889634
ATTN_TK = 512

V7X_VMEM_LIMIT_BYTES = 56 * 1024 * 1024

_A_U = 0
_B_Q, _B_Z, _B_BETA = 768, 1536, 1792
_C_Q = 1808
_D_X, _D_Z = 2576, 2832
MAIN_COLS = 2176


def _cparams(semantics):
    return pltpu.CompilerParams(dimension_semantics=semantics,
                                vmem_limit_bytes=V7X_VMEM_LIMIT_BYTES)


def _split2(x):
    hi = x.astype(BF16)
    lo = (x - hi.astype(F32)).astype(BF16)
    return hi, lo


def _iota(shape, dim):
    return lax.broadcasted_iota(jnp.int32, shape, dim)


def _ones_where(cond):
    return jnp.where(cond, 1.0, 0.0).astype(BF16)


def _head_sum(x2):
    w = x2.shape[1]
    g = _ones_where(jnp.right_shift(_iota((w, w), 0), 6) == jnp.right_shift(_iota((w, w), 1), 6))
    return jnp.dot(x2.astype(BF16), g, preferred_element_type=F32)


def _silu(z):
    return 0.5 * z * (1.0 + jnp.tanh(0.5 * z))


def _bd_mask():
    return jnp.right_shift(_iota((256, 256), 0), 6) == jnp.right_shift(_iota((256, 256), 1), 6)


def _bd(x, mask01):
    z = jnp.zeros((HEAD, 128), x.dtype)
    blocks = []
    for h in range(N_HEADS):
        t = h // 2
        m = x[:, t * 128:(t + 1) * 128] * mask01[h * HEAD:(h + 1) * HEAD, t * 128:(t + 1) * 128]
        blocks.append(jnp.concatenate([m, z] if t == 0 else [z, m], axis=1))
    return jnp.concatenate(blocks, axis=0)


def _proj_kernel(xp_ref, xc_ref, xn_ref, nw_ref, wh_ref, wm_ref, sguw_ref, sgub_ref, convw_ref,
                 dnrow_ref, qw_ref, kw_ref, rc_ref, rs_ref, poolw_ref, pools_ref,
                 ya_ref, yd_ref, dq_ref, dk_ref, dv_ref, bg_ref, gates_ref, aqt_ref, ak_ref, avt_ref,
                 hext_ref, *, tm, tq, n_tiles, seq_len):
    i = pl.program_id(1)
    n_ext = tm + 2 * HALO
    nw = nw_ref[...]

    def norm(x):
        ms = jnp.mean(x * x, axis=-1, keepdims=True)
        return x * lax.rsqrt(ms + EPS) * nw

    hext_ref[0:HALO, :] = jnp.where(i > 0, norm(xp_ref[0]), 0.0).astype(BF16)
    hext_ref[HALO:HALO + tm, :] = norm(xc_ref[0]).astype(BF16)
    hext_ref[HALO + tm:n_ext, :] = jnp.where(i < n_tiles - 1, norm(xn_ref[0]), 0.0).astype(BF16)

    ph = jnp.dot(hext_ref[...], wh_ref[...], preferred_element_type=F32)
    pm = jnp.dot(hext_ref[HALO:HALO + tm, :], wm_ref[...], preferred_element_type=F32)

    def rows(x):
        return x[HALO:HALO + tm]

    a_u, a_v, a_z = pm[:, 0:256], pm[:, 256:512], pm[:, 512:768]
    vn = a_v * lax.rsqrt(_head_sum(a_v * a_v) * (1.0 / HEAD) + EPS)
    mask4 = _ones_where(jnp.right_shift(_iota((512, 256), 0), 7) == jnp.right_shift(_iota((512, 256), 1), 6))
    mixed = []
    for c in range(tm // SGU_CHUNK):
        vc = vn[c * SGU_CHUNK:(c + 1) * SGU_CHUNK].astype(BF16)
        bdv = jnp.concatenate([vc, vc, vc, vc], axis=0) * mask4
        mixed.append(jnp.dot(sguw_ref[...], bdv, preferred_element_type=F32) + sgub_ref[...])
    mixed = jnp.concatenate(mixed, axis=0)
    ya_ref[0] = (a_u * mixed * _silu(a_z)).astype(BF16)

    xd = ph[:, 768:1024]
    a1 = xd + pltpu.roll(xd, n_ext - 1, 0)
    a2 = a1 + pltpu.roll(a1, n_ext - 2, 0)
    a3 = a2 + pltpu.roll(a2, n_ext - 4, 0)
    a4 = a3 + pltpu.roll(a3, n_ext - 8, 0)
    w2 = rows(pltpu.roll(a1, 1, 0))
    w4 = rows(pltpu.roll(a2, 2, 0))
    w8 = rows(pltpu.roll(a3, 4, 0))
    w16 = rows(pltpu.roll(a4, 8, 0))
    grp = jnp.right_shift(_iota((tm, 256), 1), 6)
    half = jnp.left_shift(jnp.ones((tm, 256), jnp.int32), grp)
    t = i * tm + _iota((tm, 256), 0)
    cnt = (jnp.minimum(t + half, seq_len) - jnp.maximum(t - half, 0)).astype(F32)
    win = jnp.where(grp == 0, w2, jnp.where(grp == 1, w4, jnp.where(grp == 2, w8, w16)))
    diff = win / cnt - rows(xd)
    yd = jnp.dot(diff.astype(BF16), poolw_ref[...], preferred_element_type=F32) * pools_ref[...]
    yd_ref[0] = (yd * _silu(pm[:, 1792:2048])).astype(BF16)

    xb = ph[:, 0:768]
    cw = convw_ref[...]
    conv = (rows(pltpu.roll(xb, 2, 0)) * cw[0:1] + rows(pltpu.roll(xb, 1, 0)) * cw[1:2]
            + rows(xb) * cw[2:3] + rows(pltpu.roll(xb, n_ext - 1, 0)) * cw[3:4]
            + rows(pltpu.roll(xb, n_ext - 2, 0)) * cw[4:5])
    act = _silu(conv)
    bq, bk = act[:, 0:256], act[:, 256:512]
    dq_ref[0] = (bq * lax.rsqrt(_head_sum(bq * bq) + EPS) * (HEAD ** -0.5)).astype(BF16)
    dk_ref[0] = (bk * lax.rsqrt(_head_sum(bk * bk) + EPS)).astype(BF16)
    dv_ref[0] = act[:, 512:768].astype(BF16)
    ba = pm[:, 2048:2176]
    lane = _iota((tm, 128), 1)
    xa = ba + dnrow_ref[1:2, :]
    softplus = jnp.maximum(xa, 0.0) + jnp.log1p(jnp.exp(-jnp.abs(xa)))
    g = -jnp.exp(dnrow_ref[0:1, :]) * softplus
    bg_ref[0] = jnp.where(lane < 8, 1.0 / (1.0 + jnp.exp(-ba)), g)
    gates_ref[0] = jnp.concatenate([_silu(pm[:, 768:1024]), _silu(pm[:, 1536:1792])], axis=1).astype(BF16)

    rc, rs = rc_ref[...], rs_ref[...]
    first = (_iota((tm, 128), 1) & 16) == 0

    def rope(x):
        sw = jnp.where(first, pltpu.roll(x, 112, 1), pltpu.roll(x, 16, 1))
        return x * rc + sw * rs

    cq, ck = pm[:, 1024:1280], pm[:, 1280:1408]
    qn = cq * lax.rsqrt(_head_sum(cq * cq) * (1.0 / HEAD) + EPS) * qw_ref[...]
    qr = jnp.concatenate([rope(qn[:, 0:128]), rope(qn[:, 128:256])], axis=1) * (HEAD ** -0.5 * LOG2_E)
    qt = qr.T.astype(BF16)
    zq = jnp.zeros((HEAD, 2 * tq), BF16)
    for r in range(tm // tq):
        cs = slice(r * tq, (r + 1) * tq)
        top0 = jnp.concatenate([qt[0:64, cs], qt[64:128, cs]], axis=1)
        top1 = jnp.concatenate([qt[128:192, cs], qt[192:256, cs]], axis=1)
        aqt_ref[0, 0, r] = jnp.concatenate([top0, zq], axis=0)
        aqt_ref[0, 1, r] = jnp.concatenate([zq, top1], axis=0)
    kn = ck * lax.rsqrt(_head_sum(ck * ck) * (1.0 / HEAD) + EPS) * kw_ref[...]
    ak_ref[0] = rope(kn).astype(BF16)
    vt = pm[:, 1408:1536].T.astype(BF16)
    for g in range(2):
        for r in range(tm // ATTN_TK):
            avt_ref[0, g, r] = vt[g * HEAD:(g + 1) * HEAD, r * ATTN_TK:(r + 1) * ATTN_TK]


def _proj_call(x, lw, rope_c, rope_s, tm, tq):
    b, s, _ = x.shape
    n_tiles = s // tm
    hb = tm // HALO
    last_hb = s // HALO - 1
    full = lambda shape: pl.BlockSpec(shape, lambda bi, i: (0,) * len(shape))
    tok = lambda w: pl.BlockSpec((1, tm, w), lambda bi, i: (bi, i, 0))
    in_specs = [
        pl.BlockSpec((1, HALO, D_MODEL), lambda bi, i: (bi, jnp.maximum(i * hb - 1, 0), 0)),
        pl.BlockSpec((1, tm, D_MODEL), lambda bi, i: (bi, i, 0)),
        pl.BlockSpec((1, HALO, D_MODEL), lambda bi, i: (bi, jnp.minimum((i + 1) * hb, last_hb), 0)),
        full((1, D_MODEL)), full((D_MODEL, 1024)), full((D_MODEL, MAIN_COLS)),
        full((128, 512)), full((128, 256)), full((8, 768)), full((8, 128)),
        full((1, 256)), full((1, 128)),
        pl.BlockSpec((tm, 128), lambda bi, i: (i, 0)), pl.BlockSpec((tm, 128), lambda bi, i: (i, 0)),
        full((256, 256)), full((1, 256)),
    ]
    out_shape = [
        jax.ShapeDtypeStruct((b, s, 256), BF16),
        jax.ShapeDtypeStruct((b, s, 256), BF16),
        jax.ShapeDtypeStruct((b, s, 256), BF16),
        jax.ShapeDtypeStruct((b, s, 256), BF16),
        jax.ShapeDtypeStruct((b, s, 256), BF16),
        jax.ShapeDtypeStruct((b, s, 128), F32),
        jax.ShapeDtypeStruct((b, s, 512), BF16),
        jax.ShapeDtypeStruct((b, 2, s // tq, 128, 2 * tq), BF16),
        jax.ShapeDtypeStruct((b, s, 128), BF16),
        jax.ShapeDtypeStruct((b, 2, s // ATTN_TK, HEAD, ATTN_TK), BF16),
    ]
    out_specs = [tok(256), tok(256), tok(256), tok(256), tok(256), tok(128), tok(512),
                 pl.BlockSpec((1, 2, tm // tq, 128, 2 * tq), lambda bi, i: (bi, 0, i, 0, 0)), tok(128),
                 pl.BlockSpec((1, 2, tm // ATTN_TK, HEAD, ATTN_TK), lambda bi, i: (bi, 0, i, 0, 0))]
    return pl.pallas_call(
        functools.partial(_proj_kernel, tm=tm, tq=tq, n_tiles=n_tiles, seq_len=s),
        grid=(b, n_tiles), in_specs=in_specs, out_specs=out_specs, out_shape=out_shape,
        scratch_shapes=[pltpu.VMEM((tm + 2 * HALO, D_MODEL), BF16)],
        compiler_params=_cparams(("parallel", "parallel")), name="proj",
    )(x, x, x, lw["norm_w"], lw["w_halo"], lw["w_main"], lw["sgu_w"], lw["sgu_b"], lw["conv_w"],
      lw["dn_row"], lw["q_norm_w"], lw["k_norm_w"], rope_c, rope_s, lw["pool_w"], lw["pool_scale"])


def _dn_kernel(qf_ref, kf_ref, vf_ref, bgf_ref, qb_ref, kb_ref, vb_ref, bgb_ref,
               of_ref, ob_ref, state_ref, sums_ref, expand_ref, ones_ref, bd_ref, *, nb, tb):
    nc = tb // DN_CHUNK

    @pl.when(pl.program_id(0) == 0)
    def _():
        state_ref[...] = jnp.zeros_like(state_ref)
        r = _iota((tb, tb), 0)
        c = _iota((tb, tb), 1)
        same = jnp.right_shift(r, 6) == jnp.right_shift(c, 6)
        src = _iota((128, 768), 0)
        col = _iota((128, 768), 1)
        for d in range(2):
            m_in = _ones_where(same & ((c <= r) if d == 0 else (c >= r)))
            m_st = _ones_where(same & ((c > r) if d == 0 else (c < r)))
            m2 = jnp.concatenate([m_in, m_st], axis=0)
            sums_ref[d] = jnp.concatenate([m2, m2], axis=1)
            expand_ref[d] = _ones_where((src & 63) == jnp.left_shift(jnp.right_shift(col, 8), 3) + 4 * d
                                        + (jnp.right_shift(col, 6) & 3))
        ones_ref[...] = jnp.concatenate([_ones_where(same), _ones_where(same)], axis=1)
        bd_ref[...] = _ones_where(_bd_mask())

    li = _iota((tb, 256), 0) & 63
    lj = _iota((tb, 256), 1) & 63
    eyecat = li == lj
    eyef = jnp.where(eyecat, 1.0, 0.0)
    j2 = ones_ref[...]
    bdm = _bd_mask()
    bdm01 = bd_ref[...]
    lane = _iota((tb, 128), 1)
    zpad = jnp.zeros((DN_CHUNK, 256), BF16)

    dir_consts = []
    for d in range(2):
        dir_consts.append(dict(m2=sums_ref[d], expand=expand_ref[d],
                               incl=(li >= lj) if d == 0 else (li <= lj),
                               strict=(li > lj) if d == 0 else (li < lj)))
    blocks = []
    for d in range(2):
        q_ref, k_ref, v_ref, bg_ref = ((qf_ref, kf_ref, vf_ref, bgf_ref), (qb_ref, kb_ref, vb_ref, bgb_ref))[d]
        for bi in range(nb):
            blocks.append(dict(d=d, bi=bi, q_ref=q_ref, k_ref=k_ref, v_ref=v_ref, bg=bg_ref[bi], **dir_consts[d]))
    for blk in blocks:
        blk["cs"] = jnp.dot(blk["m2"], jnp.concatenate(_split2(blk["bg"]), axis=0),
                            preferred_element_type=F32)
    for blk in blocks:
        cs = blk["cs"]
        nbx = jnp.where(lane < 8, blk["bg"], jnp.where(lane < 16, cs[0:tb], pltpu.roll(cs[tb:2 * tb], 8, 1)))
        hi = nbx.astype(BF16)
        lo = pltpu.roll(nbx - hi.astype(F32), 64, 1)
        packed = jnp.where(lane < 64, hi.astype(F32), lo).astype(BF16)
        blk["x"] = jnp.dot(packed, blk["expand"], preferred_element_type=F32)
    for blk in blocks:
        gcx = blk["x"][:, 256:512]
        blk["rowf"] = jnp.dot(j2, jnp.concatenate(_split2(jnp.where(eyecat, gcx, 0.0)), axis=0),
                              preferred_element_type=F32)
    for blk in blocks:
        gram, qk = [], []
        for ci in range(nc):
            sl = slice(ci * DN_CHUNK, (ci + 1) * DN_CHUNK)
            kc = blk["k_ref"][blk["bi"], sl, :]
            gq = lax.dot_general(jnp.concatenate([kc, blk["q_ref"][blk["bi"], sl, :]], axis=0), _bd(kc, bdm01),
                                 (((1,), (1,)), ((), ())), preferred_element_type=F32)
            gram.append(gq[0:64])
            qk.append(gq[64:128])
        blk["gram"] = jnp.concatenate(gram, axis=0)
        blk["qk"] = jnp.concatenate(qk, axis=0)
    chains = {}
    for blk in blocks:
        d, bi = blk["d"], blk["bi"]
        q = blk["q_ref"][bi].astype(F32)
        k = blk["k_ref"][bi].astype(F32)
        v = blk["v_ref"][bi].astype(F32)
        x = blk["x"]
        bx, gcx, dglx = x[:, 0:256], x[:, 256:512], x[:, 512:768]
        decay = jnp.exp(jnp.where(blk["incl"], gcx - blk["rowf"], NEG_BIG))
        eg = jnp.exp(gcx)
        a_all = jnp.where(blk["strict"], bx * blk["gram"] * decay, 0.0)
        p_all = eyef - a_all
        intra = jnp.where(blk["incl"], blk["qk"] * decay, 0.0).astype(BF16)
        qd = (q * eg).astype(BF16)
        kdt = (k * jnp.exp(dglx)).T.astype(BF16)
        vbeta = (v * bx).astype(BF16)
        kbg = (k * bx * eg).astype(BF16)
        for ci in range(nc):
            sl = slice(ci * DN_CHUNK, (ci + 1) * DN_CHUNK)
            last = ci * DN_CHUNK + (DN_CHUNK - 1 if d == 0 else 0)
            pair = ci // 2
            chains[(d, bi, ci)] = dict(
                ak=a_all[sl], p=p_all[sl], vb=vbeta[sl], kbg=kbg[sl], intra=intra[sl], qd=qd[sl],
                kpair=kdt[:, pair * 128:(pair + 1) * 128], egl=eg[last:last + 1, :])

    def prepare(group, phase):
        for ch in group:
            if phase < 5:
                akb = ch["ak"].astype(BF16)
                lhs = akb if phase == 0 else jnp.concatenate([akb, ch["p"].astype(BF16)], axis=0)
                res = jnp.dot(lhs, _bd(akb, bdm01), preferred_element_type=F32)
                ch["ak"] = res[0:64]
                if phase > 0:
                    ch["p"] = ch["p"] + res[64:128]
            elif phase == 5:
                ch["p"] = ch["p"] + jnp.dot(ch["p"].astype(BF16), _bd(ch["ak"].astype(BF16), bdm01),
                                            preferred_element_type=F32)
            else:
                rhs = jnp.concatenate([_bd(ch["vb"], bdm01), _bd(ch["kbg"], bdm01)], axis=1)
                uw = jnp.dot(ch["p"].astype(BF16), rhs, preferred_element_type=F32)
                ch["u"] = uw[:, 0:256]
                ch["w"] = uw[:, 256:512].astype(BF16)

    def recur_a(group):
        for ch in group:
            ch["state"] = state_ref[ch["si"]]
            res = jnp.dot(jnp.concatenate([ch["w"], ch["qd"]], axis=0), ch["state"].astype(BF16),
                          preferred_element_type=F32)
            ch["v_new"] = (ch["u"] - res[0:64]).astype(BF16)
            ch["o_inter"] = res[64:128]

    def recur_b(group):
        for ch in group:
            v_new = ch["v_new"]
            ch["o_ref"][ch["bi"], ch["sl"], :] = (ch["o_inter"] + jnp.dot(
                ch["intra"], _bd(v_new, bdm01), preferred_element_type=F32)).astype(BF16)
            vpad = jnp.concatenate([v_new, zpad] if ch["even"] else [zpad, v_new], axis=0)
            ds = jnp.dot(ch["kpair"], vpad, preferred_element_type=F32)
            state_ref[ch["si"]] = ch["state"] * ch["egl"] + jnp.where(bdm, ds, 0.0)

    groups = []
    for step in range(nc):
        group = []
        for bi in range(nb):
            for d in range(2):
                ci = step if d == 0 else nc - 1 - step
                ch = chains[(d, bi, ci)]
                ch.update(si=d * nb + bi, bi=bi, o_ref=(of_ref, ob_ref)[d], even=ci % 2 == 0,
                          sl=slice(ci * DN_CHUNK, (ci + 1) * DN_CHUNK))
                group.append(ch)
        groups.append(group)

    for phase in range(7):
        prepare(groups[0], phase)
    for step in range(nc):
        nxt = groups[step + 1] if step + 1 < nc else []
        prepare(nxt, 0)
        prepare(nxt, 1)
        recur_a(groups[step])
        prepare(nxt, 2)
        prepare(nxt, 3)
        recur_b(groups[step])
        prepare(nxt, 4)
        prepare(nxt, 5)
        prepare(nxt, 6)


def _dn_call(dq, dk, dv, bg, tb):
    b, s, _ = dq.shape
    n = s // tb
    fwd = lambda w: pl.BlockSpec((b, tb, w), lambda i: (0, i, 0))
    bwd = lambda w: pl.BlockSpec((b, tb, w), lambda i: (0, n - 1 - i, 0))
    return pl.pallas_call(
        functools.partial(_dn_kernel, nb=b, tb=tb),
        grid=(n,),
        in_specs=[fwd(256), fwd(256), fwd(256), fwd(128), bwd(256), bwd(256), bwd(256), bwd(128)],
        out_specs=[fwd(256), bwd(256)],
        out_shape=[jax.ShapeDtypeStruct((b, s, 256), BF16)] * 2,
        scratch_shapes=[pltpu.VMEM((2 * b, 256, 256), F32), pltpu.VMEM((2, 2 * tb, 2 * tb), BF16),
                        pltpu.VMEM((2, 128, 768), BF16), pltpu.VMEM((tb, 2 * tb), BF16),
                        pltpu.VMEM((256, 256), BF16)],
        compiler_params=_cparams(("arbitrary",)), name="dn",
    )(dq, dk, dv, bg, dq, dk, dv, bg)


def _attn_kernel(qt_ref, k_ref, vt_ref, o_ref, st_ref, *, tq, tk, n_q, n_kv, unroll):
    per_q = n_kv // unroll
    n_trips = n_q * per_q

    def scores(qi, j):
        kj = k_ref[0, pl.ds(pl.multiple_of(j * tk, tk), tk), :]
        return jnp.dot(kj, qt_ref[0, 0, qi], preferred_element_type=F32)

    def update(j, st, m, l, acc):
        m_new = jnp.maximum(m, jnp.max(st, axis=0, keepdims=True))
        p = jnp.exp2(st - m_new)
        alpha = jnp.exp2(m - m_new)
        l_new = alpha * l + jnp.sum(p, axis=0, keepdims=True)
        return m_new, l_new, alpha * acc + jnp.dot(vt_ref[0, 0, j], p.astype(BF16),
                                                   preferred_element_type=F32)

    st_ref[0] = scores(0, 0)

    def body(t, carry):
        qi = t // per_q
        base = (t - qi * per_q) * unroll
        first = base == 0
        m = jnp.where(first, -jnp.inf, carry[0])
        l = jnp.where(first, 0.0, carry[1])
        acc = jnp.where(first, 0.0, carry[2])
        t_next = jnp.minimum(t + 1, n_trips - 1)
        qi_next = t_next // per_q
        base_next = (t_next - qi_next * per_q) * unroll
        for r in range(unroll):
            if r < unroll - 1:
                st_ref[(r + 1) % 2] = scores(qi, base + r + 1)
            else:
                st_ref[0] = scores(qi_next, base_next)
            m, l, acc = update(base + r, st_ref[r % 2], m, l, acc)

        @pl.when(base == n_kv - unroll)
        def _():
            o = acc / l
            ot = jnp.concatenate([o[:, 0:tq], o[:, tq:2 * tq]], axis=0)
            o_ref[0, pl.ds(pl.multiple_of(qi * tq, tq), tq), :] = ot.T.astype(BF16)

        return m, l, acc

    m0 = jnp.full((1, 2 * tq), -jnp.inf, F32)
    l0 = jnp.zeros((1, 2 * tq), F32)
    a0 = jnp.zeros((HEAD, 2 * tq), F32)
    lax.fori_loop(0, n_trips, body, (m0, l0, a0))


def _attn_call(aqt, ak, avt):
    b, s, _ = ak.shape
    _, _, n_q, _, tq2 = aqt.shape
    _, _, n_kv, vrows, tk = avt.shape
    unroll = 8 if n_kv % 8 == 0 else (4 if n_kv % 4 == 0 else 2)
    return pl.pallas_call(
        functools.partial(_attn_kernel, tq=tq2 // 2, tk=tk, n_q=n_q, n_kv=n_kv, unroll=unroll),
        grid=(b, 2),
        in_specs=[pl.BlockSpec((1, 1, n_q, 128, tq2), lambda bi, g: (bi, g, 0, 0, 0)),
                  pl.BlockSpec((1, s, 128), lambda bi, g: (bi, 0, 0)),
                  pl.BlockSpec((1, 1, n_kv, vrows, tk), lambda bi, g: (bi, g, 0, 0, 0))],
        out_specs=pl.BlockSpec((1, s, 128), lambda bi, g: (bi, 0, g)),
        out_shape=jax.ShapeDtypeStruct((b, s, 256), BF16),
        scratch_shapes=[pltpu.VMEM((2, tk, tq2), F32)],
        compiler_params=_cparams(("parallel", "parallel")), name="attn",
    )(aqt, ak, avt)


def _out_kernel(x_ref, ya_ref, yd_ref, yc_ref, of_ref, ob_ref, gates_ref, dnw_ref, wo_ref, o_ref):
    o = of_ref[0].astype(F32) + ob_ref[0].astype(F32)
    gates = gates_ref[0].astype(F32)
    on = o * lax.rsqrt(_head_sum(o * o) * (1.0 / HEAD) + EPS) * dnw_ref[...]
    yb = (on * gates[:, 0:256]).astype(BF16)
    yc = (yc_ref[0].astype(F32) * gates[:, 256:512]).astype(BF16)
    mix = jnp.concatenate([ya_ref[0], yb, yc, yd_ref[0]], axis=1)
    o_ref[0] = x_ref[0] + jnp.dot(mix, wo_ref[...], preferred_element_type=F32)


def _out_call(x, ya, yd, yc, o_f, o_b, gates, lw, tm):
    b, s, _ = x.shape
    tok = lambda w: pl.BlockSpec((1, tm, w), lambda bi, i: (bi, i, 0))
    full = lambda shape: pl.BlockSpec(shape, lambda bi, i: (0,) * len(shape))
    return pl.pallas_call(
        _out_kernel, grid=(b, s // tm),
        in_specs=[tok(D_MODEL), tok(256), tok(256), tok(256), tok(256), tok(256), tok(512),
                  full((1, 256)), full((D_MODEL, D_MODEL))],
        out_specs=tok(D_MODEL), out_shape=jax.ShapeDtypeStruct((b, s, D_MODEL), F32),
        compiler_params=_cparams(("parallel", "parallel")), name="out",
    )(x, ya, yd, yc, o_f, o_b, gates, lw["dn_norm_w"], lw["w_out"])


def _rope_tables(seq_len):
    t = jnp.arange(seq_len)
    pos = jnp.stack([t // GRID_W, t % GRID_W], axis=-1).astype(F32)
    n_freq = HEAD // 4
    inv_freq = jnp.power(ROPE_THETA, -2.0 * jnp.arange(n_freq, dtype=F32) / (HEAD // 2))
    ang = pos[:, :, None] * inv_freq
    cos = jnp.repeat(jnp.cos(ang)[:, :, None, :], 2, axis=2).reshape(seq_len, HEAD)
    sin = jnp.sin(ang)
    sin = jnp.stack([-sin, sin], axis=2).reshape(seq_len, HEAD)
    return jnp.tile(cos, (1, 2)), jnp.tile(sin, (1, 2))


def _layer_weights(l, norm_w, w_in, sgu_w, sgu_b, conv_w, a_log, dt_bias, dn_norm_w, q_norm_w,
                   k_norm_w, pool_w, pool_scale, w_out):
    w = w_in[l]
    cols = lambda a, n: w[:, a:a + n]
    w_halo = jnp.concatenate([cols(_B_Q, 768), cols(_D_X, 256)], axis=1).astype(BF16)
    w_main = jnp.concatenate([
        cols(_A_U, 768), cols(_B_Z, 256), cols(_C_Q, 768), cols(_D_Z, 256), cols(_B_BETA, 16),
        jnp.zeros((D_MODEL, 112), F32)], axis=1).astype(BF16)
    dn_row = jnp.zeros((8, 128), F32)
    dn_row = dn_row.at[0, 8:16].set(a_log[l].reshape(8)).at[1, 8:16].set(dt_bias[l].reshape(8))
    pool_bd = jnp.zeros((256, 256), F32)
    for gi in range(len(POOL_WINDOWS)):
        pool_bd = pool_bd.at[gi * 64:(gi + 1) * 64, gi * 64:(gi + 1) * 64].set(pool_w[l, gi])
    return {
        "norm_w": norm_w[l].reshape(1, D_MODEL),
        "w_halo": w_halo, "w_main": w_main,
        "sgu_w": jnp.transpose(sgu_w[l], (1, 0, 2)).reshape(SGU_CHUNK, 4 * SGU_CHUNK).astype(BF16),
        "sgu_b": jnp.repeat(sgu_b[l].T, HEAD, axis=1),
        "conv_w": jnp.concatenate([conv_w[l], jnp.zeros((3, 768), F32)], axis=0),
        "dn_row": dn_row,
        "q_norm_w": jnp.tile(q_norm_w[l], 4).reshape(1, 256),
        "k_norm_w": jnp.tile(k_norm_w[l], 2).reshape(1, 128),
        "pool_w": pool_bd.astype(BF16),
        "pool_scale": pool_scale[l].reshape(1, 256),
        "dn_norm_w": jnp.tile(dn_norm_w[l], 4).reshape(1, 256),
        "w_out": w_out[l].astype(BF16),
    }


def _tiles(batch, seq_len):
    tb = min(DN_CHUNK * max(1, 8 // batch), seq_len)
    return dict(tm=min(512, seq_len), tb=tb, tq=min(256, seq_len))


def _layer(x, lw, rope_c, rope_s):
    t = _tiles(x.shape[0], x.shape[1])
    ya, yd, dq, dk, dv, bg, gates, aqt, ak, avt = _proj_call(x, lw, rope_c, rope_s, t["tm"], t["tq"])
    o_f, o_b = _dn_call(dq, dk, dv, bg, t["tb"])
    yc = _attn_call(aqt, ak, avt)
    return _out_call(x, ya, yd, yc, o_f, o_b, gates, lw, t["tm"])


def kernel(x_prompt, x_sample, norm_w, w_in, sgu_w, sgu_b, conv_w, a_log, dt_bias, dn_norm_w,
           q_norm_w, k_norm_w, pool_w, pool_scale, w_out):
    depth = norm_w.shape[0]
    rope_p = _rope_tables(x_prompt.shape[1])
    rope_s = _rope_tables(x_sample.shape[1])
    y_prompt, y_sample = x_prompt, x_sample
    for l in range(depth):
        lw = _layer_weights(l, norm_w, w_in, sgu_w, sgu_b, conv_w, a_log, dt_bias, dn_norm_w,
                            q_norm_w, k_norm_w, pool_w, pool_scale, w_out)
        y_prompt = _layer(y_prompt, lw, *rope_p)
        y_sample = _layer(y_sample, lw, *rope_s)
    return (y_prompt, y_sample)
```

```python
import functools

import jax
import jax.numpy as jnp
from jax import lax
from jax.experimental import pallas as pl
from jax.experimental.pallas import tpu as pltpu

F32 = jnp.float32
BF16 = jnp.bfloat16

D_MODEL = 1024
HEAD = 64
N_HEADS = 4
GRID_W = 64
EPS = 1e-6
SGU_CHUNK = 128
DN_CHUNK = 64
ROPE_THETA = 10000.0
POOL_WINDOWS = (2, 4, 8, 16)
HALO = 16
NEG_BIG = -1e30
LOG2_E = 1.4426950408889634
ATTN_TK = 512

V7X_VMEM_LIMIT_BYTES = 56 * 1024 * 1024

_A_U = 0
_B_Q, _B_Z, _B_BETA = 768, 1536, 1792
_C_Q = 1808
_D_X, _D_Z = 2576, 2832
MAIN_COLS = 2176


def _cparams(semantics):
    return pltpu.CompilerParams(dimension_semantics=semantics,
                                vmem_limit_bytes=V7X_VMEM_LIMIT_BYTES)


def _split2(x):
    hi = x.astype(BF16)
    lo = (x - hi.astype(F32)).astype(BF16)
    return hi, lo


def _iota(shape, dim):
    return lax.broadcasted_iota(jnp.int32, shape, dim)


def _ones_where(cond):
    return jnp.where(cond, 1.0, 0.0).astype(BF16)


def _head_sum(x2):
    w = x2.shape[1]
    g = _ones_where(jnp.right_shift(_iota((w, w), 0), 6) == jnp.right_shift(_iota((w, w), 1), 6))
    return jnp.dot(x2.astype(BF16), g, preferred_element_type=F32)


def _silu(z):
    return 0.5 * z * (1.0 + jnp.tanh(0.5 * z))


def _bd_mask():
    return jnp.right_shift(_iota((256, 256), 0), 6) == jnp.right_shift(_iota((256, 256), 1), 6)


def _bd(x, mask01):
    z = jnp.zeros((HEAD, 128), x.dtype)
    blocks = []
    for h in range(N_HEADS):
        t = h // 2
        m = x[:, t * 128:(t + 1) * 128] * mask01[h * HEAD:(h + 1) * HEAD, t * 128:(t + 1) * 128]
        blocks.append(jnp.concatenate([m, z] if t == 0 else [z, m], axis=1))
    return jnp.concatenate(blocks, axis=0)


def _proj_kernel(xp_ref, xc_ref, xn_ref, nw_ref, wh_ref, wm_ref, sguw_ref, sgub_ref, convw_ref,
                 dnrow_ref, qw_ref, kw_ref, rc_ref, rs_ref, poolw_ref, pools_ref,
                 ya_ref, yd_ref, dq_ref, dk_ref, dv_ref, bg_ref, gates_ref, aqt_ref, ak_ref, avt_ref,
                 hext_ref, *, tm, tq, n_tiles, seq_len):
    i = pl.program_id(1)
    n_ext = tm + 2 * HALO
    nw = nw_ref[...]

    def norm(x):
        ms = jnp.mean(x * x, axis=-1, keepdims=True)
        return x * lax.rsqrt(ms + EPS) * nw

    hext_ref[0:HALO, :] = jnp.where(i > 0, norm(xp_ref[0]), 0.0).astype(BF16)
    hext_ref[HALO:HALO + tm, :] = norm(xc_ref[0]).astype(BF16)
    hext_ref[HALO + tm:n_ext, :] = jnp.where(i < n_tiles - 1, norm(xn_ref[0]), 0.0).astype(BF16)

    ph = jnp.dot(hext_ref[...], wh_ref[...], preferred_element_type=F32)
    pm = jnp.dot(hext_ref[HALO:HALO + tm, :], wm_ref[...], preferred_element_type=F32)

    def rows(x):
        return x[HALO:HALO + tm]

    a_u, a_v, a_z = pm[:, 0:256], pm[:, 256:512], pm[:, 512:768]
    vn = a_v * lax.rsqrt(_head_sum(a_v * a_v) * (1.0 / HEAD) + EPS)
    mask4 = _ones_where(jnp.right_shift(_iota((512, 256), 0), 7) == jnp.right_shift(_iota((512, 256), 1), 6))
    mixed = []
    for c in range(tm // SGU_CHUNK):
        vc = vn[c * SGU_CHUNK:(c + 1) * SGU_CHUNK].astype(BF16)
        bdv = jnp.concatenate([vc, vc, vc, vc], axis=0) * mask4
        mixed.append(jnp.dot(sguw_ref[...], bdv, preferred_element_type=F32) + sgub_ref[...])
    mixed = jnp.concatenate(mixed, axis=0)
    ya_ref[0] = (a_u * mixed * _silu(a_z)).astype(BF16)

    xd = ph[:, 768:1024]
    a1 = xd + pltpu.roll(xd, n_ext - 1, 0)
    a2 = a1 + pltpu.roll(a1, n_ext - 2, 0)
    a3 = a2 + pltpu.roll(a2, n_ext - 4, 0)
    a4 = a3 + pltpu.roll(a3, n_ext - 8, 0)
    w2 = rows(pltpu.roll(a1, 1, 0))
    w4 = rows(pltpu.roll(a2, 2, 0))
    w8 = rows(pltpu.roll(a3, 4, 0))
    w16 = rows(pltpu.roll(a4, 8, 0))
    grp = jnp.right_shift(_iota((tm, 256), 1), 6)
    half = jnp.left_shift(jnp.ones((tm, 256), jnp.int32), grp)
    t = i * tm + _iota((tm, 256), 0)
    cnt = (jnp.minimum(t + half, seq_len) - jnp.maximum(t - half, 0)).astype(F32)
    win = jnp.where(grp == 0, w2, jnp.where(grp == 1, w4, jnp.where(grp == 2, w8, w16)))
    diff = win / cnt - rows(xd)
    yd = jnp.dot(diff.astype(BF16), poolw_ref[...], preferred_element_type=F32) * pools_ref[...]
    yd_ref[0] = (yd * _silu(pm[:, 1792:2048])).astype(BF16)

    xb = ph[:, 0:768]
    cw = convw_ref[...]
    conv = (rows(pltpu.roll(xb, 2, 0)) * cw[0:1] + rows(pltpu.roll(xb, 1, 0)) * cw[1:2]
            + rows(xb) * cw[2:3] + rows(pltpu.roll(xb, n_ext - 1, 0)) * cw[3:4]
            + rows(pltpu.roll(xb, n_ext - 2, 0)) * cw[4:5])
    act = _silu(conv)
    bq, bk = act[:, 0:256], act[:, 256:512]
    dq_ref[0] = (bq * lax.rsqrt(_head_sum(bq * bq) + EPS) * (HEAD ** -0.5)).astype(BF16)
    dk_ref[0] = (bk * lax.rsqrt(_head_sum(bk * bk) + EPS)).astype(BF16)
    dv_ref[0] = act[:, 512:768].astype(BF16)
    ba = pm[:, 2048:2176]
    lane = _iota((tm, 128), 1)
    xa = ba + dnrow_ref[1:2, :]
    softplus = jnp.maximum(xa, 0.0) + jnp.log1p(jnp.exp(-jnp.abs(xa)))
    g = -jnp.exp(dnrow_ref[0:1, :]) * softplus
    bg_ref[0] = jnp.where(lane < 8, 1.0 / (1.0 + jnp.exp(-ba)), g)
    gates_ref[0] = jnp.concatenate([_silu(pm[:, 768:1024]), _silu(pm[:, 1536:1792])], axis=1).astype(BF16)

    rc, rs = rc_ref[...], rs_ref[...]
    first = (_iota((tm, 128), 1) & 16) == 0

    def rope(x):
        sw = jnp.where(first, pltpu.roll(x, 112, 1), pltpu.roll(x, 16, 1))
        return x * rc + sw * rs

    cq, ck = pm[:, 1024:1280], pm[:, 1280:1408]
    qn = cq * lax.rsqrt(_head_sum(cq * cq) * (1.0 / HEAD) + EPS) * qw_ref[...]
    qr = jnp.concatenate([rope(qn[:, 0:128]), rope(qn[:, 128:256])], axis=1) * (HEAD ** -0.5 * LOG2_E)
    qt = qr.T.astype(BF16)
    zq = jnp.zeros((HEAD, 2 * tq), BF16)
    for r in range(tm // tq):
        cs = slice(r * tq, (r + 1) * tq)
        top0 = jnp.concatenate([qt[0:64, cs], qt[64:128, cs]], axis=1)
        top1 = jnp.concatenate([qt[128:192, cs], qt[192:256, cs]], axis=1)
        aqt_ref[0, 0, r] = jnp.concatenate([top0, zq], axis=0)
        aqt_ref[0, 1, r] = jnp.concatenate([zq, top1], axis=0)
    kn = ck * lax.rsqrt(_head_sum(ck * ck) * (1.0 / HEAD) + EPS) * kw_ref[...]
    ak_ref[0] = rope(kn).astype(BF16)
    vt = pm[:, 1408:1536].T
    ones = jnp.ones((16, tm), F32)
    for g in range(2):
        vg = jnp.concatenate([vt[g * HEAD:(g + 1) * HEAD], ones], axis=0).astype(BF16)
        for r in range(tm // ATTN_TK):
            avt_ref[0, g, r] = vg[:, r * ATTN_TK:(r + 1) * ATTN_TK]


def _proj_call(x, lw, rope_c, rope_s, tm, tq):
    b, s, _ = x.shape
    n_tiles = s // tm
    hb = tm // HALO
    last_hb = s // HALO - 1
    full = lambda shape: pl.BlockSpec(shape, lambda bi, i: (0,) * len(shape))
    tok = lambda w: pl.BlockSpec((1, tm, w), lambda bi, i: (bi, i, 0))
    in_specs = [
        pl.BlockSpec((1, HALO, D_MODEL), lambda bi, i: (bi, jnp.maximum(i * hb - 1, 0), 0)),
        pl.BlockSpec((1, tm, D_MODEL), lambda bi, i: (bi, i, 0)),
        pl.BlockSpec((1, HALO, D_MODEL), lambda bi, i: (bi, jnp.minimum((i + 1) * hb, last_hb), 0)),
        full((1, D_MODEL)), full((D_MODEL, 1024)), full((D_MODEL, MAIN_COLS)),
        full((128, 512)), full((128, 256)), full((8, 768)), full((8, 128)),
        full((1, 256)), full((1, 128)),
        pl.BlockSpec((tm, 128), lambda bi, i: (i, 0)), pl.BlockSpec((tm, 128), lambda bi, i: (i, 0)),
        full((256, 256)), full((1, 256)),
    ]
    out_shape = [
        jax.ShapeDtypeStruct((b, s, 256), BF16),
        jax.ShapeDtypeStruct((b, s, 256), BF16),
        jax.ShapeDtypeStruct((b, s, 256), BF16),
        jax.ShapeDtypeStruct((b, s, 256), BF16),
        jax.ShapeDtypeStruct((b, s, 256), BF16),
        jax.ShapeDtypeStruct((b, s, 128), F32),
        jax.ShapeDtypeStruct((b, s, 512), BF16),
        jax.ShapeDtypeStruct((b, 2, s // tq, 128, 2 * tq), BF16),
        jax.ShapeDtypeStruct((b, s, 128), BF16),
        jax.ShapeDtypeStruct((b, 2, s // ATTN_TK, HEAD + 16, ATTN_TK), BF16),
    ]
    out_specs = [tok(256), tok(256), tok(256), tok(256), tok(256), tok(128), tok(512),
                 pl.BlockSpec((1, 2, tm // tq, 128, 2 * tq), lambda bi, i: (bi, 0, i, 0, 0)), tok(128),
                 pl.BlockSpec((1, 2, tm // ATTN_TK, HEAD + 16, ATTN_TK), lambda bi, i: (bi, 0, i, 0, 0))]
    return pl.pallas_call(
        functools.partial(_proj_kernel, tm=tm, tq=tq, n_tiles=n_tiles, seq_len=s),
        grid=(b, n_tiles), in_specs=in_specs, out_specs=out_specs, out_shape=out_shape,
        scratch_shapes=[pltpu.VMEM((tm + 2 * HALO, D_MODEL), BF16)],
        compiler_params=_cparams(("parallel", "parallel")), name="proj",
    )(x, x, x, lw["norm_w"], lw["w_halo"], lw["w_main"], lw["sgu_w"], lw["sgu_b"], lw["conv_w"],
      lw["dn_row"], lw["q_norm_w"], lw["k_norm_w"], rope_c, rope_s, lw["pool_w"], lw["pool_scale"])


def _dn_kernel(qf_ref, kf_ref, vf_ref, bgf_ref, qb_ref, kb_ref, vb_ref, bgb_ref,
               of_ref, ob_ref, state_ref, sums_ref, expand_ref, ones_ref, bd_ref, *, nb, tb):
    nc = tb // DN_CHUNK

    @pl.when(pl.program_id(0) == 0)
    def _():
        state_ref[...] = jnp.zeros_like(state_ref)
        r = _iota((tb, tb), 0)
        c = _iota((tb, tb), 1)
        same = jnp.right_shift(r, 6) == jnp.right_shift(c, 6)
        src = _iota((128, 768), 0)
        col = _iota((128, 768), 1)
        for d in range(2):
            m_in = _ones_where(same & ((c <= r) if d == 0 else (c >= r)))
            m_st = _ones_where(same & ((c > r) if d == 0 else (c < r)))
            m2 = jnp.concatenate([m_in, m_st], axis=0)
            sums_ref[d] = jnp.concatenate([m2, m2], axis=1)
            expand_ref[d] = _ones_where((src & 63) == jnp.left_shift(jnp.right_shift(col, 8), 3) + 4 * d
                                        + (jnp.right_shift(col, 6) & 3))
        ones_ref[...] = jnp.concatenate([_ones_where(same), _ones_where(same)], axis=1)
        bd_ref[...] = _ones_where(_bd_mask())

    li = _iota((tb, 256), 0) & 63
    lj = _iota((tb, 256), 1) & 63
    eyecat = li == lj
    eyef = jnp.where(eyecat, 1.0, 0.0)
    j2 = ones_ref[...]
    bdm = _bd_mask()
    bdm01 = bd_ref[...]
    lane = _iota((tb, 128), 1)
    zpad = jnp.zeros((DN_CHUNK, 256), BF16)

    dir_consts = []
    for d in range(2):
        dir_consts.append(dict(m2=sums_ref[d], expand=expand_ref[d],
                               incl=(li >= lj) if d == 0 else (li <= lj),
                               strict=(li > lj) if d == 0 else (li < lj)))
    blocks = []
    for d in range(2):
        q_ref, k_ref, v_ref, bg_ref = ((qf_ref, kf_ref, vf_ref, bgf_ref), (qb_ref, kb_ref, vb_ref, bgb_ref))[d]
        for bi in range(nb):
            blocks.append(dict(d=d, bi=bi, q_ref=q_ref, k_ref=k_ref, v_ref=v_ref, bg=bg_ref[bi], **dir_consts[d]))
    for blk in blocks:
        blk["cs"] = jnp.dot(blk["m2"], jnp.concatenate(_split2(blk["bg"]), axis=0),
                            preferred_element_type=F32)
    for blk in blocks:
        cs = blk["cs"]
        nbx = jnp.where(lane < 8, blk["bg"], jnp.where(lane < 16, cs[0:tb], pltpu.roll(cs[tb:2 * tb], 8, 1)))
        hi = nbx.astype(BF16)
        lo = pltpu.roll(nbx - hi.astype(F32), 64, 1)
        packed = jnp.where(lane < 64, hi.astype(F32), lo).astype(BF16)
        blk["x"] = jnp.dot(packed, blk["expand"], preferred_element_type=F32)
    for blk in blocks:
        gcx = blk["x"][:, 256:512]
        blk["rowf"] = jnp.dot(j2, jnp.concatenate(_split2(jnp.where(eyecat, gcx, 0.0)), axis=0),
                              preferred_element_type=F32)
    for blk in blocks:
        gram, qk = [], []
        for ci in range(nc):
            sl = slice(ci * DN_CHUNK, (ci + 1) * DN_CHUNK)
            kc = blk["k_ref"][blk["bi"], sl, :]
            gq = lax.dot_general(jnp.concatenate([kc, blk["q_ref"][blk["bi"], sl, :]], axis=0), _bd(kc, bdm01),
                                 (((1,), (1,)), ((), ())), preferred_element_type=F32)
            gram.append(gq[0:64])
            qk.append(gq[64:128])
        blk["gram"] = jnp.concatenate(gram, axis=0)
        blk["qk"] = jnp.concatenate(qk, axis=0)
    chains = {}
    for blk in blocks:
        d, bi = blk["d"], blk["bi"]
        q = blk["q_ref"][bi].astype(F32)
        k = blk["k_ref"][bi].astype(F32)
        v = blk["v_ref"][bi].astype(F32)
        x = blk["x"]
        bx, gcx, dglx = x[:, 0:256], x[:, 256:512], x[:, 512:768]
        decay = jnp.exp(jnp.where(blk["incl"], gcx - blk["rowf"], NEG_BIG))
        eg = jnp.exp(gcx)
        a_all = jnp.where(blk["strict"], bx * blk["gram"] * decay, 0.0)
        p_all = eyef - a_all
        intra = jnp.where(blk["incl"], blk["qk"] * decay, 0.0).astype(BF16)
        qd = (q * eg).astype(BF16)
        kdt = (k * jnp.exp(dglx)).T.astype(BF16)
        vbeta = (v * bx).astype(BF16)
        kbg = (k * bx * eg).astype(BF16)
        for ci in range(nc):
            sl = slice(ci * DN_CHUNK, (ci + 1) * DN_CHUNK)
            last = ci * DN_CHUNK + (DN_CHUNK - 1 if d == 0 else 0)
            pair = ci // 2
            chains[(d, bi, ci)] = dict(
                ak=a_all[sl], p=p_all[sl], vb=vbeta[sl], kbg=kbg[sl], intra=intra[sl], qd=qd[sl],
                kpair=kdt[:, pair * 128:(pair + 1) * 128], egl=eg[last:last + 1, :])

    def prepare(group, phase):
        for ch in group:
            if phase < 5:
                akb = ch["ak"].astype(BF16)
                lhs = akb if phase == 0 else jnp.concatenate([akb, ch["p"].astype(BF16)], axis=0)
                res = jnp.dot(lhs, _bd(akb, bdm01), preferred_element_type=F32)
                ch["ak"] = res[0:64]
                if phase > 0:
                    ch["p"] = ch["p"] + res[64:128]
            elif phase == 5:
                ch["p"] = ch["p"] + jnp.dot(ch["p"].astype(BF16), _bd(ch["ak"].astype(BF16), bdm01),
                                            preferred_element_type=F32)
            else:
                rhs = jnp.concatenate([_bd(ch["vb"], bdm01), _bd(ch["kbg"], bdm01)], axis=1)
                uw = jnp.dot(ch["p"].astype(BF16), rhs, preferred_element_type=F32)
                ch["u"] = uw[:, 0:256]
                ch["w"] = uw[:, 256:512].astype(BF16)

    def recur_a(group):
        for ch in group:
            ch["state"] = state_ref[ch["si"]]
            res = jnp.dot(jnp.concatenate([ch["w"], ch["qd"]], axis=0), ch["state"].astype(BF16),
                          preferred_element_type=F32)
            ch["v_new"] = (ch["u"] - res[0:64]).astype(BF16)
            ch["o_inter"] = res[64:128]

    def recur_b(group):
        for ch in group:
            v_new = ch["v_new"]
            ch["o_ref"][ch["bi"], ch["sl"], :] = (ch["o_inter"] + jnp.dot(
                ch["intra"], _bd(v_new, bdm01), preferred_element_type=F32)).astype(BF16)
            vpad = jnp.concatenate([v_new, zpad] if ch["even"] else [zpad, v_new], axis=0)
            ds = jnp.dot(ch["kpair"], vpad, preferred_element_type=F32)
            state_ref[ch["si"]] = ch["state"] * ch["egl"] + jnp.where(bdm, ds, 0.0)

    groups = []
    for step in range(nc):
        group = []
        for bi in range(nb):
            for d in range(2):
                ci = step if d == 0 else nc - 1 - step
                ch = chains[(d, bi, ci)]
                ch.update(si=d * nb + bi, bi=bi, o_ref=(of_ref, ob_ref)[d], even=ci % 2 == 0,
                          sl=slice(ci * DN_CHUNK, (ci + 1) * DN_CHUNK))
                group.append(ch)
        groups.append(group)

    for phase in range(7):
        prepare(groups[0], phase)
    for step in range(nc):
        nxt = groups[step + 1] if step + 1 < nc else []
        prepare(nxt, 0)
        prepare(nxt, 1)
        recur_a(groups[step])
        prepare(nxt, 2)
        prepare(nxt, 3)
        recur_b(groups[step])
        prepare(nxt, 4)
        prepare(nxt, 5)
        prepare(nxt, 6)


def _dn_call(dq, dk, dv, bg, tb):
    b, s, _ = dq.shape
    n = s // tb
    fwd = lambda w: pl.BlockSpec((b, tb, w), lambda i: (0, i, 0))
    bwd = lambda w: pl.BlockSpec((b, tb, w), lambda i: (0, n - 1 - i, 0))
    return pl.pallas_call(
        functools.partial(_dn_kernel, nb=b, tb=tb),
        grid=(n,),
        in_specs=[fwd(256), fwd(256), fwd(256), fwd(128), bwd(256), bwd(256), bwd(256), bwd(128)],
        out_specs=[fwd(256), bwd(256)],
        out_shape=[jax.ShapeDtypeStruct((b, s, 256), BF16)] * 2,
        scratch_shapes=[pltpu.VMEM((2 * b, 256, 256), F32), pltpu.VMEM((2, 2 * tb, 2 * tb), BF16),
                        pltpu.VMEM((2, 128, 768), BF16), pltpu.VMEM((tb, 2 * tb), BF16),
                        pltpu.VMEM((256, 256), BF16)],
        compiler_params=_cparams(("arbitrary",)), name="dn",
    )(dq, dk, dv, bg, dq, dk, dv, bg)


def _attn_kernel(qt_ref, k_ref, vt_ref, o_ref, st_ref, *, tq, tk, n_q, n_kv, unroll):
    per_q = n_kv // unroll
    n_trips = n_q * per_q

    def scores(qi, j, slot):
        kj = k_ref[0, pl.ds(pl.multiple_of(j * tk, tk), tk), :]
        st = jnp.dot(kj, qt_ref[0, 0, qi], preferred_element_type=F32)
        st_ref[slot] = st
        return jnp.max(st, axis=0, keepdims=True)

    def update(j, st, mx, m, acc):
        m_new = jnp.maximum(m, mx)
        p = jnp.exp2(st - m_new).astype(BF16)
        alpha = jnp.exp2(m - m_new)
        return m_new, alpha * acc + jnp.dot(vt_ref[0, 0, j], p, preferred_element_type=F32)

    def body(t, carry):
        qi = t // per_q
        base = (t - qi * per_q) * unroll
        first = base == 0
        m = jnp.where(first, -jnp.inf, carry[0])
        acc = jnp.where(first, 0.0, carry[1])
        mx = carry[2]
        t_next = jnp.minimum(t + 1, n_trips - 1)
        qi_next = t_next // per_q
        base_next = (t_next - qi_next * per_q) * unroll
        for r in range(unroll):
            if r < unroll - 1:
                mx_next = scores(qi, base + r + 1, (r + 1) % 2)
            else:
                mx_next = scores(qi_next, base_next, 0)
            m, acc = update(base + r, st_ref[r % 2], mx, m, acc)
            mx = mx_next

        @pl.when(base == n_kv - unroll)
        def _():
            o = acc[0:HEAD] / acc[HEAD:HEAD + 1]
            ot = jnp.concatenate([o[:, 0:tq], o[:, tq:2 * tq]], axis=0)
            o_ref[0, pl.ds(pl.multiple_of(qi * tq, tq), tq), :] = ot.T.astype(BF16)

        return m, acc, mx

    m0 = jnp.full((1, 2 * tq), -jnp.inf, F32)
    a0 = jnp.zeros((HEAD + 16, 2 * tq), F32)
    lax.fori_loop(0, n_trips, body, (m0, a0, scores(0, 0, 0)))


def _attn_call(aqt, ak, avt):
    b, s, _ = ak.shape
    _, _, n_q, _, tq2 = aqt.shape
    _, _, n_kv, vrows, tk = avt.shape
    unroll = next(u for u in (16, 8, 4, 2) if n_kv % u == 0)
    return pl.pallas_call(
        functools.partial(_attn_kernel, tq=tq2 // 2, tk=tk, n_q=n_q, n_kv=n_kv, unroll=unroll),
        grid=(b, 2),
        in_specs=[pl.BlockSpec((1, 1, n_q, 128, tq2), lambda bi, g: (bi, g, 0, 0, 0)),
                  pl.BlockSpec((1, s, 128), lambda bi, g: (bi, 0, 0)),
                  pl.BlockSpec((1, 1, n_kv, vrows, tk), lambda bi, g: (bi, g, 0, 0, 0))],
        out_specs=pl.BlockSpec((1, s, 128), lambda bi, g: (bi, 0, g)),
        out_shape=jax.ShapeDtypeStruct((b, s, 256), BF16),
        scratch_shapes=[pltpu.VMEM((2, tk, tq2), F32)],
        compiler_params=_cparams(("parallel", "parallel")), name="attn",
    )(aqt, ak, avt)


def _out_kernel(x_ref, ya_ref, yd_ref, yc_ref, of_ref, ob_ref, gates_ref, dnw_ref, wo_ref, o_ref):
    o = of_ref[0].astype(F32) + ob_ref[0].astype(F32)
    gates = gates_ref[0].astype(F32)
    on = o * lax.rsqrt(_head_sum(o * o) * (1.0 / HEAD) + EPS) * dnw_ref[...]
    yb = (on * gates[:, 0:256]).astype(BF16)
    yc = (yc_ref[0].astype(F32) * gates[:, 256:512]).astype(BF16)
    mix = jnp.concatenate([ya_ref[0], yb, yc, yd_ref[0]], axis=1)
    o_ref[0] = x_ref[0] + jnp.dot(mix, wo_ref[...], preferred_element_type=F32)


def _out_call(x, ya, yd, yc, o_f, o_b, gates, lw, tm):
    b, s, _ = x.shape
    tok = lambda w: pl.BlockSpec((1, tm, w), lambda bi, i: (bi, i, 0))
    full = lambda shape: pl.BlockSpec(shape, lambda bi, i: (0,) * len(shape))
    return pl.pallas_call(
        _out_kernel, grid=(b, s // tm),
        in_specs=[tok(D_MODEL), tok(256), tok(256), tok(256), tok(256), tok(256), tok(512),
                  full((1, 256)), full((D_MODEL, D_MODEL))],
        out_specs=tok(D_MODEL), out_shape=jax.ShapeDtypeStruct((b, s, D_MODEL), F32),
        compiler_params=_cparams(("parallel", "parallel")), name="out",
    )(x, ya, yd, yc, o_f, o_b, gates, lw["dn_norm_w"], lw["w_out"])


def _rope_tables(seq_len):
    t = jnp.arange(seq_len)
    pos = jnp.stack([t // GRID_W, t % GRID_W], axis=-1).astype(F32)
    n_freq = HEAD // 4
    inv_freq = jnp.power(ROPE_THETA, -2.0 * jnp.arange(n_freq, dtype=F32) / (HEAD // 2))
    ang = pos[:, :, None] * inv_freq
    cos = jnp.repeat(jnp.cos(ang)[:, :, None, :], 2, axis=2).reshape(seq_len, HEAD)
    sin = jnp.sin(ang)
    sin = jnp.stack([-sin, sin], axis=2).reshape(seq_len, HEAD)
    return jnp.tile(cos, (1, 2)), jnp.tile(sin, (1, 2))


def _layer_weights(l, norm_w, w_in, sgu_w, sgu_b, conv_w, a_log, dt_bias, dn_norm_w, q_norm_w,
                   k_norm_w, pool_w, pool_scale, w_out):
    w = w_in[l]
    cols = lambda a, n: w[:, a:a + n]
    w_halo = jnp.concatenate([cols(_B_Q, 768), cols(_D_X, 256)], axis=1).astype(BF16)
    w_main = jnp.concatenate([
        cols(_A_U, 768), cols(_B_Z, 256), cols(_C_Q, 768), cols(_D_Z, 256), cols(_B_BETA, 16),
        jnp.zeros((D_MODEL, 112), F32)], axis=1).astype(BF16)
    dn_row = jnp.zeros((8, 128), F32)
    dn_row = dn_row.at[0, 8:16].set(a_log[l].reshape(8)).at[1, 8:16].set(dt_bias[l].reshape(8))
    pool_bd = jnp.zeros((256, 256), F32)
    for gi in range(len(POOL_WINDOWS)):
        pool_bd = pool_bd.at[gi * 64:(gi + 1) * 64, gi * 64:(gi + 1) * 64].set(pool_w[l, gi])
    return {
        "norm_w": norm_w[l].reshape(1, D_MODEL),
        "w_halo": w_halo, "w_main": w_main,
        "sgu_w": jnp.transpose(sgu_w[l], (1, 0, 2)).reshape(SGU_CHUNK, 4 * SGU_CHUNK).astype(BF16),
        "sgu_b": jnp.repeat(sgu_b[l].T, HEAD, axis=1),
        "conv_w": jnp.concatenate([conv_w[l], jnp.zeros((3, 768), F32)], axis=0),
        "dn_row": dn_row,
        "q_norm_w": jnp.tile(q_norm_w[l], 4).reshape(1, 256),
        "k_norm_w": jnp.tile(k_norm_w[l], 2).reshape(1, 128),
        "pool_w": pool_bd.astype(BF16),
        "pool_scale": pool_scale[l].reshape(1, 256),
        "dn_norm_w": jnp.tile(dn_norm_w[l], 4).reshape(1, 256),
        "w_out": w_out[l].astype(BF16),
    }


def _tiles(batch, seq_len):
    tb = min(DN_CHUNK * max(1, 8 // batch), seq_len)
    return dict(tm=min(512, seq_len), tb=tb, tq=min(256, seq_len))


def _layer(x, lw, rope_c, rope_s):
    t = _tiles(x.shape[0], x.shape[1])
    ya, yd, dq, dk, dv, bg, gates, aqt, ak, avt = _proj_call(x, lw, rope_c, rope_s, t["tm"], t["tq"])
    o_f, o_b = _dn_call(dq, dk, dv, bg, t["tb"])
    yc = _attn_call(aqt, ak, avt)
    return _out_call(x, ya, yd, yc, o_f, o_b, gates, lw, t["tm"])


def kernel(x_prompt, x_sample, norm_w, w_in, sgu_w, sgu_b, conv_w, a_log, dt_bias, dn_norm_w,
           q_norm_w, k_norm_w, pool_w, pool_scale, w_out):
    depth = norm_w.shape[0]
    rope_p = _rope_tables(x_prompt.shape[1])
    rope_s = _rope_tables(x_sample.shape[1])
    y_prompt, y_sample = x_prompt, x_sample
    for l in range(depth):
        lw = _layer_weights(l, norm_w, w_in, sgu_w, sgu_b, conv_w, a_log, dt_bias, dn_norm_w,
                            q_norm_w, k_norm_w, pool_w, pool_scale, w_out)
        y_prompt = _layer(y_prompt, lw, *rope_p)
        y_sample = _layer(y_sample, lw, *rope_s)
    return (y_prompt, y_sample)
```

```python
import functools

import jax
import jax.numpy as jnp
from jax import lax
from jax.experimental import pallas as pl
from jax.experimental.pallas import tpu as pltpu

F32 = jnp.float32
BF16 = jnp.bfloat16

D_MODEL = 1024
HEAD = 64
N_HEADS = 4
GRID_W = 64
EPS = 1e-6
SGU_CHUNK = 128
DN_CHUNK = 64
ROPE_THETA = 10000.0
POOL_WINDOWS = (2, 4, 8, 16)
HALO = 16
NEG_BIG = -1e30
LOG2_E = 1.4426950408889634
ATTN_TK = 512

V7X_VMEM_LIMIT_BYTES = 56 * 1024 * 1024

_A_U = 0
_B_Q, _B_Z, _B_BETA = 768, 1536, 1792
_C_Q = 1808
_D_X, _D_Z = 2576, 2832
MAIN_COLS = 2176


def _cparams(semantics):
    return pltpu.CompilerParams(dimension_semantics=semantics,
                                vmem_limit_bytes=V7X_VMEM_LIMIT_BYTES)


def _split2(x):
    hi = x.astype(BF16)
    lo = (x - hi.astype(F32)).astype(BF16)
    return hi, lo


def _iota(shape, dim):
    return lax.broadcasted_iota(jnp.int32, shape, dim)


def _ones_where(cond):
    return jnp.where(cond, 1.0, 0.0).astype(BF16)


def _head_sum(x2):
    w = x2.shape[1]
    g = _ones_where(jnp.right_shift(_iota((w, w), 0), 6) == jnp.right_shift(_iota((w, w), 1), 6))
    return jnp.dot(x2.astype(BF16), g, preferred_element_type=F32)


def _silu(z):
    return 0.5 * z * (1.0 + jnp.tanh(0.5 * z))


def _bd_mask():
    return jnp.right_shift(_iota((256, 256), 0), 6) == jnp.right_shift(_iota((256, 256), 1), 6)


def _bd(x, mask01):
    z = jnp.zeros((HEAD, 128), x.dtype)
    blocks = []
    for h in range(N_HEADS):
        t = h // 2
        m = x[:, t * 128:(t + 1) * 128] * mask01[h * HEAD:(h + 1) * HEAD, t * 128:(t + 1) * 128]
        blocks.append(jnp.concatenate([m, z] if t == 0 else [z, m], axis=1))
    return jnp.concatenate(blocks, axis=0)


def _proj_kernel(xp_ref, xc_ref, xn_ref, nw_ref, wh_ref, wm_ref, sguw_ref, sgub_ref, convw_ref,
                 dnrow_ref, qw_ref, kw_ref, rc_ref, rs_ref, poolw_ref, pools_ref,
                 ya_ref, yd_ref, dq_ref, dk_ref, dv_ref, bg_ref, gates_ref, aqt_ref, ak_ref, avt_ref,
                 hext_ref, *, tm, tq, n_tiles, seq_len):
    i = pl.program_id(1)
    n_ext = tm + 2 * HALO
    nw = nw_ref[...]

    def norm(x):
        ms = jnp.mean(x * x, axis=-1, keepdims=True)
        return x * lax.rsqrt(ms + EPS) * nw

    hext_ref[0:HALO, :] = jnp.where(i > 0, norm(xp_ref[0]), 0.0).astype(BF16)
    hext_ref[HALO:HALO + tm, :] = norm(xc_ref[0]).astype(BF16)
    hext_ref[HALO + tm:n_ext, :] = jnp.where(i < n_tiles - 1, norm(xn_ref[0]), 0.0).astype(BF16)

    ph = jnp.dot(hext_ref[...], wh_ref[...], preferred_element_type=F32)
    pm = jnp.dot(hext_ref[HALO:HALO + tm, :], wm_ref[...], preferred_element_type=F32)

    def rows(x):
        return x[HALO:HALO + tm]

    a_u, a_v, a_z = pm[:, 0:256], pm[:, 256:512], pm[:, 512:768]
    vn = a_v * lax.rsqrt(_head_sum(a_v * a_v) * (1.0 / HEAD) + EPS)
    mask4 = _ones_where(jnp.right_shift(_iota((512, 256), 0), 7) == jnp.right_shift(_iota((512, 256), 1), 6))
    mixed = []
    for c in range(tm // SGU_CHUNK):
        vc = vn[c * SGU_CHUNK:(c + 1) * SGU_CHUNK].astype(BF16)
        bdv = jnp.concatenate([vc, vc, vc, vc], axis=0) * mask4
        mixed.append(jnp.dot(sguw_ref[...], bdv, preferred_element_type=F32) + sgub_ref[...])
    mixed = jnp.concatenate(mixed, axis=0)
    ya_ref[0] = (a_u * mixed * _silu(a_z)).astype(BF16)

    xd = ph[:, 768:1024]
    a1 = xd + pltpu.roll(xd, n_ext - 1, 0)
    a2 = a1 + pltpu.roll(a1, n_ext - 2, 0)
    a3 = a2 + pltpu.roll(a2, n_ext - 4, 0)
    a4 = a3 + pltpu.roll(a3, n_ext - 8, 0)
    w2 = rows(pltpu.roll(a1, 1, 0))
    w4 = rows(pltpu.roll(a2, 2, 0))
    w8 = rows(pltpu.roll(a3, 4, 0))
    w16 = rows(pltpu.roll(a4, 8, 0))
    grp = jnp.right_shift(_iota((tm, 256), 1), 6)
    half = jnp.left_shift(jnp.ones((tm, 256), jnp.int32), grp)
    t = i * tm + _iota((tm, 256), 0)
    cnt = (jnp.minimum(t + half, seq_len) - jnp.maximum(t - half, 0)).astype(F32)
    win = jnp.where(grp == 0, w2, jnp.where(grp == 1, w4, jnp.where(grp == 2, w8, w16)))
    diff = win / cnt - rows(xd)
    yd = jnp.dot(diff.astype(BF16), poolw_ref[...], preferred_element_type=F32) * pools_ref[...]
    yd_ref[0] = (yd * _silu(pm[:, 1792:2048])).astype(BF16)

    xb = ph[:, 0:768]
    cw = convw_ref[...]
    conv = (rows(pltpu.roll(xb, 2, 0)) * cw[0:1] + rows(pltpu.roll(xb, 1, 0)) * cw[1:2]
            + rows(xb) * cw[2:3] + rows(pltpu.roll(xb, n_ext - 1, 0)) * cw[3:4]
            + rows(pltpu.roll(xb, n_ext - 2, 0)) * cw[4:5])
    act = _silu(conv)
    bq, bk = act[:, 0:256], act[:, 256:512]
    dq_ref[0] = (bq * lax.rsqrt(_head_sum(bq * bq) + EPS) * (HEAD ** -0.5)).astype(BF16)
    dk_ref[0] = (bk * lax.rsqrt(_head_sum(bk * bk) + EPS)).astype(BF16)
    dv_ref[0] = act[:, 512:768].astype(BF16)
    ba = pm[:, 2048:2176]
    lane = _iota((tm, 128), 1)
    xa = ba + dnrow_ref[1:2, :]
    softplus = jnp.maximum(xa, 0.0) + jnp.log1p(jnp.exp(-jnp.abs(xa)))
    g = -jnp.exp(dnrow_ref[0:1, :]) * softplus
    bg_ref[0] = jnp.where(lane < 8, 1.0 / (1.0 + jnp.exp(-ba)), g)
    gates_ref[0] = jnp.concatenate([_silu(pm[:, 768:1024]), _silu(pm[:, 1536:1792])], axis=1).astype(BF16)

    rc, rs = rc_ref[...], rs_ref[...]
    first = (_iota((tm, 128), 1) & 16) == 0

    def rope(x):
        sw = jnp.where(first, pltpu.roll(x, 112, 1), pltpu.roll(x, 16, 1))
        return x * rc + sw * rs

    cq, ck = pm[:, 1024:1280], pm[:, 1280:1408]
    qn = cq * lax.rsqrt(_head_sum(cq * cq) * (1.0 / HEAD) + EPS) * qw_ref[...]
    qr = jnp.concatenate([rope(qn[:, 0:128]), rope(qn[:, 128:256])], axis=1) * (HEAD ** -0.5 * LOG2_E)
    qt = qr.T.astype(BF16)
    zq = jnp.zeros((HEAD, 2 * tq), BF16)
    for r in range(tm // tq):
        cs = slice(r * tq, (r + 1) * tq)
        top0 = jnp.concatenate([qt[0:64, cs], qt[64:128, cs]], axis=1)
        top1 = jnp.concatenate([qt[128:192, cs], qt[192:256, cs]], axis=1)
        aqt_ref[0, 0, r] = jnp.concatenate([top0, zq], axis=0)
        aqt_ref[0, 1, r] = jnp.concatenate([zq, top1], axis=0)
    kn = ck * lax.rsqrt(_head_sum(ck * ck) * (1.0 / HEAD) + EPS) * kw_ref[...]
    ak_ref[0] = rope(kn).astype(BF16)
    vt = pm[:, 1408:1536].T
    ones = jnp.ones((16, tm), F32)
    for g in range(2):
        vg = jnp.concatenate([vt[g * HEAD:(g + 1) * HEAD], ones], axis=0).astype(BF16)
        for r in range(tm // ATTN_TK):
            avt_ref[0, g, r] = vg[:, r * ATTN_TK:(r + 1) * ATTN_TK]


def _proj_call(x, lw, rope_c, rope_s, tm, tq):
    b, s, _ = x.shape
    n_tiles = s // tm
    hb = tm // HALO
    last_hb = s // HALO - 1
    full = lambda shape: pl.BlockSpec(shape, lambda bi, i: (0,) * len(shape))
    tok = lambda w: pl.BlockSpec((1, tm, w), lambda bi, i: (bi, i, 0))
    in_specs = [
        pl.BlockSpec((1, HALO, D_MODEL), lambda bi, i: (bi, jnp.maximum(i * hb - 1, 0), 0)),
        pl.BlockSpec((1, tm, D_MODEL), lambda bi, i: (bi, i, 0)),
        pl.BlockSpec((1, HALO, D_MODEL), lambda bi, i: (bi, jnp.minimum((i + 1) * hb, last_hb), 0)),
        full((1, D_MODEL)), full((D_MODEL, 1024)), full((D_MODEL, MAIN_COLS)),
        full((128, 512)), full((128, 256)), full((8, 768)), full((8, 128)),
        full((1, 256)), full((1, 128)),
        pl.BlockSpec((tm, 128), lambda bi, i: (i, 0)), pl.BlockSpec((tm, 128), lambda bi, i: (i, 0)),
        full((256, 256)), full((1, 256)),
    ]
    out_shape = [
        jax.ShapeDtypeStruct((b, s, 256), BF16),
        jax.ShapeDtypeStruct((b, s, 256), BF16),
        jax.ShapeDtypeStruct((b, s, 256), BF16),
        jax.ShapeDtypeStruct((b, s, 256), BF16),
        jax.ShapeDtypeStruct((b, s, 256), BF16),
        jax.ShapeDtypeStruct((b, s, 128), F32),
        jax.ShapeDtypeStruct((b, s, 512), BF16),
        jax.ShapeDtypeStruct((b, 2, s // tq, 128, 2 * tq), BF16),
        jax.ShapeDtypeStruct((b, s, 128), BF16),
        jax.ShapeDtypeStruct((b, 2, s // ATTN_TK, HEAD + 16, ATTN_TK), BF16),
    ]
    out_specs = [tok(256), tok(256), tok(256), tok(256), tok(256), tok(128), tok(512),
                 pl.BlockSpec((1, 2, tm // tq, 128, 2 * tq), lambda bi, i: (bi, 0, i, 0, 0)), tok(128),
                 pl.BlockSpec((1, 2, tm // ATTN_TK, HEAD + 16, ATTN_TK), lambda bi, i: (bi, 0, i, 0, 0))]
    return pl.pallas_call(
        functools.partial(_proj_kernel, tm=tm, tq=tq, n_tiles=n_tiles, seq_len=s),
        grid=(b, n_tiles), in_specs=in_specs, out_specs=out_specs, out_shape=out_shape,
        scratch_shapes=[pltpu.VMEM((tm + 2 * HALO, D_MODEL), BF16)],
        compiler_params=_cparams(("parallel", "parallel")), name="proj",
    )(x, x, x, lw["norm_w"], lw["w_halo"], lw["w_main"], lw["sgu_w"], lw["sgu_b"], lw["conv_w"],
      lw["dn_row"], lw["q_norm_w"], lw["k_norm_w"], rope_c, rope_s, lw["pool_w"], lw["pool_scale"])


def _dn_kernel(qf_ref, kf_ref, vf_ref, bgf_ref, qb_ref, kb_ref, vb_ref, bgb_ref,
               of_ref, ob_ref, state_ref, sums_ref, expand_ref, ones_ref, bd_ref, *, nb, tb):
    nc = tb // DN_CHUNK

    @pl.when(pl.program_id(0) == 0)
    def _():
        state_ref[...] = jnp.zeros_like(state_ref)
        r = _iota((tb, tb), 0)
        c = _iota((tb, tb), 1)
        same = jnp.right_shift(r, 6) == jnp.right_shift(c, 6)
        src = _iota((128, 768), 0)
        col = _iota((128, 768), 1)
        for d in range(2):
            m_in = _ones_where(same & ((c <= r) if d == 0 else (c >= r)))
            m_st = _ones_where(same & ((c > r) if d == 0 else (c < r)))
            m2 = jnp.concatenate([m_in, m_st], axis=0)
            sums_ref[d] = jnp.concatenate([m2, m2], axis=1)
            expand_ref[d] = _ones_where((src & 63) == jnp.left_shift(jnp.right_shift(col, 8), 3) + 4 * d
                                        + (jnp.right_shift(col, 6) & 3))
        ones_ref[...] = jnp.concatenate([_ones_where(same), _ones_where(same)], axis=1)
        bd_ref[...] = _ones_where(_bd_mask())

    li = _iota((tb, 256), 0) & 63
    lj = _iota((tb, 256), 1) & 63
    eyecat = li == lj
    eyef = jnp.where(eyecat, 1.0, 0.0)
    j2 = ones_ref[...]
    bdm = _bd_mask()
    bdm01 = bd_ref[...]
    lane = _iota((tb, 128), 1)
    zpad = jnp.zeros((DN_CHUNK, 256), BF16)

    dir_consts = []
    for d in range(2):
        dir_consts.append(dict(m2=sums_ref[d], expand=expand_ref[d],
                               incl=(li >= lj) if d == 0 else (li <= lj),
                               strict=(li > lj) if d == 0 else (li < lj)))
    blocks = []
    for d in range(2):
        q_ref, k_ref, v_ref, bg_ref = ((qf_ref, kf_ref, vf_ref, bgf_ref), (qb_ref, kb_ref, vb_ref, bgb_ref))[d]
        for bi in range(nb):
            blocks.append(dict(d=d, bi=bi, q_ref=q_ref, k_ref=k_ref, v_ref=v_ref, bg=bg_ref[bi], **dir_consts[d]))
    for blk in blocks:
        blk["cs"] = jnp.dot(blk["m2"], jnp.concatenate(_split2(blk["bg"]), axis=0),
                            preferred_element_type=F32)
    for blk in blocks:
        cs = blk["cs"]
        nbx = jnp.where(lane < 8, blk["bg"], jnp.where(lane < 16, cs[0:tb], pltpu.roll(cs[tb:2 * tb], 8, 1)))
        hi = nbx.astype(BF16)
        lo = pltpu.roll(nbx - hi.astype(F32), 64, 1)
        packed = jnp.where(lane < 64, hi.astype(F32), lo).astype(BF16)
        blk["x"] = jnp.dot(packed, blk["expand"], preferred_element_type=F32)
    for blk in blocks:
        gcx = blk["x"][:, 256:512]
        blk["rowf"] = jnp.dot(j2, jnp.concatenate(_split2(jnp.where(eyecat, gcx, 0.0)), axis=0),
                              preferred_element_type=F32)
    for blk in blocks:
        gram, qk = [], []
        for ci in range(nc):
            sl = slice(ci * DN_CHUNK, (ci + 1) * DN_CHUNK)
            kc = blk["k_ref"][blk["bi"], sl, :]
            gq = lax.dot_general(jnp.concatenate([kc, blk["q_ref"][blk["bi"], sl, :]], axis=0), _bd(kc, bdm01),
                                 (((1,), (1,)), ((), ())), preferred_element_type=F32)
            gram.append(gq[0:64])
            qk.append(gq[64:128])
        blk["gram"] = jnp.concatenate(gram, axis=0)
        blk["qk"] = jnp.concatenate(qk, axis=0)
    chains = {}
    for blk in blocks:
        d, bi = blk["d"], blk["bi"]
        q = blk["q_ref"][bi].astype(F32)
        k = blk["k_ref"][bi].astype(F32)
        v = blk["v_ref"][bi].astype(F32)
        x = blk["x"]
        bx, gcx, dglx = x[:, 0:256], x[:, 256:512], x[:, 512:768]
        decay = jnp.exp(jnp.where(blk["incl"], gcx - blk["rowf"], NEG_BIG))
        eg = jnp.exp(gcx)
        a_all = jnp.where(blk["strict"], bx * blk["gram"] * decay, 0.0)
        p_all = eyef - a_all
        intra = jnp.where(blk["incl"], blk["qk"] * decay, 0.0).astype(BF16)
        qd = (q * eg).astype(BF16)
        kdt = (k * jnp.exp(dglx)).T.astype(BF16)
        vbeta = (v * bx).astype(BF16)
        kbg = (k * bx * eg).astype(BF16)
        for ci in range(nc):
            sl = slice(ci * DN_CHUNK, (ci + 1) * DN_CHUNK)
            last = ci * DN_CHUNK + (DN_CHUNK - 1 if d == 0 else 0)
            pair = ci // 2
            chains[(d, bi, ci)] = dict(
                ak=a_all[sl], p=p_all[sl], vb=vbeta[sl], kbg=kbg[sl], intra=intra[sl], qd=qd[sl],
                kpair=kdt[:, pair * 128:(pair + 1) * 128], egl=eg[last:last + 1, :])

    def prepare(group, phase):
        for ch in group:
            if phase < 5:
                akb = ch["ak"].astype(BF16)
                lhs = akb if phase == 0 else jnp.concatenate([akb, ch["p"].astype(BF16)], axis=0)
                res = jnp.dot(lhs, _bd(akb, bdm01), preferred_element_type=F32)
                ch["ak"] = res[0:64]
                if phase > 0:
                    ch["p"] = ch["p"] + res[64:128]
            elif phase == 5:
                ch["p"] = ch["p"] + jnp.dot(ch["p"].astype(BF16), _bd(ch["ak"].astype(BF16), bdm01),
                                            preferred_element_type=F32)
            else:
                rhs = jnp.concatenate([_bd(ch["vb"], bdm01), _bd(ch["kbg"], bdm01)], axis=1)
                uw = jnp.dot(ch["p"].astype(BF16), rhs, preferred_element_type=F32)
                ch["u"] = uw[:, 0:256]
                ch["w"] = uw[:, 256:512].astype(BF16)

    def recur_a(group):
        for ch in group:
            ch["state"] = state_ref[ch["si"]]
            res = jnp.dot(jnp.concatenate([ch["w"], ch["qd"]], axis=0), ch["state"].astype(BF16),
                          preferred_element_type=F32)
            ch["v_new"] = (ch["u"] - res[0:64]).astype(BF16)
            ch["o_inter"] = res[64:128]

    def recur_b(group):
        for ch in group:
            v_new = ch["v_new"]
            ch["o_ref"][ch["bi"], ch["sl"], :] = (ch["o_inter"] + jnp.dot(
                ch["intra"], _bd(v_new, bdm01), preferred_element_type=F32)).astype(BF16)
            vpad = jnp.concatenate([v_new, zpad] if ch["even"] else [zpad, v_new], axis=0)
            ds = jnp.dot(ch["kpair"], vpad, preferred_element_type=F32)
            state_ref[ch["si"]] = ch["state"] * ch["egl"] + jnp.where(bdm, ds, 0.0)

    groups = []
    for step in range(nc):
        group = []
        for bi in range(nb):
            for d in range(2):
                ci = step if d == 0 else nc - 1 - step
                ch = chains[(d, bi, ci)]
                ch.update(si=d * nb + bi, bi=bi, o_ref=(of_ref, ob_ref)[d], even=ci % 2 == 0,
                          sl=slice(ci * DN_CHUNK, (ci + 1) * DN_CHUNK))
                group.append(ch)
        groups.append(group)

    for phase in range(7):
        prepare(groups[0], phase)
    for step in range(nc):
        nxt = groups[step + 1] if step + 1 < nc else []
        prepare(nxt, 0)
        prepare(nxt, 1)
        recur_a(groups[step])
        prepare(nxt, 2)
        prepare(nxt, 3)
        recur_b(groups[step])
        prepare(nxt, 4)
        prepare(nxt, 5)
        prepare(nxt, 6)


def _dn_call(dq, dk, dv, bg, tb):
    b, s, _ = dq.shape
    n = s // tb
    fwd = lambda w: pl.BlockSpec((b, tb, w), lambda i: (0, i, 0))
    bwd = lambda w: pl.BlockSpec((b, tb, w), lambda i: (0, n - 1 - i, 0))
    return pl.pallas_call(
        functools.partial(_dn_kernel, nb=b, tb=tb),
        grid=(n,),
        in_specs=[fwd(256), fwd(256), fwd(256), fwd(128), bwd(256), bwd(256), bwd(256), bwd(128)],
        out_specs=[fwd(256), bwd(256)],
        out_shape=[jax.ShapeDtypeStruct((b, s, 256), BF16)] * 2,
        scratch_shapes=[pltpu.VMEM((2 * b, 256, 256), F32), pltpu.VMEM((2, 2 * tb, 2 * tb), BF16),
                        pltpu.VMEM((2, 128, 768), BF16), pltpu.VMEM((tb, 2 * tb), BF16),
                        pltpu.VMEM((256, 256), BF16)],
        compiler_params=_cparams(("arbitrary",)), name="dn",
    )(dq, dk, dv, bg, dq, dk, dv, bg)


def _attn_kernel(qt_ref, k_ref, vt_ref, o_ref, st_ref, *, tq, tk, n_q, n_kv, unroll):
    per_q = n_kv // unroll
    n_trips = n_q * per_q

    def scores(qi, j, slot):
        kj = k_ref[0, pl.ds(pl.multiple_of(j * tk, tk), tk), :]
        st = jnp.dot(kj, qt_ref[0, 0, qi], preferred_element_type=F32)
        st_ref[slot] = st
        return jnp.max(st, axis=0, keepdims=True)

    def update(j, st, mx, m, acc):
        m_new = jnp.maximum(m, mx)
        p = jnp.exp2(st - m_new).astype(BF16)
        alpha = jnp.exp2(m - m_new)
        return m_new, alpha * acc + jnp.dot(vt_ref[0, 0, j], p, preferred_element_type=F32)

    def body(t, carry):
        qi = t // per_q
        base = (t - qi * per_q) * unroll
        first = base == 0
        m = jnp.where(first, -jnp.inf, carry[0])
        acc = jnp.where(first, 0.0, carry[1])
        mx = carry[2]
        t_next = jnp.minimum(t + 1, n_trips - 1)
        qi_next = t_next // per_q
        base_next = (t_next - qi_next * per_q) * unroll
        for r in range(unroll):
            if r < unroll - 1:
                mx_next = scores(qi, base + r + 1, (r + 1) % 2)
            else:
                mx_next = scores(qi_next, base_next, 0)
            m, acc = update(base + r, st_ref[r % 2], mx, m, acc)
            mx = mx_next

        @pl.when(base == n_kv - unroll)
        def _():
            o = acc[0:HEAD] / acc[HEAD:HEAD + 1]
            ot = jnp.concatenate([o[:, 0:tq], o[:, tq:2 * tq]], axis=0)
            o_ref[0, pl.ds(pl.multiple_of(qi * tq, tq), tq), :] = ot.T.astype(BF16)

        return m, acc, mx

    m0 = jnp.full((1, 2 * tq), -jnp.inf, F32)
    a0 = jnp.zeros((HEAD + 16, 2 * tq), F32)
    lax.fori_loop(0, n_trips, body, (m0, a0, scores(0, 0, 0)))


def _attn_call(aqt, ak, avt):
    b, s, _ = ak.shape
    _, _, n_q, _, tq2 = aqt.shape
    _, _, n_kv, vrows, tk = avt.shape
    unroll = next(u for u in (32, 16, 8, 4, 2) if n_kv % u == 0)
    return pl.pallas_call(
        functools.partial(_attn_kernel, tq=tq2 // 2, tk=tk, n_q=n_q, n_kv=n_kv, unroll=unroll),
        grid=(b, 2),
        in_specs=[pl.BlockSpec((1, 1, n_q, 128, tq2), lambda bi, g: (bi, g, 0, 0, 0)),
                  pl.BlockSpec((1, s, 128), lambda bi, g: (bi, 0, 0)),
                  pl.BlockSpec((1, 1, n_kv, vrows, tk), lambda bi, g: (bi, g, 0, 0, 0))],
        out_specs=pl.BlockSpec((1, s, 128), lambda bi, g: (bi, 0, g)),
        out_shape=jax.ShapeDtypeStruct((b, s, 256), BF16),
        scratch_shapes=[pltpu.VMEM((2, tk, tq2), F32)],
        compiler_params=_cparams(("parallel", "parallel")), name="attn",
    )(aqt, ak, avt)


def _out_kernel(x_ref, ya_ref, yd_ref, yc_ref, of_ref, ob_ref, gates_ref, dnw_ref, wo_ref, o_ref):
    o = of_ref[0].astype(F32) + ob_ref[0].astype(F32)
    gates = gates_ref[0].astype(F32)
    on = o * lax.rsqrt(_head_sum(o * o) * (1.0 / HEAD) + EPS) * dnw_ref[...]
    yb = (on * gates[:, 0:256]).astype(BF16)
    yc = (yc_ref[0].astype(F32) * gates[:, 256:512]).astype(BF16)
    mix = jnp.concatenate([ya_ref[0], yb, yc, yd_ref[0]], axis=1)
    o_ref[0] = x_ref[0] + jnp.dot(mix, wo_ref[...], preferred_element_type=F32)


def _out_call(x, ya, yd, yc, o_f, o_b, gates, lw, tm):
    b, s, _ = x.shape
    tok = lambda w: pl.BlockSpec((1, tm, w), lambda bi, i: (bi, i, 0))
    full = lambda shape: pl.BlockSpec(shape, lambda bi, i: (0,) * len(shape))
    return pl.pallas_call(
        _out_kernel, grid=(b, s // tm),
        in_specs=[tok(D_MODEL), tok(256), tok(256), tok(256), tok(256), tok(256), tok(512),
                  full((1, 256)), full((D_MODEL, D_MODEL))],
        out_specs=tok(D_MODEL), out_shape=jax.ShapeDtypeStruct((b, s, D_MODEL), F32),
        compiler_params=_cparams(("parallel", "parallel")), name="out",
    )(x, ya, yd, yc, o_f, o_b, gates, lw["dn_norm_w"], lw["w_out"])


def _rope_tables(seq_len):
    t = jnp.arange(seq_len)
    pos = jnp.stack([t // GRID_W, t % GRID_W], axis=-1).astype(F32)
    n_freq = HEAD // 4
    inv_freq = jnp.power(ROPE_THETA, -2.0 * jnp.arange(n_freq, dtype=F32) / (HEAD // 2))
    ang = pos[:, :, None] * inv_freq
    cos = jnp.repeat(jnp.cos(ang)[:, :, None, :], 2, axis=2).reshape(seq_len, HEAD)
    sin = jnp.sin(ang)
    sin = jnp.stack([-sin, sin], axis=2).reshape(seq_len, HEAD)
    return jnp.tile(cos, (1, 2)), jnp.tile(sin, (1, 2))


def _layer_weights(l, norm_w, w_in, sgu_w, sgu_b, conv_w, a_log, dt_bias, dn_norm_w, q_norm_w,
                   k_norm_w, pool_w, pool_scale, w_out):
    w = w_in[l]
    cols = lambda a, n: w[:, a:a + n]
    w_halo = jnp.concatenate([cols(_B_Q, 768), cols(_D_X, 256)], axis=1).astype(BF16)
    w_main = jnp.concatenate([
        cols(_A_U, 768), cols(_B_Z, 256), cols(_C_Q, 768), cols(_D_Z, 256), cols(_B_BETA, 16),
        jnp.zeros((D_MODEL, 112), F32)], axis=1).astype(BF16)
    dn_row = jnp.zeros((8, 128), F32)
    dn_row = dn_row.at[0, 8:16].set(a_log[l].reshape(8)).at[1, 8:16].set(dt_bias[l].reshape(8))
    pool_bd = jnp.zeros((256, 256), F32)
    for gi in range(len(POOL_WINDOWS)):
        pool_bd = pool_bd.at[gi * 64:(gi + 1) * 64, gi * 64:(gi + 1) * 64].set(pool_w[l, gi])
    return {
        "norm_w": norm_w[l].reshape(1, D_MODEL),
        "w_halo": w_halo, "w_main": w_main,
        "sgu_w": jnp.transpose(sgu_w[l], (1, 0, 2)).reshape(SGU_CHUNK, 4 * SGU_CHUNK).astype(BF16),
        "sgu_b": jnp.repeat(sgu_b[l].T, HEAD, axis=1),
        "conv_w": jnp.concatenate([conv_w[l], jnp.zeros((3, 768), F32)], axis=0),
        "dn_row": dn_row,
        "q_norm_w": jnp.tile(q_norm_w[l], 4).reshape(1, 256),
        "k_norm_w": jnp.tile(k_norm_w[l], 2).reshape(1, 128),
        "pool_w": pool_bd.astype(BF16),
        "pool_scale": pool_scale[l].reshape(1, 256),
        "dn_norm_w": jnp.tile(dn_norm_w[l], 4).reshape(1, 256),
        "w_out": w_out[l].astype(BF16),
    }


def _tiles(batch, seq_len):
    tb = min(DN_CHUNK * max(1, 8 // batch), seq_len)
    return dict(tm=min(512, seq_len), tb=tb, tq=min(256, seq_len))


def _layer(x, lw, rope_c, rope_s):
    t = _tiles(x.shape[0], x.shape[1])
    ya, yd, dq, dk, dv, bg, gates, aqt, ak, avt = _proj_call(x, lw, rope_c, rope_s, t["tm"], t["tq"])
    o_f, o_b = _dn_call(dq, dk, dv, bg, t["tb"])
    yc = _attn_call(aqt, ak, avt)
    return _out_call(x, ya, yd, yc, o_f, o_b, gates, lw, t["tm"])


def kernel(x_prompt, x_sample, norm_w, w_in, sgu_w, sgu_b, conv_w, a_log, dt_bias, dn_norm_w,
           q_norm_w, k_norm_w, pool_w, pool_scale, w_out):
    depth = norm_w.shape[0]
    rope_p = _rope_tables(x_prompt.shape[1])
    rope_s = _rope_tables(x_sample.shape[1])
    y_prompt, y_sample = x_prompt, x_sample
    for l in range(depth):
        lw = _layer_weights(l, norm_w, w_in, sgu_w, sgu_b, conv_w, a_log, dt_bias, dn_norm_w,
                            q_norm_w, k_norm_w, pool_w, pool_scale, w_out)
        y_prompt = _layer(y_prompt, lw, *rope_p)
        y_sample = _layer(y_sample, lw, *rope_s)
    return (y_prompt, y_sample)
```

```python
import functools

import jax
import jax.numpy as jnp
from jax import lax
from jax.experimental import pallas as pl
from jax.experimental.pallas import tpu as pltpu

F32 = jnp.float32
BF16 = jnp.bfloat16

D_MODEL = 1024
HEAD = 64
N_HEADS = 4
GRID_W = 64
EPS = 1e-6
SGU_CHUNK = 128
DN_CHUNK = 64
ROPE_THETA = 10000.0
POOL_WINDOWS = (2, 4, 8, 16)
HALO = 16
NEG_BIG = -1e30
LOG2_E = 1.4426950408889634
ATTN_TK = 512

V7X_VMEM_LIMIT_BYTES = 56 * 1024 * 1024

_A_U = 0
_B_Q, _B_Z, _B_BETA = 768, 1536, 1792
_C_Q = 1808
_D_X, _D_Z = 2576, 2832
MAIN_COLS = 2176


def _cparams(semantics):
    return pltpu.CompilerParams(dimension_semantics=semantics,
                                vmem_limit_bytes=V7X_VMEM_LIMIT_BYTES)


def _split2(x):
    hi = x.astype(BF16)
    lo = (x - hi.astype(F32)).astype(BF16)
    return hi, lo


def _iota(shape, dim):
    return lax.broadcasted_iota(jnp.int32, shape, dim)


def _ones_where(cond):
    return jnp.where(cond, 1.0, 0.0).astype(BF16)


def _head_sum(x2):
    w = x2.shape[1]
    g = _ones_where(jnp.right_shift(_iota((w, w), 0), 6) == jnp.right_shift(_iota((w, w), 1), 6))
    return jnp.dot(x2.astype(BF16), g, preferred_element_type=F32)


def _silu(z):
    return 0.5 * z * (1.0 + jnp.tanh(0.5 * z))


def _bd_mask():
    return jnp.right_shift(_iota((256, 256), 0), 6) == jnp.right_shift(_iota((256, 256), 1), 6)


def _bd(x, mask01):
    z = jnp.zeros((HEAD, 128), x.dtype)
    blocks = []
    for h in range(N_HEADS):
        t = h // 2
        m = x[:, t * 128:(t + 1) * 128] * mask01[h * HEAD:(h + 1) * HEAD, t * 128:(t + 1) * 128]
        blocks.append(jnp.concatenate([m, z] if t == 0 else [z, m], axis=1))
    return jnp.concatenate(blocks, axis=0)


def _proj_kernel(xp_ref, xc_ref, xn_ref, nw_ref, wh_ref, wm_ref, sguw_ref, sgub_ref, convw_ref,
                 dnrow_ref, qw_ref, kw_ref, rc_ref, rs_ref, poolw_ref, pools_ref,
                 ya_ref, yd_ref, dq_ref, dk_ref, dv_ref, bg_ref, gates_ref, aqt_ref, ak_ref, avt_ref,
                 hext_ref, *, tm, tq, n_tiles, seq_len):
    i = pl.program_id(1)
    n_ext = tm + 2 * HALO
    nw = nw_ref[...]

    def norm(x):
        ms = jnp.mean(x * x, axis=-1, keepdims=True)
        return x * lax.rsqrt(ms + EPS) * nw

    hext_ref[0:HALO, :] = jnp.where(i > 0, norm(xp_ref[0]), 0.0).astype(BF16)
    hext_ref[HALO:HALO + tm, :] = norm(xc_ref[0]).astype(BF16)
    hext_ref[HALO + tm:n_ext, :] = jnp.where(i < n_tiles - 1, norm(xn_ref[0]), 0.0).astype(BF16)

    ph = jnp.dot(hext_ref[...], wh_ref[...], preferred_element_type=F32)
    pm = jnp.dot(hext_ref[HALO:HALO + tm, :], wm_ref[...], preferred_element_type=F32)

    def rows(x):
        return x[HALO:HALO + tm]

    a_u, a_v, a_z = pm[:, 0:256], pm[:, 256:512], pm[:, 512:768]
    vn = a_v * lax.rsqrt(_head_sum(a_v * a_v) * (1.0 / HEAD) + EPS)
    mask4 = _ones_where(jnp.right_shift(_iota((512, 256), 0), 7) == jnp.right_shift(_iota((512, 256), 1), 6))
    mixed = []
    for c in range(tm // SGU_CHUNK):
        vc = vn[c * SGU_CHUNK:(c + 1) * SGU_CHUNK].astype(BF16)
        bdv = jnp.concatenate([vc, vc, vc, vc], axis=0) * mask4
        mixed.append(jnp.dot(sguw_ref[...], bdv, preferred_element_type=F32) + sgub_ref[...])
    mixed = jnp.concatenate(mixed, axis=0)
    ya_ref[0] = (a_u * mixed * _silu(a_z)).astype(BF16)

    xd = ph[:, 768:1024]
    a1 = xd + pltpu.roll(xd, n_ext - 1, 0)
    a2 = a1 + pltpu.roll(a1, n_ext - 2, 0)
    a3 = a2 + pltpu.roll(a2, n_ext - 4, 0)
    a4 = a3 + pltpu.roll(a3, n_ext - 8, 0)
    w2 = rows(pltpu.roll(a1, 1, 0))
    w4 = rows(pltpu.roll(a2, 2, 0))
    w8 = rows(pltpu.roll(a3, 4, 0))
    w16 = rows(pltpu.roll(a4, 8, 0))
    grp = jnp.right_shift(_iota((tm, 256), 1), 6)
    half = jnp.left_shift(jnp.ones((tm, 256), jnp.int32), grp)
    t = i * tm + _iota((tm, 256), 0)
    cnt = (jnp.minimum(t + half, seq_len) - jnp.maximum(t - half, 0)).astype(F32)
    win = jnp.where(grp == 0, w2, jnp.where(grp == 1, w4, jnp.where(grp == 2, w8, w16)))
    diff = win / cnt - rows(xd)
    yd = jnp.dot(diff.astype(BF16), poolw_ref[...], preferred_element_type=F32) * pools_ref[...]
    yd_ref[0] = (yd * _silu(pm[:, 1792:2048])).astype(BF16)

    xb = ph[:, 0:768]
    cw = convw_ref[...]
    conv = (rows(pltpu.roll(xb, 2, 0)) * cw[0:1] + rows(pltpu.roll(xb, 1, 0)) * cw[1:2]
            + rows(xb) * cw[2:3] + rows(pltpu.roll(xb, n_ext - 1, 0)) * cw[3:4]
            + rows(pltpu.roll(xb, n_ext - 2, 0)) * cw[4:5])
    act = _silu(conv)
    bq, bk = act[:, 0:256], act[:, 256:512]
    dq_ref[0] = (bq * lax.rsqrt(_head_sum(bq * bq) + EPS) * (HEAD ** -0.5)).astype(BF16)
    dk_ref[0] = (bk * lax.rsqrt(_head_sum(bk * bk) + EPS)).astype(BF16)
    dv_ref[0] = act[:, 512:768].astype(BF16)
    ba = pm[:, 2048:2176]
    lane = _iota((tm, 128), 1)
    xa = ba + dnrow_ref[1:2, :]
    softplus = jnp.maximum(xa, 0.0) + jnp.log1p(jnp.exp(-jnp.abs(xa)))
    g = -jnp.exp(dnrow_ref[0:1, :]) * softplus
    bg_ref[0] = jnp.where(lane < 8, 1.0 / (1.0 + jnp.exp(-ba)), g)
    gates_ref[0] = jnp.concatenate([_silu(pm[:, 768:1024]), _silu(pm[:, 1536:1792])], axis=1).astype(BF16)

    rc, rs = rc_ref[...], rs_ref[...]
    first = (_iota((tm, 128), 1) & 16) == 0

    def rope(x):
        sw = jnp.where(first, pltpu.roll(x, 112, 1), pltpu.roll(x, 16, 1))
        return x * rc + sw * rs

    cq, ck = pm[:, 1024:1280], pm[:, 1280:1408]
    qn = cq * lax.rsqrt(_head_sum(cq * cq) * (1.0 / HEAD) + EPS) * qw_ref[...]
    qr = jnp.concatenate([rope(qn[:, 0:128]), rope(qn[:, 128:256])], axis=1) * (HEAD ** -0.5 * LOG2_E)
    qt = qr.T.astype(BF16)
    zq = jnp.zeros((HEAD, 2 * tq), BF16)
    for r in range(tm // tq):
        cs = slice(r * tq, (r + 1) * tq)
        top0 = jnp.concatenate([qt[0:64, cs], qt[64:128, cs]], axis=1)
        top1 = jnp.concatenate([qt[128:192, cs], qt[192:256, cs]], axis=1)
        aqt_ref[0, 0, r] = jnp.concatenate([top0, zq], axis=0)
        aqt_ref[0, 1, r] = jnp.concatenate([zq, top1], axis=0)
    kn = ck * lax.rsqrt(_head_sum(ck * ck) * (1.0 / HEAD) + EPS) * kw_ref[...]
    ak_ref[0] = rope(kn).astype(BF16)
    vt = pm[:, 1408:1536].T
    ones = jnp.ones((16, tm), F32)
    for g in range(2):
        vg = jnp.concatenate([vt[g * HEAD:(g + 1) * HEAD], ones], axis=0).astype(BF16)
        for r in range(tm // ATTN_TK):
            avt_ref[0, g, r] = vg[:, r * ATTN_TK:(r + 1) * ATTN_TK]


def _proj_call(x, lw, rope_c, rope_s, tm, tq):
    b, s, _ = x.shape
    n_tiles = s // tm
    hb = tm // HALO
    last_hb = s // HALO - 1
    full = lambda shape: pl.BlockSpec(shape, lambda bi, i: (0,) * len(shape))
    tok = lambda w: pl.BlockSpec((1, tm, w), lambda bi, i: (bi, i, 0))
    in_specs = [
        pl.BlockSpec((1, HALO, D_MODEL), lambda bi, i: (bi, jnp.maximum(i * hb - 1, 0), 0)),
        pl.BlockSpec((1, tm, D_MODEL), lambda bi, i: (bi, i, 0)),
        pl.BlockSpec((1, HALO, D_MODEL), lambda bi, i: (bi, jnp.minimum((i + 1) * hb, last_hb), 0)),
        full((1, D_MODEL)), full((D_MODEL, 1024)), full((D_MODEL, MAIN_COLS)),
        full((128, 512)), full((128, 256)), full((8, 768)), full((8, 128)),
        full((1, 256)), full((1, 128)),
        pl.BlockSpec((tm, 128), lambda bi, i: (i, 0)), pl.BlockSpec((tm, 128), lambda bi, i: (i, 0)),
        full((256, 256)), full((1, 256)),
    ]
    out_shape = [
        jax.ShapeDtypeStruct((b, s, 256), BF16),
        jax.ShapeDtypeStruct((b, s, 256), BF16),
        jax.ShapeDtypeStruct((b, s, 256), BF16),
        jax.ShapeDtypeStruct((b, s, 256), BF16),
        jax.ShapeDtypeStruct((b, s, 256), BF16),
        jax.ShapeDtypeStruct((b, s, 128), F32),
        jax.ShapeDtypeStruct((b, s, 512), BF16),
        jax.ShapeDtypeStruct((b, 2, s // tq, 128, 2 * tq), BF16),
        jax.ShapeDtypeStruct((b, s, 128), BF16),
        jax.ShapeDtypeStruct((b, 2, s // ATTN_TK, HEAD + 16, ATTN_TK), BF16),
    ]
    out_specs = [tok(256), tok(256), tok(256), tok(256), tok(256), tok(128), tok(512),
                 pl.BlockSpec((1, 2, tm // tq, 128, 2 * tq), lambda bi, i: (bi, 0, i, 0, 0)), tok(128),
                 pl.BlockSpec((1, 2, tm // ATTN_TK, HEAD + 16, ATTN_TK), lambda bi, i: (bi, 0, i, 0, 0))]
    return pl.pallas_call(
        functools.partial(_proj_kernel, tm=tm, tq=tq, n_tiles=n_tiles, seq_len=s),
        grid=(b, n_tiles), in_specs=in_specs, out_specs=out_specs, out_shape=out_shape,
        scratch_shapes=[pltpu.VMEM((tm + 2 * HALO, D_MODEL), BF16)],
        compiler_params=_cparams(("parallel", "parallel")), name="proj",
    )(x, x, x, lw["norm_w"], lw["w_halo"], lw["w_main"], lw["sgu_w"], lw["sgu_b"], lw["conv_w"],
      lw["dn_row"], lw["q_norm_w"], lw["k_norm_w"], rope_c, rope_s, lw["pool_w"], lw["pool_scale"])


def _dn_kernel(qf_ref, kf_ref, vf_ref, bgf_ref, qb_ref, kb_ref, vb_ref, bgb_ref,
               of_ref, ob_ref, state_ref, sums_ref, expand_ref, ones_ref, bd_ref, *, nb, tb):
    nc = tb // DN_CHUNK

    @pl.when(pl.program_id(0) == 0)
    def _():
        state_ref[...] = jnp.zeros_like(state_ref)
        r = _iota((tb, tb), 0)
        c = _iota((tb, tb), 1)
        same = jnp.right_shift(r, 6) == jnp.right_shift(c, 6)
        src = _iota((128, 768), 0)
        col = _iota((128, 768), 1)
        for d in range(2):
            m_in = _ones_where(same & ((c <= r) if d == 0 else (c >= r)))
            m_st = _ones_where(same & ((c > r) if d == 0 else (c < r)))
            m2 = jnp.concatenate([m_in, m_st], axis=0)
            sums_ref[d] = jnp.concatenate([m2, m2], axis=1)
            expand_ref[d] = _ones_where((src & 63) == jnp.left_shift(jnp.right_shift(col, 8), 3) + 4 * d
                                        + (jnp.right_shift(col, 6) & 3))
        ones_ref[...] = jnp.concatenate([_ones_where(same), _ones_where(same)], axis=1)
        bd_ref[...] = _ones_where(_bd_mask())

    li = _iota((tb, 256), 0) & 63
    lj = _iota((tb, 256), 1) & 63
    eyecat = li == lj
    eyef = jnp.where(eyecat, 1.0, 0.0)
    j2 = ones_ref[...]
    bdm = _bd_mask()
    bdm01 = bd_ref[...]
    lane = _iota((tb, 128), 1)
    zpad = jnp.zeros((DN_CHUNK, 256), BF16)

    dir_consts = []
    for d in range(2):
        dir_consts.append(dict(m2=sums_ref[d], expand=expand_ref[d],
                               incl=(li >= lj) if d == 0 else (li <= lj),
                               strict=(li > lj) if d == 0 else (li < lj)))
    blocks = []
    for d in range(2):
        q_ref, k_ref, v_ref, bg_ref = ((qf_ref, kf_ref, vf_ref, bgf_ref), (qb_ref, kb_ref, vb_ref, bgb_ref))[d]
        for bi in range(nb):
            blocks.append(dict(d=d, bi=bi, q_ref=q_ref, k_ref=k_ref, v_ref=v_ref, bg=bg_ref[bi], **dir_consts[d]))
    for blk in blocks:
        blk["cs"] = jnp.dot(blk["m2"], jnp.concatenate(_split2(blk["bg"]), axis=0),
                            preferred_element_type=F32)
    for blk in blocks:
        cs = blk["cs"]
        nbx = jnp.where(lane < 8, blk["bg"], jnp.where(lane < 16, cs[0:tb], pltpu.roll(cs[tb:2 * tb], 8, 1)))
        hi = nbx.astype(BF16)
        lo = pltpu.roll(nbx - hi.astype(F32), 64, 1)
        packed = jnp.where(lane < 64, hi.astype(F32), lo).astype(BF16)
        blk["x"] = jnp.dot(packed, blk["expand"], preferred_element_type=F32)
    for blk in blocks:
        gcx = blk["x"][:, 256:512]
        blk["rowf"] = jnp.dot(j2, jnp.concatenate(_split2(jnp.where(eyecat, gcx, 0.0)), axis=0),
                              preferred_element_type=F32)
    for blk in blocks:
        gram, qk = [], []
        for ci in range(nc):
            sl = slice(ci * DN_CHUNK, (ci + 1) * DN_CHUNK)
            kc = blk["k_ref"][blk["bi"], sl, :]
            gq = lax.dot_general(jnp.concatenate([kc, blk["q_ref"][blk["bi"], sl, :]], axis=0), _bd(kc, bdm01),
                                 (((1,), (1,)), ((), ())), preferred_element_type=F32)
            gram.append(gq[0:64])
            qk.append(gq[64:128])
        blk["gram"] = jnp.concatenate(gram, axis=0)
        blk["qk"] = jnp.concatenate(qk, axis=0)
    chains = {}
    for blk in blocks:
        d, bi = blk["d"], blk["bi"]
        q = blk["q_ref"][bi].astype(F32)
        k = blk["k_ref"][bi].astype(F32)
        v = blk["v_ref"][bi].astype(F32)
        x = blk["x"]
        bx, gcx, dglx = x[:, 0:256], x[:, 256:512], x[:, 512:768]
        decay = jnp.exp(jnp.where(blk["incl"], gcx - blk["rowf"], NEG_BIG))
        eg = jnp.exp(gcx)
        a_all = jnp.where(blk["strict"], bx * blk["gram"] * decay, 0.0)
        p_all = eyef - a_all
        intra = jnp.where(blk["incl"], blk["qk"] * decay, 0.0).astype(BF16)
        qd = (q * eg).astype(BF16)
        kdt = (k * jnp.exp(dglx)).T.astype(BF16)
        vbeta = (v * bx).astype(BF16)
        kbg = (k * bx * eg).astype(BF16)
        for ci in range(nc):
            sl = slice(ci * DN_CHUNK, (ci + 1) * DN_CHUNK)
            last = ci * DN_CHUNK + (DN_CHUNK - 1 if d == 0 else 0)
            pair = ci // 2
            chains[(d, bi, ci)] = dict(
                ak=a_all[sl], p=p_all[sl], vb=vbeta[sl], kbg=kbg[sl], intra=intra[sl], qd=qd[sl],
                kpair=kdt[:, pair * 128:(pair + 1) * 128], egl=eg[last:last + 1, :])

    def prepare(group, phase):
        for ch in group:
            if phase < 5:
                akb = ch["ak"].astype(BF16)
                lhs = akb if phase == 0 else jnp.concatenate([akb, ch["p"].astype(BF16)], axis=0)
                res = jnp.dot(lhs, _bd(akb, bdm01), preferred_element_type=F32)
                ch["ak"] = res[0:64]
                if phase > 0:
                    ch["p"] = ch["p"] + res[64:128]
            elif phase == 5:
                ch["p"] = ch["p"] + jnp.dot(ch["p"].astype(BF16), _bd(ch["ak"].astype(BF16), bdm01),
                                            preferred_element_type=F32)
            else:
                rhs = jnp.concatenate([_bd(ch["vb"], bdm01), _bd(ch["kbg"], bdm01)], axis=1)
                uw = jnp.dot(ch["p"].astype(BF16), rhs, preferred_element_type=F32)
                ch["u"] = uw[:, 0:256]
                ch["w"] = uw[:, 256:512].astype(BF16)

    def recur_a(group):
        for ch in group:
            ch["state"] = state_ref[ch["si"]]
            res = jnp.dot(jnp.concatenate([ch["w"], ch["qd"]], axis=0), ch["state"].astype(BF16),
                          preferred_element_type=F32)
            ch["v_new"] = (ch["u"] - res[0:64]).astype(BF16)
            ch["o_inter"] = res[64:128]

    def recur_b(group):
        for ch in group:
            v_new = ch["v_new"]
            ch["o_ref"][ch["bi"], ch["sl"], :] = (ch["o_inter"] + jnp.dot(
                ch["intra"], _bd(v_new, bdm01), preferred_element_type=F32)).astype(BF16)
            vpad = jnp.concatenate([v_new, zpad] if ch["even"] else [zpad, v_new], axis=0)
            ds = jnp.dot(ch["kpair"], vpad, preferred_element_type=F32)
            state_ref[ch["si"]] = ch["state"] * ch["egl"] + jnp.where(bdm, ds, 0.0)

    groups = []
    for step in range(nc):
        group = []
        for bi in range(nb):
            for d in range(2):
                ci = step if d == 0 else nc - 1 - step
                ch = chains[(d, bi, ci)]
                ch.update(si=d * nb + bi, bi=bi, o_ref=(of_ref, ob_ref)[d], even=ci % 2 == 0,
                          sl=slice(ci * DN_CHUNK, (ci + 1) * DN_CHUNK))
                group.append(ch)
        groups.append(group)

    for phase in range(7):
        prepare(groups[0], phase)
    for step in range(nc):
        nxt = groups[step + 1] if step + 1 < nc else []
        prepare(nxt, 0)
        prepare(nxt, 1)
        recur_a(groups[step])
        prepare(nxt, 2)
        prepare(nxt, 3)
        recur_b(groups[step])
        prepare(nxt, 4)
        prepare(nxt, 5)
        prepare(nxt, 6)


def _dn_call(dq, dk, dv, bg, tb):
    b, s, _ = dq.shape
    n = s // tb
    fwd = lambda w: pl.BlockSpec((b, tb, w), lambda i: (0, i, 0))
    bwd = lambda w: pl.BlockSpec((b, tb, w), lambda i: (0, n - 1 - i, 0))
    return pl.pallas_call(
        functools.partial(_dn_kernel, nb=b, tb=tb),
        grid=(n,),
        in_specs=[fwd(256), fwd(256), fwd(256), fwd(128), bwd(256), bwd(256), bwd(256), bwd(128)],
        out_specs=[fwd(256), bwd(256)],
        out_shape=[jax.ShapeDtypeStruct((b, s, 256), BF16)] * 2,
        scratch_shapes=[pltpu.VMEM((2 * b, 256, 256), F32), pltpu.VMEM((2, 2 * tb, 2 * tb), BF16),
                        pltpu.VMEM((2, 128, 768), BF16), pltpu.VMEM((tb, 2 * tb), BF16),
                        pltpu.VMEM((256, 256), BF16)],
        compiler_params=_cparams(("arbitrary",)), name="dn",
    )(dq, dk, dv, bg, dq, dk, dv, bg)


def _attn_kernel(qt_ref, k_ref, vt_ref, o_ref, st_ref, *, tq, tk, n_q, n_kv, unroll):
    per_q = n_kv // unroll
    n_trips = n_q * per_q

    def scores(qi, j, slot):
        kj = k_ref[0, pl.ds(pl.multiple_of(j * tk, tk), tk), :]
        st = jnp.dot(kj, qt_ref[0, 0, qi], preferred_element_type=F32)
        st_ref[slot, :, 0:2 * tq] = st
        return jnp.max(st, axis=0, keepdims=True)

    def update(j, st, mx, m, acc):
        m_new = jnp.maximum(m, mx)
        p = jnp.exp2(st - m_new).astype(BF16)
        alpha = jnp.exp2(m - m_new)
        return m_new, alpha * acc + jnp.dot(vt_ref[0, 0, j], p, preferred_element_type=F32)

    def body(t, carry):
        qi = t // per_q
        base = (t - qi * per_q) * unroll
        first = base == 0
        m = jnp.where(first, -jnp.inf, carry[0])
        acc = jnp.where(first, 0.0, carry[1])
        mx = carry[2]
        t_next = jnp.minimum(t + 1, n_trips - 1)
        qi_next = t_next // per_q
        base_next = (t_next - qi_next * per_q) * unroll
        for r in range(unroll):
            if r < unroll - 1:
                mx_next = scores(qi, base + r + 1, (r + 1) % 2)
            else:
                mx_next = scores(qi_next, base_next, 0)
            m, acc = update(base + r, st_ref[r % 2, :, 0:2 * tq], mx, m, acc)
            mx = mx_next

        @pl.when(base == n_kv - unroll)
        def _():
            o = acc[0:HEAD] / acc[HEAD:HEAD + 1]
            ot = jnp.concatenate([o[:, 0:tq], o[:, tq:2 * tq]], axis=0)
            o_ref[0, pl.ds(pl.multiple_of(qi * tq, tq), tq), :] = ot.T.astype(BF16)

        return m, acc, mx

    m0 = jnp.full((1, 2 * tq), -jnp.inf, F32)
    a0 = jnp.zeros((HEAD + 16, 2 * tq), F32)
    lax.fori_loop(0, n_trips, body, (m0, a0, scores(0, 0, 0)))


def _attn_call(aqt, ak, avt):
    b, s, _ = ak.shape
    _, _, n_q, _, tq2 = aqt.shape
    _, _, n_kv, vrows, tk = avt.shape
    unroll = next(u for u in (32, 16, 8, 4, 2) if n_kv % u == 0)
    return pl.pallas_call(
        functools.partial(_attn_kernel, tq=tq2 // 2, tk=tk, n_q=n_q, n_kv=n_kv, unroll=unroll),
        grid=(b, 2),
        in_specs=[pl.BlockSpec((1, 1, n_q, 128, tq2), lambda bi, g: (bi, g, 0, 0, 0)),
                  pl.BlockSpec((1, s, 128), lambda bi, g: (bi, 0, 0)),
                  pl.BlockSpec((1, 1, n_kv, vrows, tk), lambda bi, g: (bi, g, 0, 0, 0))],
        out_specs=pl.BlockSpec((1, s, 128), lambda bi, g: (bi, 0, g)),
        out_shape=jax.ShapeDtypeStruct((b, s, 256), BF16),
        scratch_shapes=[pltpu.VMEM((2, tk, tq2 + 128), F32)],
        compiler_params=_cparams(("parallel", "parallel")), name="attn",
    )(aqt, ak, avt)


def _out_kernel(x_ref, ya_ref, yd_ref, yc_ref, of_ref, ob_ref, gates_ref, dnw_ref, wo_ref, o_ref):
    o = of_ref[0].astype(F32) + ob_ref[0].astype(F32)
    gates = gates_ref[0].astype(F32)
    on = o * lax.rsqrt(_head_sum(o * o) * (1.0 / HEAD) + EPS) * dnw_ref[...]
    yb = (on * gates[:, 0:256]).astype(BF16)
    yc = (yc_ref[0].astype(F32) * gates[:, 256:512]).astype(BF16)
    mix = jnp.concatenate([ya_ref[0], yb, yc, yd_ref[0]], axis=1)
    o_ref[0] = x_ref[0] + jnp.dot(mix, wo_ref[...], preferred_element_type=F32)


def _out_call(x, ya, yd, yc, o_f, o_b, gates, lw, tm):
    b, s, _ = x.shape
    tok = lambda w: pl.BlockSpec((1, tm, w), lambda bi, i: (bi, i, 0))
    full = lambda shape: pl.BlockSpec(shape, lambda bi, i: (0,) * len(shape))
    return pl.pallas_call(
        _out_kernel, grid=(b, s // tm),
        in_specs=[tok(D_MODEL), tok(256), tok(256), tok(256), tok(256), tok(256), tok(512),
                  full((1, 256)), full((D_MODEL, D_MODEL))],
        out_specs=tok(D_MODEL), out_shape=jax.ShapeDtypeStruct((b, s, D_MODEL), F32),
        compiler_params=_cparams(("parallel", "parallel")), name="out",
    )(x, ya, yd, yc, o_f, o_b, gates, lw["dn_norm_w"], lw["w_out"])


def _rope_tables(seq_len):
    t = jnp.arange(seq_len)
    pos = jnp.stack([t // GRID_W, t % GRID_W], axis=-1).astype(F32)
    n_freq = HEAD // 4
    inv_freq = jnp.power(ROPE_THETA, -2.0 * jnp.arange(n_freq, dtype=F32) / (HEAD // 2))
    ang = pos[:, :, None] * inv_freq
    cos = jnp.repeat(jnp.cos(ang)[:, :, None, :], 2, axis=2).reshape(seq_len, HEAD)
    sin = jnp.sin(ang)
    sin = jnp.stack([-sin, sin], axis=2).reshape(seq_len, HEAD)
    return jnp.tile(cos, (1, 2)), jnp.tile(sin, (1, 2))


def _layer_weights(l, norm_w, w_in, sgu_w, sgu_b, conv_w, a_log, dt_bias, dn_norm_w, q_norm_w,
                   k_norm_w, pool_w, pool_scale, w_out):
    w = w_in[l]
    cols = lambda a, n: w[:, a:a + n]
    w_halo = jnp.concatenate([cols(_B_Q, 768), cols(_D_X, 256)], axis=1).astype(BF16)
    w_main = jnp.concatenate([
        cols(_A_U, 768), cols(_B_Z, 256), cols(_C_Q, 768), cols(_D_Z, 256), cols(_B_BETA, 16),
        jnp.zeros((D_MODEL, 112), F32)], axis=1).astype(BF16)
    dn_row = jnp.zeros((8, 128), F32)
    dn_row = dn_row.at[0, 8:16].set(a_log[l].reshape(8)).at[1, 8:16].set(dt_bias[l].reshape(8))
    pool_bd = jnp.zeros((256, 256), F32)
    for gi in range(len(POOL_WINDOWS)):
        pool_bd = pool_bd.at[gi * 64:(gi + 1) * 64, gi * 64:(gi + 1) * 64].set(pool_w[l, gi])
    return {
        "norm_w": norm_w[l].reshape(1, D_MODEL),
        "w_halo": w_halo, "w_main": w_main,
        "sgu_w": jnp.transpose(sgu_w[l], (1, 0, 2)).reshape(SGU_CHUNK, 4 * SGU_CHUNK).astype(BF16),
        "sgu_b": jnp.repeat(sgu_b[l].T, HEAD, axis=1),
        "conv_w": jnp.concatenate([conv_w[l], jnp.zeros((3, 768), F32)], axis=0),
        "dn_row": dn_row,
        "q_norm_w": jnp.tile(q_norm_w[l], 4).reshape(1, 256),
        "k_norm_w": jnp.tile(k_norm_w[l], 2).reshape(1, 128),
        "pool_w": pool_bd.astype(BF16),
        "pool_scale": pool_scale[l].reshape(1, 256),
        "dn_norm_w": jnp.tile(dn_norm_w[l], 4).reshape(1, 256),
        "w_out": w_out[l].astype(BF16),
    }


def _tiles(batch, seq_len):
    tb = min(DN_CHUNK * max(1, 8 // batch), seq_len)
    return dict(tm=min(512, seq_len), tb=tb, tq=min(256, seq_len))


def _layer(x, lw, rope_c, rope_s):
    t = _tiles(x.shape[0], x.shape[1])
    ya, yd, dq, dk, dv, bg, gates, aqt, ak, avt = _proj_call(x, lw, rope_c, rope_s, t["tm"], t["tq"])
    o_f, o_b = _dn_call(dq, dk, dv, bg, t["tb"])
    yc = _attn_call(aqt, ak, avt)
    return _out_call(x, ya, yd, yc, o_f, o_b, gates, lw, t["tm"])


def kernel(x_prompt, x_sample, norm_w, w_in, sgu_w, sgu_b, conv_w, a_log, dt_bias, dn_norm_w,
           q_norm_w, k_norm_w, pool_w, pool_scale, w_out):
    depth = norm_w.shape[0]
    rope_p = _rope_tables(x_prompt.shape[1])
    rope_s = _rope_tables(x_sample.shape[1])
    y_prompt, y_sample = x_prompt, x_sample
    for l in range(depth):
        lw = _layer_weights(l, norm_w, w_in, sgu_w, sgu_b, conv_w, a_log, dt_bias, dn_norm_w,
                            q_norm_w, k_norm_w, pool_w, pool_scale, w_out)
        y_prompt = _layer(y_prompt, lw, *rope_p)
        y_sample = _layer(y_sample, lw, *rope_s)
    return (y_prompt, y_sample)
```

```python
import functools

import jax
import jax.numpy as jnp
from jax import lax
from jax.experimental import pallas as pl
from jax.experimental.pallas import tpu as pltpu

F32 = jnp.float32
BF16 = jnp.bfloat16

D_MODEL = 1024
HEAD = 64
N_HEADS = 4
GRID_W = 64
EPS = 1e-6
SGU_CHUNK = 128
DN_CHUNK = 64
ROPE_THETA = 10000.0
POOL_WINDOWS = (2, 4, 8, 16)
HALO = 16
NEG_BIG = -1e30
LOG2_E = 1.4426950408889634
ATTN_TK = 512

V7X_VMEM_LIMIT_BYTES = 56 * 1024 * 1024

_A_U = 0
_B_Q, _B_Z, _B_BETA = 768, 1536, 1792
_C_Q = 1808
_D_X, _D_Z = 2576, 2832
MAIN_COLS = 2176


def _cparams(semantics):
    return pltpu.CompilerParams(dimension_semantics=semantics,
                                vmem_limit_bytes=V7X_VMEM_LIMIT_BYTES)


def _split2(x):
    hi = x.astype(BF16)
    lo = (x - hi.astype(F32)).astype(BF16)
    return hi, lo


def _iota(shape, dim):
    return lax.broadcasted_iota(jnp.int32, shape, dim)


def _ones_where(cond):
    return jnp.where(cond, 1.0, 0.0).astype(BF16)


def _head_sum(x2):
    w = x2.shape[1]
    g = _ones_where(jnp.right_shift(_iota((w, w), 0), 6) == jnp.right_shift(_iota((w, w), 1), 6))
    return jnp.dot(x2.astype(BF16), g, preferred_element_type=F32)


def _silu(z):
    return 0.5 * z * (1.0 + jnp.tanh(0.5 * z))


def _bd_mask():
    return jnp.right_shift(_iota((256, 256), 0), 6) == jnp.right_shift(_iota((256, 256), 1), 6)


def _bd(x, mask01):
    z = jnp.zeros((HEAD, 128), x.dtype)
    blocks = []
    for h in range(N_HEADS):
        t = h // 2
        m = x[:, t * 128:(t + 1) * 128] * mask01[h * HEAD:(h + 1) * HEAD, t * 128:(t + 1) * 128]
        blocks.append(jnp.concatenate([m, z] if t == 0 else [z, m], axis=1))
    return jnp.concatenate(blocks, axis=0)


def _proj_kernel(xp_ref, xc_ref, xn_ref, nw_ref, wh_ref, wm_ref, sguw_ref, sgub_ref, convw_ref,
                 dnrow_ref, qw_ref, kw_ref, rc_ref, rs_ref, poolw_ref, pools_ref,
                 ya_ref, yd_ref, dq_ref, dk_ref, dv_ref, bg_ref, gates_ref, aqt_ref, ak_ref, avt_ref,
                 hext_ref, *, tm, tq, n_tiles, seq_len):
    i = pl.program_id(1)
    n_ext = tm + 2 * HALO
    nw = nw_ref[...]

    def norm(x):
        ms = jnp.mean(x * x, axis=-1, keepdims=True)
        return x * lax.rsqrt(ms + EPS) * nw

    hext_ref[0:HALO, :] = jnp.where(i > 0, norm(xp_ref[0]), 0.0).astype(BF16)
    hext_ref[HALO:HALO + tm, :] = norm(xc_ref[0]).astype(BF16)
    hext_ref[HALO + tm:n_ext, :] = jnp.where(i < n_tiles - 1, norm(xn_ref[0]), 0.0).astype(BF16)

    ph = jnp.dot(hext_ref[...], wh_ref[...], preferred_element_type=F32)
    pm = jnp.dot(hext_ref[HALO:HALO + tm, :], wm_ref[...], preferred_element_type=F32)

    def rows(x):
        return x[HALO:HALO + tm]

    a_u, a_v, a_z = pm[:, 0:256], pm[:, 256:512], pm[:, 512:768]
    vn = a_v * lax.rsqrt(_head_sum(a_v * a_v) * (1.0 / HEAD) + EPS)
    mask4 = _ones_where(jnp.right_shift(_iota((512, 256), 0), 7) == jnp.right_shift(_iota((512, 256), 1), 6))
    mixed = []
    for c in range(tm // SGU_CHUNK):
        vc = vn[c * SGU_CHUNK:(c + 1) * SGU_CHUNK].astype(BF16)
        bdv = jnp.concatenate([vc, vc, vc, vc], axis=0) * mask4
        mixed.append(jnp.dot(sguw_ref[...], bdv, preferred_element_type=F32) + sgub_ref[...])
    mixed = jnp.concatenate(mixed, axis=0)
    ya_ref[0] = (a_u * mixed * _silu(a_z)).astype(BF16)

    xd = ph[:, 768:1024]
    a1 = xd + pltpu.roll(xd, n_ext - 1, 0)
    a2 = a1 + pltpu.roll(a1, n_ext - 2, 0)
    a3 = a2 + pltpu.roll(a2, n_ext - 4, 0)
    a4 = a3 + pltpu.roll(a3, n_ext - 8, 0)
    w2 = rows(pltpu.roll(a1, 1, 0))
    w4 = rows(pltpu.roll(a2, 2, 0))
    w8 = rows(pltpu.roll(a3, 4, 0))
    w16 = rows(pltpu.roll(a4, 8, 0))
    grp = jnp.right_shift(_iota((tm, 256), 1), 6)
    half = jnp.left_shift(jnp.ones((tm, 256), jnp.int32), grp)
    t = i * tm + _iota((tm, 256), 0)
    cnt = (jnp.minimum(t + half, seq_len) - jnp.maximum(t - half, 0)).astype(F32)
    win = jnp.where(grp == 0, w2, jnp.where(grp == 1, w4, jnp.where(grp == 2, w8, w16)))
    diff = win / cnt - rows(xd)
    yd = jnp.dot(diff.astype(BF16), poolw_ref[...], preferred_element_type=F32) * pools_ref[...]
    yd_ref[0] = (yd * _silu(pm[:, 1792:2048])).astype(BF16)

    xb = ph[:, 0:768]
    cw = convw_ref[...]
    conv = (rows(pltpu.roll(xb, 2, 0)) * cw[0:1] + rows(pltpu.roll(xb, 1, 0)) * cw[1:2]
            + rows(xb) * cw[2:3] + rows(pltpu.roll(xb, n_ext - 1, 0)) * cw[3:4]
            + rows(pltpu.roll(xb, n_ext - 2, 0)) * cw[4:5])
    act = _silu(conv)
    bq, bk = act[:, 0:256], act[:, 256:512]
    dq_ref[0] = (bq * lax.rsqrt(_head_sum(bq * bq) + EPS) * (HEAD ** -0.5)).astype(BF16)
    dk_ref[0] = (bk * lax.rsqrt(_head_sum(bk * bk) + EPS)).astype(BF16)
    dv_ref[0] = act[:, 512:768].astype(BF16)
    ba = pm[:, 2048:2176]
    lane = _iota((tm, 128), 1)
    xa = ba + dnrow_ref[1:2, :]
    softplus = jnp.maximum(xa, 0.0) + jnp.log1p(jnp.exp(-jnp.abs(xa)))
    g = -jnp.exp(dnrow_ref[0:1, :]) * softplus
    bg_ref[0] = jnp.where(lane < 8, 1.0 / (1.0 + jnp.exp(-ba)), g)
    gates_ref[0] = jnp.concatenate([_silu(pm[:, 768:1024]), _silu(pm[:, 1536:1792])], axis=1).astype(BF16)

    rc, rs = rc_ref[...], rs_ref[...]
    first = (_iota((tm, 128), 1) & 16) == 0

    def rope(x):
        sw = jnp.where(first, pltpu.roll(x, 112, 1), pltpu.roll(x, 16, 1))
        return x * rc + sw * rs

    cq, ck = pm[:, 1024:1280], pm[:, 1280:1408]
    qn = cq * lax.rsqrt(_head_sum(cq * cq) * (1.0 / HEAD) + EPS) * qw_ref[...]
    qr = jnp.concatenate([rope(qn[:, 0:128]), rope(qn[:, 128:256])], axis=1) * (HEAD ** -0.5 * LOG2_E)
    qt = qr.T.astype(BF16)
    zq = jnp.zeros((HEAD, 2 * tq), BF16)
    for r in range(tm // tq):
        cs = slice(r * tq, (r + 1) * tq)
        top0 = jnp.concatenate([qt[0:64, cs], qt[64:128, cs]], axis=1)
        top1 = jnp.concatenate([qt[128:192, cs], qt[192:256, cs]], axis=1)
        aqt_ref[0, 0, r] = jnp.concatenate([top0, zq], axis=0)
        aqt_ref[0, 1, r] = jnp.concatenate([zq, top1], axis=0)
    kn = ck * lax.rsqrt(_head_sum(ck * ck) * (1.0 / HEAD) + EPS) * kw_ref[...]
    ak_ref[0] = rope(kn).astype(BF16)
    vt = pm[:, 1408:1536].T
    ones = jnp.ones((16, tm), F32)
    for g in range(2):
        vg = jnp.concatenate([vt[g * HEAD:(g + 1) * HEAD], ones], axis=0).astype(BF16)
        for r in range(tm // ATTN_TK):
            avt_ref[0, g, r] = vg[:, r * ATTN_TK:(r + 1) * ATTN_TK]


def _proj_call(x, lw, rope_c, rope_s, tm, tq):
    b, s, _ = x.shape
    n_tiles = s // tm
    hb = tm // HALO
    last_hb = s // HALO - 1
    full = lambda shape: pl.BlockSpec(shape, lambda bi, i: (0,) * len(shape))
    tok = lambda w: pl.BlockSpec((1, tm, w), lambda bi, i: (bi, i, 0))
    in_specs = [
        pl.BlockSpec((1, HALO, D_MODEL), lambda bi, i: (bi, jnp.maximum(i * hb - 1, 0), 0)),
        pl.BlockSpec((1, tm, D_MODEL), lambda bi, i: (bi, i, 0)),
        pl.BlockSpec((1, HALO, D_MODEL), lambda bi, i: (bi, jnp.minimum((i + 1) * hb, last_hb), 0)),
        full((1, D_MODEL)), full((D_MODEL, 1024)), full((D_MODEL, MAIN_COLS)),
        full((128, 512)), full((128, 256)), full((8, 768)), full((8, 128)),
        full((1, 256)), full((1, 128)),
        pl.BlockSpec((tm, 128), lambda bi, i: (i, 0)), pl.BlockSpec((tm, 128), lambda bi, i: (i, 0)),
        full((256, 256)), full((1, 256)),
    ]
    out_shape = [
        jax.ShapeDtypeStruct((b, s, 256), BF16),
        jax.ShapeDtypeStruct((b, s, 256), BF16),
        jax.ShapeDtypeStruct((b, s, 256), BF16),
        jax.ShapeDtypeStruct((b, s, 256), BF16),
        jax.ShapeDtypeStruct((b, s, 256), BF16),
        jax.ShapeDtypeStruct((b, s, 128), F32),
        jax.ShapeDtypeStruct((b, s, 512), BF16),
        jax.ShapeDtypeStruct((b, 2, s // tq, 128, 2 * tq), BF16),
        jax.ShapeDtypeStruct((b, s, 128), BF16),
        jax.ShapeDtypeStruct((b, 2, s // ATTN_TK, HEAD + 16, ATTN_TK), BF16),
    ]
    out_specs = [tok(256), tok(256), tok(256), tok(256), tok(256), tok(128), tok(512),
                 pl.BlockSpec((1, 2, tm // tq, 128, 2 * tq), lambda bi, i: (bi, 0, i, 0, 0)), tok(128),
                 pl.BlockSpec((1, 2, tm // ATTN_TK, HEAD + 16, ATTN_TK), lambda bi, i: (bi, 0, i, 0, 0))]
    return pl.pallas_call(
        functools.partial(_proj_kernel, tm=tm, tq=tq, n_tiles=n_tiles, seq_len=s),
        grid=(b, n_tiles), in_specs=in_specs, out_specs=out_specs, out_shape=out_shape,
        scratch_shapes=[pltpu.VMEM((tm + 2 * HALO, D_MODEL), BF16)],
        compiler_params=_cparams(("parallel", "parallel")), name="proj",
    )(x, x, x, lw["norm_w"], lw["w_halo"], lw["w_main"], lw["sgu_w"], lw["sgu_b"], lw["conv_w"],
      lw["dn_row"], lw["q_norm_w"], lw["k_norm_w"], rope_c, rope_s, lw["pool_w"], lw["pool_scale"])


def _dn_kernel(qf_ref, kf_ref, vf_ref, bgf_ref, qb_ref, kb_ref, vb_ref, bgb_ref,
               of_ref, ob_ref, state_ref, expand_ref, ones_ref, bd_ref, *, nb, tb):
    nc = tb // DN_CHUNK

    @pl.when(pl.program_id(0) == 0)
    def _():
        state_ref[...] = jnp.zeros_like(state_ref)
        r = _iota((tb, tb), 0)
        c = _iota((tb, tb), 1)
        same = jnp.right_shift(r, 6) == jnp.right_shift(c, 6)
        src = _iota((128, 768), 0)
        col = _iota((128, 768), 1)
        for d in range(2):
            expand_ref[d] = _ones_where((src & 63) == jnp.left_shift(jnp.right_shift(col, 8), 3) + 4 * d
                                        + (jnp.right_shift(col, 6) & 3))
        ones_ref[...] = jnp.concatenate([_ones_where(same), _ones_where(same)], axis=1)
        bd_ref[...] = _ones_where(_bd_mask())

    li = _iota((tb, 256), 0) & 63
    lj = _iota((tb, 256), 1) & 63
    eyecat = li == lj
    eyef = jnp.where(eyecat, 1.0, 0.0)
    j2 = ones_ref[...]
    bdm = _bd_mask()
    bdm01 = bd_ref[...]
    lane = _iota((tb, 128), 1)
    zpad = jnp.zeros((DN_CHUNK, 256), BF16)

    dir_consts = []
    for d in range(2):
        dir_consts.append(dict(expand=expand_ref[d],
                               incl=(li >= lj) if d == 0 else (li <= lj),
                               strict=(li > lj) if d == 0 else (li < lj)))
    blocks = []
    for d in range(2):
        q_ref, k_ref, v_ref, bg_ref = ((qf_ref, kf_ref, vf_ref, bgf_ref), (qb_ref, kb_ref, vb_ref, bgb_ref))[d]
        for bi in range(nb):
            blocks.append(dict(d=d, bi=bi, q_ref=q_ref, k_ref=k_ref, v_ref=v_ref, bg=bg_ref[bi], **dir_consts[d]))
    row64 = _iota((tb, 128), 0) & 63
    for blk in blocks:
        c = blk["bg"]
        for s in (1, 2, 4, 8, 16, 32):
            c = c + jnp.where(row64 >= s, pltpu.roll(c, s, 0), 0.0)
        tot = jnp.concatenate([jnp.broadcast_to(c[ci * DN_CHUNK + DN_CHUNK - 1:(ci + 1) * DN_CHUNK, :],
                                                (DN_CHUNK, 128)) for ci in range(nc)], axis=0)
        if blk["d"] == 0:
            blk["cs"] = (c, tot - c)
        else:
            blk["cs"] = (tot - c + blk["bg"], c - blk["bg"])
    for blk in blocks:
        cs_in, cs_st = blk["cs"]
        nbx = jnp.where(lane < 8, blk["bg"], jnp.where(lane < 16, cs_in, pltpu.roll(cs_st, 8, 1)))
        hi = nbx.astype(BF16)
        lo = pltpu.roll(nbx - hi.astype(F32), 64, 1)
        packed = jnp.where(lane < 64, hi.astype(F32), lo).astype(BF16)
        blk["x"] = jnp.dot(packed, blk["expand"], preferred_element_type=F32)
    for blk in blocks:
        gcx = blk["x"][:, 256:512]
        blk["rowf"] = jnp.dot(j2, jnp.concatenate(_split2(jnp.where(eyecat, gcx, 0.0)), axis=0),
                              preferred_element_type=F32)
    for blk in blocks:
        gram, qk = [], []
        for ci in range(nc):
            sl = slice(ci * DN_CHUNK, (ci + 1) * DN_CHUNK)
            kc = blk["k_ref"][blk["bi"], sl, :]
            gq = lax.dot_general(jnp.concatenate([kc, blk["q_ref"][blk["bi"], sl, :]], axis=0), _bd(kc, bdm01),
                                 (((1,), (1,)), ((), ())), preferred_element_type=F32)
            gram.append(gq[0:64])
            qk.append(gq[64:128])
        blk["gram"] = jnp.concatenate(gram, axis=0)
        blk["qk"] = jnp.concatenate(qk, axis=0)
    chains = {}
    for blk in blocks:
        d, bi = blk["d"], blk["bi"]
        q = blk["q_ref"][bi].astype(F32)
        k = blk["k_ref"][bi].astype(F32)
        v = blk["v_ref"][bi].astype(F32)
        x = blk["x"]
        bx, gcx, dglx = x[:, 0:256], x[:, 256:512], x[:, 512:768]
        decay = jnp.exp(jnp.where(blk["incl"], gcx - blk["rowf"], NEG_BIG))
        eg = jnp.exp(gcx)
        a_all = jnp.where(blk["strict"], bx * blk["gram"] * decay, 0.0)
        p_all = eyef - a_all
        intra = jnp.where(blk["incl"], blk["qk"] * decay, 0.0).astype(BF16)
        qd = (q * eg).astype(BF16)
        kdt = (k * jnp.exp(dglx)).T.astype(BF16)
        vbeta = (v * bx).astype(BF16)
        kbg = (k * bx * eg).astype(BF16)
        for ci in range(nc):
            sl = slice(ci * DN_CHUNK, (ci + 1) * DN_CHUNK)
            last = ci * DN_CHUNK + (DN_CHUNK - 1 if d == 0 else 0)
            pair = ci // 2
            chains[(d, bi, ci)] = dict(
                ak=a_all[sl], p=p_all[sl], vb=vbeta[sl], kbg=kbg[sl], intra=intra[sl], qd=qd[sl],
                kpair=kdt[:, pair * 128:(pair + 1) * 128], egl=eg[last:last + 1, :])

    def prepare(group, phase):
        for ch in group:
            if phase < 5:
                akb = ch["ak"].astype(BF16)
                lhs = akb if phase == 0 else jnp.concatenate([akb, ch["p"].astype(BF16)], axis=0)
                res = jnp.dot(lhs, _bd(akb, bdm01), preferred_element_type=F32)
                ch["ak"] = res[0:64]
                if phase > 0:
                    ch["p"] = ch["p"] + res[64:128]
            elif phase == 5:
                ch["p"] = ch["p"] + jnp.dot(ch["p"].astype(BF16), _bd(ch["ak"].astype(BF16), bdm01),
                                            preferred_element_type=F32)
            else:
                rhs = jnp.concatenate([_bd(ch["vb"], bdm01), _bd(ch["kbg"], bdm01)], axis=1)
                uw = jnp.dot(ch["p"].astype(BF16), rhs, preferred_element_type=F32)
                ch["u"] = uw[:, 0:256]
                ch["w"] = uw[:, 256:512].astype(BF16)

    def recur_a(group):
        for ch in group:
            ch["state"] = state_ref[ch["si"]]
            res = jnp.dot(jnp.concatenate([ch["w"], ch["qd"]], axis=0), ch["state"].astype(BF16),
                          preferred_element_type=F32)
            ch["v_new"] = (ch["u"] - res[0:64]).astype(BF16)
            ch["o_inter"] = res[64:128]

    def recur_b(group):
        for ch in group:
            v_new = ch["v_new"]
            ch["o_ref"][ch["bi"], ch["sl"], :] = (ch["o_inter"] + jnp.dot(
                ch["intra"], _bd(v_new, bdm01), preferred_element_type=F32)).astype(BF16)
            vpad = jnp.concatenate([v_new, zpad] if ch["even"] else [zpad, v_new], axis=0)
            ds = jnp.dot(ch["kpair"], vpad, preferred_element_type=F32)
            state_ref[ch["si"]] = ch["state"] * ch["egl"] + jnp.where(bdm, ds, 0.0)

    groups = []
    for step in range(nc):
        group = []
        for bi in range(nb):
            for d in range(2):
                ci = step if d == 0 else nc - 1 - step
                ch = chains[(d, bi, ci)]
                ch.update(si=d * nb + bi, bi=bi, o_ref=(of_ref, ob_ref)[d], even=ci % 2 == 0,
                          sl=slice(ci * DN_CHUNK, (ci + 1) * DN_CHUNK))
                group.append(ch)
        groups.append(group)

    for phase in range(7):
        prepare(groups[0], phase)
    for step in range(nc):
        nxt = groups[step + 1] if step + 1 < nc else []
        prepare(nxt, 0)
        prepare(nxt, 1)
        recur_a(groups[step])
        prepare(nxt, 2)
        prepare(nxt, 3)
        recur_b(groups[step])
        prepare(nxt, 4)
        prepare(nxt, 5)
        prepare(nxt, 6)


def _dn_call(dq, dk, dv, bg, tb):
    b, s, _ = dq.shape
    n = s // tb
    fwd = lambda w: pl.BlockSpec((b, tb, w), lambda i: (0, i, 0))
    bwd = lambda w: pl.BlockSpec((b, tb, w), lambda i: (0, n - 1 - i, 0))
    return pl.pallas_call(
        functools.partial(_dn_kernel, nb=b, tb=tb),
        grid=(n,),
        in_specs=[fwd(256), fwd(256), fwd(256), fwd(128), bwd(256), bwd(256), bwd(256), bwd(128)],
        out_specs=[fwd(256), bwd(256)],
        out_shape=[jax.ShapeDtypeStruct((b, s, 256), BF16)] * 2,
        scratch_shapes=[pltpu.VMEM((2 * b, 256, 256), F32),
                        pltpu.VMEM((2, 128, 768), BF16), pltpu.VMEM((tb, 2 * tb), BF16),
                        pltpu.VMEM((256, 256), BF16)],
        compiler_params=_cparams(("arbitrary",)), name="dn",
    )(dq, dk, dv, bg, dq, dk, dv, bg)


def _attn_kernel(qt_ref, k_ref, vt_ref, o_ref, st_ref, *, tq, tk, n_q, n_kv, unroll):
    per_q = n_kv // unroll
    n_trips = n_q * per_q

    def scores(qi, j, slot):
        kj = k_ref[0, pl.ds(pl.multiple_of(j * tk, tk), tk), :]
        st = jnp.dot(kj, qt_ref[0, 0, qi], preferred_element_type=F32)
        st_ref[slot, :, 0:2 * tq] = st
        return jnp.max(st, axis=0, keepdims=True)

    def update(j, st, mx, m, acc):
        m_new = jnp.maximum(m, mx)
        p = jnp.exp2(st - m_new).astype(BF16)
        alpha = jnp.exp2(m - m_new)
        return m_new, alpha * acc + jnp.dot(vt_ref[0, 0, j], p, preferred_element_type=F32)

    def body(t, carry):
        qi = t // per_q
        base = (t - qi * per_q) * unroll
        first = base == 0
        m = jnp.where(first, -jnp.inf, carry[0])
        acc = jnp.where(first, 0.0, carry[1])
        mx = carry[2]
        t_next = jnp.minimum(t + 1, n_trips - 1)
        qi_next = t_next // per_q
        base_next = (t_next - qi_next * per_q) * unroll
        for r in range(unroll):
            if r < unroll - 1:
                mx_next = scores(qi, base + r + 1, (r + 1) % 2)
            else:
                mx_next = scores(qi_next, base_next, 0)
            m, acc = update(base + r, st_ref[r % 2, :, 0:2 * tq], mx, m, acc)
            mx = mx_next

        @pl.when(base == n_kv - unroll)
        def _():
            o = acc[0:HEAD] / acc[HEAD:HEAD + 1]
            ot = jnp.concatenate([o[:, 0:tq], o[:, tq:2 * tq]], axis=0)
            o_ref[0, pl.ds(pl.multiple_of(qi * tq, tq), tq), :] = ot.T.astype(BF16)

        return m, acc, mx

    m0 = jnp.full((1, 2 * tq), -jnp.inf, F32)
    a0 = jnp.zeros((HEAD + 16, 2 * tq), F32)
    lax.fori_loop(0, n_trips, body, (m0, a0, scores(0, 0, 0)))


def _attn_call(aqt, ak, avt):
    b, s, _ = ak.shape
    _, _, n_q, _, tq2 = aqt.shape
    _, _, n_kv, vrows, tk = avt.shape
    unroll = next(u for u in (32, 16, 8, 4, 2) if n_kv % u == 0)
    return pl.pallas_call(
        functools.partial(_attn_kernel, tq=tq2 // 2, tk=tk, n_q=n_q, n_kv=n_kv, unroll=unroll),
        grid=(b, 2),
        in_specs=[pl.BlockSpec((1, 1, n_q, 128, tq2), lambda bi, g: (bi, g, 0, 0, 0)),
                  pl.BlockSpec((1, s, 128), lambda bi, g: (bi, 0, 0)),
                  pl.BlockSpec((1, 1, n_kv, vrows, tk), lambda bi, g: (bi, g, 0, 0, 0))],
        out_specs=pl.BlockSpec((1, s, 128), lambda bi, g: (bi, 0, g)),
        out_shape=jax.ShapeDtypeStruct((b, s, 256), BF16),
        scratch_shapes=[pltpu.VMEM((2, tk, tq2 + 128), F32)],
        compiler_params=_cparams(("parallel", "parallel")), name="attn",
    )(aqt, ak, avt)


def _out_kernel(x_ref, ya_ref, yd_ref, yc_ref, of_ref, ob_ref, gates_ref, dnw_ref, wo_ref, o_ref):
    o = of_ref[0].astype(F32) + ob_ref[0].astype(F32)
    gates = gates_ref[0].astype(F32)
    on = o * lax.rsqrt(_head_sum(o * o) * (1.0 / HEAD) + EPS) * dnw_ref[...]
    yb = (on * gates[:, 0:256]).astype(BF16)
    yc = (yc_ref[0].astype(F32) * gates[:, 256:512]).astype(BF16)
    mix = jnp.concatenate([ya_ref[0], yb, yc, yd_ref[0]], axis=1)
    o_ref[0] = x_ref[0] + jnp.dot(mix, wo_ref[...], preferred_element_type=F32)


def _out_call(x, ya, yd, yc, o_f, o_b, gates, lw, tm):
    b, s, _ = x.shape
    tok = lambda w: pl.BlockSpec((1, tm, w), lambda bi, i: (bi, i, 0))
    full = lambda shape: pl.BlockSpec(shape, lambda bi, i: (0,) * len(shape))
    return pl.pallas_call(
        _out_kernel, grid=(b, s // tm),
        in_specs=[tok(D_MODEL), tok(256), tok(256), tok(256), tok(256), tok(256), tok(512),
                  full((1, 256)), full((D_MODEL, D_MODEL))],
        out_specs=tok(D_MODEL), out_shape=jax.ShapeDtypeStruct((b, s, D_MODEL), F32),
        compiler_params=_cparams(("parallel", "parallel")), name="out",
    )(x, ya, yd, yc, o_f, o_b, gates, lw["dn_norm_w"], lw["w_out"])


def _rope_tables(seq_len):
    t = jnp.arange(seq_len)
    pos = jnp.stack([t // GRID_W, t % GRID_W], axis=-1).astype(F32)
    n_freq = HEAD // 4
    inv_freq = jnp.power(ROPE_THETA, -2.0 * jnp.arange(n_freq, dtype=F32) / (HEAD // 2))
    ang = pos[:, :, None] * inv_freq
    cos = jnp.repeat(jnp.cos(ang)[:, :, None, :], 2, axis=2).reshape(seq_len, HEAD)
    sin = jnp.sin(ang)
    sin = jnp.stack([-sin, sin], axis=2).reshape(seq_len, HEAD)
    return jnp.tile(cos, (1, 2)), jnp.tile(sin, (1, 2))


def _layer_weights(l, norm_w, w_in, sgu_w, sgu_b, conv_w, a_log, dt_bias, dn_norm_w, q_norm_w,
                   k_norm_w, pool_w, pool_scale, w_out):
    w = w_in[l]
    cols = lambda a, n: w[:, a:a + n]
    w_halo = jnp.concatenate([cols(_B_Q, 768), cols(_D_X, 256)], axis=1).astype(BF16)
    w_main = jnp.concatenate([
        cols(_A_U, 768), cols(_B_Z, 256), cols(_C_Q, 768), cols(_D_Z, 256), cols(_B_BETA, 16),
        jnp.zeros((D_MODEL, 112), F32)], axis=1).astype(BF16)
    dn_row = jnp.zeros((8, 128), F32)
    dn_row = dn_row.at[0, 8:16].set(a_log[l].reshape(8)).at[1, 8:16].set(dt_bias[l].reshape(8))
    pool_bd = jnp.zeros((256, 256), F32)
    for gi in range(len(POOL_WINDOWS)):
        pool_bd = pool_bd.at[gi * 64:(gi + 1) * 64, gi * 64:(gi + 1) * 64].set(pool_w[l, gi])
    return {
        "norm_w": norm_w[l].reshape(1, D_MODEL),
        "w_halo": w_halo, "w_main": w_main,
        "sgu_w": jnp.transpose(sgu_w[l], (1, 0, 2)).reshape(SGU_CHUNK, 4 * SGU_CHUNK).astype(BF16),
        "sgu_b": jnp.repeat(sgu_b[l].T, HEAD, axis=1),
        "conv_w": jnp.concatenate([conv_w[l], jnp.zeros((3, 768), F32)], axis=0),
        "dn_row": dn_row,
        "q_norm_w": jnp.tile(q_norm_w[l], 4).reshape(1, 256),
        "k_norm_w": jnp.tile(k_norm_w[l], 2).reshape(1, 128),
        "pool_w": pool_bd.astype(BF16),
        "pool_scale": pool_scale[l].reshape(1, 256),
        "dn_norm_w": jnp.tile(dn_norm_w[l], 4).reshape(1, 256),
        "w_out": w_out[l].astype(BF16),
    }


def _tiles(batch, seq_len):
    tb = min(DN_CHUNK * max(1, 8 // batch), seq_len)
    return dict(tm=min(512, seq_len), tb=tb, tq=min(256, seq_len))


def _layer(x, lw, rope_c, rope_s):
    t = _tiles(x.shape[0], x.shape[1])
    ya, yd, dq, dk, dv, bg, gates, aqt, ak, avt = _proj_call(x, lw, rope_c, rope_s, t["tm"], t["tq"])
    o_f, o_b = _dn_call(dq, dk, dv, bg, t["tb"])
    yc = _attn_call(aqt, ak, avt)
    return _out_call(x, ya, yd, yc, o_f, o_b, gates, lw, t["tm"])


def kernel(x_prompt, x_sample, norm_w, w_in, sgu_w, sgu_b, conv_w, a_log, dt_bias, dn_norm_w,
           q_norm_w, k_norm_w, pool_w, pool_scale, w_out):
    depth = norm_w.shape[0]
    rope_p = _rope_tables(x_prompt.shape[1])
    rope_s = _rope_tables(x_sample.shape[1])
    y_prompt, y_sample = x_prompt, x_sample
    for l in range(depth):
        lw = _layer_weights(l, norm_w, w_in, sgu_w, sgu_b, conv_w, a_log, dt_bias, dn_norm_w,
                            q_norm_w, k_norm_w, pool_w, pool_scale, w_out)
        y_prompt = _layer(y_prompt, lw, *rope_p)
        y_sample = _layer(y_sample, lw, *rope_s)
    return (y_prompt, y_sample)
```

```python
import functools

import jax
import jax.numpy as jnp
from jax import lax
from jax.experimental import pallas as pl
from jax.experimental.pallas import tpu as pltpu

F32 = jnp.float32
BF16 = jnp.bfloat16

D_MODEL = 1024
HEAD = 64
N_HEADS = 4
GRID_W = 64
EPS = 1e-6
SGU_CHUNK = 128
DN_CHUNK = 64
ROPE_THETA = 10000.0
POOL_WINDOWS = (2, 4, 8, 16)
HALO = 16
NEG_BIG = -1e30
LOG2_E = 1.4426950408889634
ATTN_TK = 512
SCORE_BOUND = 64.0
SCORE_MARGIN = 1.05

V7X_VMEM_LIMIT_BYTES = 56 * 1024 * 1024

_A_U = 0
_B_Q, _B_Z, _B_BETA = 768, 1536, 1792
_C_Q = 1808
_D_X, _D_Z = 2576, 2832
MAIN_COLS = 2176


def _cparams(semantics):
    return pltpu.CompilerParams(dimension_semantics=semantics,
                                vmem_limit_bytes=V7X_VMEM_LIMIT_BYTES)


def _split2(x):
    hi = x.astype(BF16)
    lo = (x - hi.astype(F32)).astype(BF16)
    return hi, lo


def _iota(shape, dim):
    return lax.broadcasted_iota(jnp.int32, shape, dim)


def _ones_where(cond):
    return jnp.where(cond, 1.0, 0.0).astype(BF16)


def _head_sum(x2):
    w = x2.shape[1]
    g = _ones_where(jnp.right_shift(_iota((w, w), 0), 6) == jnp.right_shift(_iota((w, w), 1), 6))
    return jnp.dot(x2.astype(BF16), g, preferred_element_type=F32)


def _silu(z):
    return 0.5 * z * (1.0 + jnp.tanh(0.5 * z))


def _bd_mask():
    return jnp.right_shift(_iota((256, 256), 0), 6) == jnp.right_shift(_iota((256, 256), 1), 6)


def _bd(x, mask01):
    z = jnp.zeros((HEAD, 128), x.dtype)
    blocks = []
    for h in range(N_HEADS):
        t = h // 2
        m = x[:, t * 128:(t + 1) * 128] * mask01[h * HEAD:(h + 1) * HEAD, t * 128:(t + 1) * 128]
        blocks.append(jnp.concatenate([m, z] if t == 0 else [z, m], axis=1))
    return jnp.concatenate(blocks, axis=0)


def _proj_kernel(xp_ref, xc_ref, xn_ref, nw_ref, wh_ref, wm_ref, sguw_ref, sgub_ref, convw_ref,
                 dnrow_ref, qw_ref, kw_ref, rc_ref, rs_ref, poolw_ref, pools_ref,
                 ya_ref, yd_ref, dq_ref, dk_ref, dv_ref, bg_ref, gates_ref, aqt_ref, ak_ref, avt_ref,
                 hext_ref, *, tm, tq, n_tiles, seq_len):
    i = pl.program_id(1)
    n_ext = tm + 2 * HALO
    nw = nw_ref[...]

    def norm(x):
        ms = jnp.mean(x * x, axis=-1, keepdims=True)
        return x * lax.rsqrt(ms + EPS) * nw

    hext_ref[0:HALO, :] = jnp.where(i > 0, norm(xp_ref[0]), 0.0).astype(BF16)
    hext_ref[HALO:HALO + tm, :] = norm(xc_ref[0]).astype(BF16)
    hext_ref[HALO + tm:n_ext, :] = jnp.where(i < n_tiles - 1, norm(xn_ref[0]), 0.0).astype(BF16)

    ph = jnp.dot(hext_ref[...], wh_ref[...], preferred_element_type=F32)
    pm = jnp.dot(hext_ref[HALO:HALO + tm, :], wm_ref[...], preferred_element_type=F32)

    def rows(x):
        return x[HALO:HALO + tm]

    a_u, a_v, a_z = pm[:, 0:256], pm[:, 256:512], pm[:, 512:768]
    vn = a_v * lax.rsqrt(_head_sum(a_v * a_v) * (1.0 / HEAD) + EPS)
    mask4 = _ones_where(jnp.right_shift(_iota((512, 256), 0), 7) == jnp.right_shift(_iota((512, 256), 1), 6))
    mixed = []
    for c in range(tm // SGU_CHUNK):
        vc = vn[c * SGU_CHUNK:(c + 1) * SGU_CHUNK].astype(BF16)
        bdv = jnp.concatenate([vc, vc, vc, vc], axis=0) * mask4
        mixed.append(jnp.dot(sguw_ref[...], bdv, preferred_element_type=F32) + sgub_ref[...])
    mixed = jnp.concatenate(mixed, axis=0)
    ya_ref[0] = (a_u * mixed * _silu(a_z)).astype(BF16)

    xd = ph[:, 768:1024]
    a1 = xd + pltpu.roll(xd, n_ext - 1, 0)
    a2 = a1 + pltpu.roll(a1, n_ext - 2, 0)
    a3 = a2 + pltpu.roll(a2, n_ext - 4, 0)
    a4 = a3 + pltpu.roll(a3, n_ext - 8, 0)
    w2 = rows(pltpu.roll(a1, 1, 0))
    w4 = rows(pltpu.roll(a2, 2, 0))
    w8 = rows(pltpu.roll(a3, 4, 0))
    w16 = rows(pltpu.roll(a4, 8, 0))
    grp = jnp.right_shift(_iota((tm, 256), 1), 6)
    half = jnp.left_shift(jnp.ones((tm, 256), jnp.int32), grp)
    t = i * tm + _iota((tm, 256), 0)
    cnt = (jnp.minimum(t + half, seq_len) - jnp.maximum(t - half, 0)).astype(F32)
    win = jnp.where(grp == 0, w2, jnp.where(grp == 1, w4, jnp.where(grp == 2, w8, w16)))
    diff = win / cnt - rows(xd)
    yd = jnp.dot(diff.astype(BF16), poolw_ref[...], preferred_element_type=F32) * pools_ref[...]
    yd_ref[0] = (yd * _silu(pm[:, 1792:2048])).astype(BF16)

    xb = ph[:, 0:768]
    cw = convw_ref[...]
    conv = (rows(pltpu.roll(xb, 2, 0)) * cw[0:1] + rows(pltpu.roll(xb, 1, 0)) * cw[1:2]
            + rows(xb) * cw[2:3] + rows(pltpu.roll(xb, n_ext - 1, 0)) * cw[3:4]
            + rows(pltpu.roll(xb, n_ext - 2, 0)) * cw[4:5])
    act = _silu(conv)
    bq, bk = act[:, 0:256], act[:, 256:512]
    dq_ref[0] = (bq * lax.rsqrt(_head_sum(bq * bq) + EPS) * (HEAD ** -0.5)).astype(BF16)
    dk_ref[0] = (bk * lax.rsqrt(_head_sum(bk * bk) + EPS)).astype(BF16)
    dv_ref[0] = act[:, 512:768].astype(BF16)
    ba = pm[:, 2048:2176]
    lane = _iota((tm, 128), 1)
    xa = ba + dnrow_ref[1:2, :]
    softplus = jnp.maximum(xa, 0.0) + jnp.log1p(jnp.exp(-jnp.abs(xa)))
    g = -jnp.exp(dnrow_ref[0:1, :]) * softplus
    bg_ref[0] = jnp.where(lane < 8, 1.0 / (1.0 + jnp.exp(-ba)), g)
    gates_ref[0] = jnp.concatenate([_silu(pm[:, 768:1024]), _silu(pm[:, 1536:1792])], axis=1).astype(BF16)

    rc, rs = rc_ref[...], rs_ref[...]
    first = (_iota((tm, 128), 1) & 16) == 0

    def rope(x):
        sw = jnp.where(first, pltpu.roll(x, 112, 1), pltpu.roll(x, 16, 1))
        return x * rc + sw * rs

    cq, ck = pm[:, 1024:1280], pm[:, 1280:1408]
    qn = cq * lax.rsqrt(_head_sum(cq * cq) * (1.0 / HEAD) + EPS) * qw_ref[...]
    qr = jnp.concatenate([rope(qn[:, 0:128]), rope(qn[:, 128:256])], axis=1) * (HEAD ** -0.5 * LOG2_E)
    qt = qr.T.astype(BF16)
    zq = jnp.zeros((HEAD, 2 * tq), BF16)
    for r in range(tm // tq):
        cs = slice(r * tq, (r + 1) * tq)
        top0 = jnp.concatenate([qt[0:64, cs], qt[64:128, cs]], axis=1)
        top1 = jnp.concatenate([qt[128:192, cs], qt[192:256, cs]], axis=1)
        aqt_ref[0, 0, r] = jnp.concatenate([top0, zq], axis=0)
        aqt_ref[0, 1, r] = jnp.concatenate([zq, top1], axis=0)
    kn = ck * lax.rsqrt(_head_sum(ck * ck) * (1.0 / HEAD) + EPS) * kw_ref[...]
    ak_ref[0] = rope(kn).astype(BF16)
    vt = pm[:, 1408:1536].T
    ones = jnp.ones((16, tm), F32)
    for g in range(2):
        vg = jnp.concatenate([vt[g * HEAD:(g + 1) * HEAD], ones], axis=0).astype(BF16)
        for r in range(tm // ATTN_TK):
            avt_ref[0, g, r] = vg[:, r * ATTN_TK:(r + 1) * ATTN_TK]


def _proj_call(x, lw, rope_c, rope_s, tm, tq):
    b, s, _ = x.shape
    n_tiles = s // tm
    hb = tm // HALO
    last_hb = s // HALO - 1
    full = lambda shape: pl.BlockSpec(shape, lambda bi, i: (0,) * len(shape))
    tok = lambda w: pl.BlockSpec((1, tm, w), lambda bi, i: (bi, i, 0))
    in_specs = [
        pl.BlockSpec((1, HALO, D_MODEL), lambda bi, i: (bi, jnp.maximum(i * hb - 1, 0), 0)),
        pl.BlockSpec((1, tm, D_MODEL), lambda bi, i: (bi, i, 0)),
        pl.BlockSpec((1, HALO, D_MODEL), lambda bi, i: (bi, jnp.minimum((i + 1) * hb, last_hb), 0)),
        full((1, D_MODEL)), full((D_MODEL, 1024)), full((D_MODEL, MAIN_COLS)),
        full((128, 512)), full((128, 256)), full((8, 768)), full((8, 128)),
        full((1, 256)), full((1, 128)),
        pl.BlockSpec((tm, 128), lambda bi, i: (i, 0)), pl.BlockSpec((tm, 128), lambda bi, i: (i, 0)),
        full((256, 256)), full((1, 256)),
    ]
    out_shape = [
        jax.ShapeDtypeStruct((b, s, 256), BF16),
        jax.ShapeDtypeStruct((b, s, 256), BF16),
        jax.ShapeDtypeStruct((b, s, 256), BF16),
        jax.ShapeDtypeStruct((b, s, 256), BF16),
        jax.ShapeDtypeStruct((b, s, 256), BF16),
        jax.ShapeDtypeStruct((b, s, 128), F32),
        jax.ShapeDtypeStruct((b, s, 512), BF16),
        jax.ShapeDtypeStruct((b, 2, s // tq, 128, 2 * tq), BF16),
        jax.ShapeDtypeStruct((b, s, 128), BF16),
        jax.ShapeDtypeStruct((b, 2, s // ATTN_TK, HEAD + 16, ATTN_TK), BF16),
    ]
    out_specs = [tok(256), tok(256), tok(256), tok(256), tok(256), tok(128), tok(512),
                 pl.BlockSpec((1, 2, tm // tq, 128, 2 * tq), lambda bi, i: (bi, 0, i, 0, 0)), tok(128),
                 pl.BlockSpec((1, 2, tm // ATTN_TK, HEAD + 16, ATTN_TK), lambda bi, i: (bi, 0, i, 0, 0))]
    return pl.pallas_call(
        functools.partial(_proj_kernel, tm=tm, tq=tq, n_tiles=n_tiles, seq_len=s),
        grid=(b, n_tiles), in_specs=in_specs, out_specs=out_specs, out_shape=out_shape,
        scratch_shapes=[pltpu.VMEM((tm + 2 * HALO, D_MODEL), BF16)],
        compiler_params=_cparams(("parallel", "parallel")), name="proj",
    )(x, x, x, lw["norm_w"], lw["w_halo"], lw["w_main"], lw["sgu_w"], lw["sgu_b"], lw["conv_w"],
      lw["dn_row"], lw["q_norm_w"], lw["k_norm_w"], rope_c, rope_s, lw["pool_w"], lw["pool_scale"])


def _dn_kernel(qf_ref, kf_ref, vf_ref, bgf_ref, qb_ref, kb_ref, vb_ref, bgb_ref,
               of_ref, ob_ref, state_ref, expand_ref, ones_ref, bd_ref, *, nb, tb):
    nc = tb // DN_CHUNK

    @pl.when(pl.program_id(0) == 0)
    def _():
        state_ref[...] = jnp.zeros_like(state_ref)
        r = _iota((tb, tb), 0)
        c = _iota((tb, tb), 1)
        same = jnp.right_shift(r, 6) == jnp.right_shift(c, 6)
        src = _iota((128, 768), 0)
        col = _iota((128, 768), 1)
        for d in range(2):
            expand_ref[d] = _ones_where((src & 63) == jnp.left_shift(jnp.right_shift(col, 8), 3) + 4 * d
                                        + (jnp.right_shift(col, 6) & 3))
        ones_ref[...] = jnp.concatenate([_ones_where(same), _ones_where(same)], axis=1)
        bd_ref[...] = _ones_where(_bd_mask())

    li = _iota((tb, 256), 0) & 63
    lj = _iota((tb, 256), 1) & 63
    eyecat = li == lj
    eyef = jnp.where(eyecat, 1.0, 0.0)
    j2 = ones_ref[...]
    bdm = _bd_mask()
    bdm01 = bd_ref[...]
    lane = _iota((tb, 128), 1)
    zpad = jnp.zeros((DN_CHUNK, 256), BF16)

    dir_consts = []
    for d in range(2):
        dir_consts.append(dict(expand=expand_ref[d],
                               incl=(li >= lj) if d == 0 else (li <= lj),
                               strict=(li > lj) if d == 0 else (li < lj)))
    blocks = []
    for d in range(2):
        q_ref, k_ref, v_ref, bg_ref = ((qf_ref, kf_ref, vf_ref, bgf_ref), (qb_ref, kb_ref, vb_ref, bgb_ref))[d]
        for bi in range(nb):
            blocks.append(dict(d=d, bi=bi, q_ref=q_ref, k_ref=k_ref, v_ref=v_ref, bg=bg_ref[bi], **dir_consts[d]))
    row64 = _iota((tb, 128), 0) & 63
    for blk in blocks:
        c = blk["bg"]
        for s in (1, 2, 4, 8, 16, 32):
            c = c + jnp.where(row64 >= s, pltpu.roll(c, s, 0), 0.0)
        tot = jnp.concatenate([jnp.broadcast_to(c[ci * DN_CHUNK + DN_CHUNK - 1:(ci + 1) * DN_CHUNK, :],
                                                (DN_CHUNK, 128)) for ci in range(nc)], axis=0)
        if blk["d"] == 0:
            blk["cs"] = (c, tot - c)
        else:
            blk["cs"] = (tot - c + blk["bg"], c - blk["bg"])
    for blk in blocks:
        cs_in, cs_st = blk["cs"]
        nbx = jnp.where(lane < 8, blk["bg"], jnp.where(lane < 16, cs_in, pltpu.roll(cs_st, 8, 1)))
        hi = nbx.astype(BF16)
        lo = pltpu.roll(nbx - hi.astype(F32), 64, 1)
        packed = jnp.where(lane < 64, hi.astype(F32), lo).astype(BF16)
        blk["x"] = jnp.dot(packed, blk["expand"], preferred_element_type=F32)
    for blk in blocks:
        gcx = blk["x"][:, 256:512]
        blk["rowf"] = jnp.dot(j2, jnp.concatenate(_split2(jnp.where(eyecat, gcx, 0.0)), axis=0),
                              preferred_element_type=F32)
    for blk in blocks:
        gram, qk = [], []
        for ci in range(nc):
            sl = slice(ci * DN_CHUNK, (ci + 1) * DN_CHUNK)
            kc = blk["k_ref"][blk["bi"], sl, :]
            gq = lax.dot_general(jnp.concatenate([kc, blk["q_ref"][blk["bi"], sl, :]], axis=0), _bd(kc, bdm01),
                                 (((1,), (1,)), ((), ())), preferred_element_type=F32)
            gram.append(gq[0:64])
            qk.append(gq[64:128])
        blk["gram"] = jnp.concatenate(gram, axis=0)
        blk["qk"] = jnp.concatenate(qk, axis=0)
    chains = {}
    for blk in blocks:
        d, bi = blk["d"], blk["bi"]
        q = blk["q_ref"][bi].astype(F32)
        k = blk["k_ref"][bi].astype(F32)
        v = blk["v_ref"][bi].astype(F32)
        x = blk["x"]
        bx, gcx, dglx = x[:, 0:256], x[:, 256:512], x[:, 512:768]
        decay = jnp.exp(jnp.where(blk["incl"], gcx - blk["rowf"], NEG_BIG))
        eg = jnp.exp(gcx)
        a_all = jnp.where(blk["strict"], bx * blk["gram"] * decay, 0.0)
        p_all = eyef - a_all
        intra = jnp.where(blk["incl"], blk["qk"] * decay, 0.0).astype(BF16)
        qd = (q * eg).astype(BF16)
        kdt = (k * jnp.exp(dglx)).T.astype(BF16)
        vbeta = (v * bx).astype(BF16)
        kbg = (k * bx * eg).astype(BF16)
        for ci in range(nc):
            sl = slice(ci * DN_CHUNK, (ci + 1) * DN_CHUNK)
            last = ci * DN_CHUNK + (DN_CHUNK - 1 if d == 0 else 0)
            pair = ci // 2
            chains[(d, bi, ci)] = dict(
                ak=a_all[sl], p=p_all[sl], vb=vbeta[sl], kbg=kbg[sl], intra=intra[sl], qd=qd[sl],
                kpair=kdt[:, pair * 128:(pair + 1) * 128], egl=eg[last:last + 1, :])

    def prepare(group, phase):
        for ch in group:
            if phase < 5:
                akb = ch["ak"].astype(BF16)
                lhs = akb if phase == 0 else jnp.concatenate([akb, ch["p"].astype(BF16)], axis=0)
                res = jnp.dot(lhs, _bd(akb, bdm01), preferred_element_type=F32)
                ch["ak"] = res[0:64]
                if phase > 0:
                    ch["p"] = ch["p"] + res[64:128]
            elif phase == 5:
                ch["p"] = ch["p"] + jnp.dot(ch["p"].astype(BF16), _bd(ch["ak"].astype(BF16), bdm01),
                                            preferred_element_type=F32)
            else:
                rhs = jnp.concatenate([_bd(ch["vb"], bdm01), _bd(ch["kbg"], bdm01)], axis=1)
                uw = jnp.dot(ch["p"].astype(BF16), rhs, preferred_element_type=F32)
                ch["u"] = uw[:, 0:256]
                ch["w"] = uw[:, 256:512].astype(BF16)

    def recur_a(group):
        for ch in group:
            ch["state"] = state_ref[ch["si"]]
            res = jnp.dot(jnp.concatenate([ch["w"], ch["qd"]], axis=0), ch["state"].astype(BF16),
                          preferred_element_type=F32)
            ch["v_new"] = (ch["u"] - res[0:64]).astype(BF16)
            ch["o_inter"] = res[64:128]

    def recur_b(group):
        for ch in group:
            v_new = ch["v_new"]
            ch["o_ref"][ch["bi"], ch["sl"], :] = (ch["o_inter"] + jnp.dot(
                ch["intra"], _bd(v_new, bdm01), preferred_element_type=F32)).astype(BF16)
            vpad = jnp.concatenate([v_new, zpad] if ch["even"] else [zpad, v_new], axis=0)
            ds = jnp.dot(ch["kpair"], vpad, preferred_element_type=F32)
            state_ref[ch["si"]] = ch["state"] * ch["egl"] + jnp.where(bdm, ds, 0.0)

    groups = []
    for step in range(nc):
        group = []
        for bi in range(nb):
            for d in range(2):
                ci = step if d == 0 else nc - 1 - step
                ch = chains[(d, bi, ci)]
                ch.update(si=d * nb + bi, bi=bi, o_ref=(of_ref, ob_ref)[d], even=ci % 2 == 0,
                          sl=slice(ci * DN_CHUNK, (ci + 1) * DN_CHUNK))
                group.append(ch)
        groups.append(group)

    for phase in range(7):
        prepare(groups[0], phase)
    for step in range(nc):
        nxt = groups[step + 1] if step + 1 < nc else []
        prepare(nxt, 0)
        prepare(nxt, 1)
        recur_a(groups[step])
        prepare(nxt, 2)
        prepare(nxt, 3)
        recur_b(groups[step])
        prepare(nxt, 4)
        prepare(nxt, 5)
        prepare(nxt, 6)


def _dn_call(dq, dk, dv, bg, tb):
    b, s, _ = dq.shape
    n = s // tb
    fwd = lambda w: pl.BlockSpec((b, tb, w), lambda i: (0, i, 0))
    bwd = lambda w: pl.BlockSpec((b, tb, w), lambda i: (0, n - 1 - i, 0))
    return pl.pallas_call(
        functools.partial(_dn_kernel, nb=b, tb=tb),
        grid=(n,),
        in_specs=[fwd(256), fwd(256), fwd(256), fwd(128), bwd(256), bwd(256), bwd(256), bwd(128)],
        out_specs=[fwd(256), bwd(256)],
        out_shape=[jax.ShapeDtypeStruct((b, s, 256), BF16)] * 2,
        scratch_shapes=[pltpu.VMEM((2 * b, 256, 256), F32),
                        pltpu.VMEM((2, 128, 768), BF16), pltpu.VMEM((tb, 2 * tb), BF16),
                        pltpu.VMEM((256, 256), BF16)],
        compiler_params=_cparams(("arbitrary",)), name="dn",
    )(dq, dk, dv, bg, dq, dk, dv, bg)


def _attn_kernel(qt_ref, k_ref, vt_ref, o_ref, st_ref, *, tq, tk, n_q, n_kv, unroll, bounded):
    per_q = n_kv // unroll
    n_trips = n_q * per_q

    def scores(qi, j, slot):
        kj = k_ref[0, pl.ds(pl.multiple_of(j * tk, tk), tk), :]
        st = jnp.dot(kj, qt_ref[0, 0, qi], preferred_element_type=F32)
        st_ref[slot, :, 0:2 * tq] = st
        return None if bounded else jnp.max(st, axis=0, keepdims=True)

    def update(j, st, mx, m, acc):
        if bounded:
            return m, acc + jnp.dot(vt_ref[0, 0, j], jnp.exp2(st).astype(BF16), preferred_element_type=F32)
        m_new = jnp.maximum(m, mx)
        p = jnp.exp2(st - m_new).astype(BF16)
        alpha = jnp.exp2(m - m_new)
        return m_new, alpha * acc + jnp.dot(vt_ref[0, 0, j], p, preferred_element_type=F32)

    def body(t, carry):
        qi = t // per_q
        base = (t - qi * per_q) * unroll
        first = base == 0
        m = jnp.where(first, -jnp.inf, carry[0])
        acc = jnp.where(first, 0.0, carry[1])
        mx = carry[2]
        t_next = jnp.minimum(t + 1, n_trips - 1)
        qi_next = t_next // per_q
        base_next = (t_next - qi_next * per_q) * unroll
        for r in range(unroll):
            if r < unroll - 1:
                mx_next = scores(qi, base + r + 1, (r + 1) % 2)
            else:
                mx_next = scores(qi_next, base_next, 0)
            m, acc = update(base + r, st_ref[r % 2, :, 0:2 * tq], mx, m, acc)
            mx = mx if bounded else mx_next

        @pl.when(base == n_kv - unroll)
        def _():
            o = acc[0:HEAD] / acc[HEAD:HEAD + 1]
            ot = jnp.concatenate([o[:, 0:tq], o[:, tq:2 * tq]], axis=0)
            o_ref[0, pl.ds(pl.multiple_of(qi * tq, tq), tq), :] = ot.T.astype(BF16)

        return m, acc, mx

    m0 = jnp.full((1, 2 * tq), -jnp.inf, F32)
    a0 = jnp.zeros((HEAD + 16, 2 * tq), F32)
    mx0 = scores(0, 0, 0)
    lax.fori_loop(0, n_trips, body, (m0, a0, m0 if bounded else mx0))


def _attn_call(aqt, ak, avt, bounded):
    b, s, _ = ak.shape
    _, _, n_q, _, tq2 = aqt.shape
    _, _, n_kv, vrows, tk = avt.shape
    unroll = next(u for u in (32, 16, 8, 4, 2) if n_kv % u == 0)
    return pl.pallas_call(
        functools.partial(_attn_kernel, tq=tq2 // 2, tk=tk, n_q=n_q, n_kv=n_kv, unroll=unroll,
                          bounded=bounded),
        grid=(b, 2),
        in_specs=[pl.BlockSpec((1, 1, n_q, 128, tq2), lambda bi, g: (bi, g, 0, 0, 0)),
                  pl.BlockSpec((1, s, 128), lambda bi, g: (bi, 0, 0)),
                  pl.BlockSpec((1, 1, n_kv, vrows, tk), lambda bi, g: (bi, g, 0, 0, 0))],
        out_specs=pl.BlockSpec((1, s, 128), lambda bi, g: (bi, 0, g)),
        out_shape=jax.ShapeDtypeStruct((b, s, 256), BF16),
        scratch_shapes=[pltpu.VMEM((2, tk, tq2 + 128), F32)],
        compiler_params=_cparams(("parallel", "parallel")), name="attn",
    )(aqt, ak, avt)


def _out_kernel(x_ref, ya_ref, yd_ref, yc_ref, of_ref, ob_ref, gates_ref, dnw_ref, wo_ref, o_ref):
    o = of_ref[0].astype(F32) + ob_ref[0].astype(F32)
    gates = gates_ref[0].astype(F32)
    on = o * lax.rsqrt(_head_sum(o * o) * (1.0 / HEAD) + EPS) * dnw_ref[...]
    yb = (on * gates[:, 0:256]).astype(BF16)
    yc = (yc_ref[0].astype(F32) * gates[:, 256:512]).astype(BF16)
    mix = jnp.concatenate([ya_ref[0], yb, yc, yd_ref[0]], axis=1)
    o_ref[0] = x_ref[0] + jnp.dot(mix, wo_ref[...], preferred_element_type=F32)


def _out_call(x, ya, yd, yc, o_f, o_b, gates, lw, tm):
    b, s, _ = x.shape
    tok = lambda w: pl.BlockSpec((1, tm, w), lambda bi, i: (bi, i, 0))
    full = lambda shape: pl.BlockSpec(shape, lambda bi, i: (0,) * len(shape))
    return pl.pallas_call(
        _out_kernel, grid=(b, s // tm),
        in_specs=[tok(D_MODEL), tok(256), tok(256), tok(256), tok(256), tok(256), tok(512),
                  full((1, 256)), full((D_MODEL, D_MODEL))],
        out_specs=tok(D_MODEL), out_shape=jax.ShapeDtypeStruct((b, s, D_MODEL), F32),
        compiler_params=_cparams(("parallel", "parallel")), name="out",
    )(x, ya, yd, yc, o_f, o_b, gates, lw["dn_norm_w"], lw["w_out"])


def _rope_tables(seq_len):
    t = jnp.arange(seq_len)
    pos = jnp.stack([t // GRID_W, t % GRID_W], axis=-1).astype(F32)
    n_freq = HEAD // 4
    inv_freq = jnp.power(ROPE_THETA, -2.0 * jnp.arange(n_freq, dtype=F32) / (HEAD // 2))
    ang = pos[:, :, None] * inv_freq
    cos = jnp.repeat(jnp.cos(ang)[:, :, None, :], 2, axis=2).reshape(seq_len, HEAD)
    sin = jnp.sin(ang)
    sin = jnp.stack([-sin, sin], axis=2).reshape(seq_len, HEAD)
    return jnp.tile(cos, (1, 2)), jnp.tile(sin, (1, 2))


def _layer_weights(l, norm_w, w_in, sgu_w, sgu_b, conv_w, a_log, dt_bias, dn_norm_w, q_norm_w,
                   k_norm_w, pool_w, pool_scale, w_out):
    w = w_in[l]
    cols = lambda a, n: w[:, a:a + n]
    w_halo = jnp.concatenate([cols(_B_Q, 768), cols(_D_X, 256)], axis=1).astype(BF16)
    w_main = jnp.concatenate([
        cols(_A_U, 768), cols(_B_Z, 256), cols(_C_Q, 768), cols(_D_Z, 256), cols(_B_BETA, 16),
        jnp.zeros((D_MODEL, 112), F32)], axis=1).astype(BF16)
    dn_row = jnp.zeros((8, 128), F32)
    dn_row = dn_row.at[0, 8:16].set(a_log[l].reshape(8)).at[1, 8:16].set(dt_bias[l].reshape(8))
    pool_bd = jnp.zeros((256, 256), F32)
    for gi in range(len(POOL_WINDOWS)):
        pool_bd = pool_bd.at[gi * 64:(gi + 1) * 64, gi * 64:(gi + 1) * 64].set(pool_w[l, gi])
    return {
        "norm_w": norm_w[l].reshape(1, D_MODEL),
        "w_halo": w_halo, "w_main": w_main,
        "sgu_w": jnp.transpose(sgu_w[l], (1, 0, 2)).reshape(SGU_CHUNK, 4 * SGU_CHUNK).astype(BF16),
        "sgu_b": jnp.repeat(sgu_b[l].T, HEAD, axis=1),
        "conv_w": jnp.concatenate([conv_w[l], jnp.zeros((3, 768), F32)], axis=0),
        "dn_row": dn_row,
        "q_norm_w": jnp.tile(q_norm_w[l], 4).reshape(1, 256),
        "k_norm_w": jnp.tile(k_norm_w[l], 2).reshape(1, 128),
        "pool_w": pool_bd.astype(BF16),
        "pool_scale": pool_scale[l].reshape(1, 256),
        "dn_norm_w": jnp.tile(dn_norm_w[l], 4).reshape(1, 256),
        "score_bound": (HEAD * jnp.max(jnp.abs(q_norm_w[l])) * jnp.max(jnp.abs(k_norm_w[l]))
                        * (HEAD ** -0.5 * LOG2_E)),
        "w_out": w_out[l].astype(BF16),
    }


def _tiles(batch, seq_len):
    tb = min(DN_CHUNK * max(1, 8 // batch), seq_len)
    return dict(tm=min(512, seq_len), tb=tb, tq=min(256, seq_len))


def _layer(x, lw, rope_c, rope_s):
    t = _tiles(x.shape[0], x.shape[1])
    ya, yd, dq, dk, dv, bg, gates, aqt, ak, avt = _proj_call(x, lw, rope_c, rope_s, t["tm"], t["tq"])
    o_f, o_b = _dn_call(dq, dk, dv, bg, t["tb"])
    yc = lax.cond(lw["score_bound"] * SCORE_MARGIN <= SCORE_BOUND,
                  functools.partial(_attn_call, bounded=True),
                  functools.partial(_attn_call, bounded=False), aqt, ak, avt)
    return _out_call(x, ya, yd, yc, o_f, o_b, gates, lw, t["tm"])


def kernel(x_prompt, x_sample, norm_w, w_in, sgu_w, sgu_b, conv_w, a_log, dt_bias, dn_norm_w,
           q_norm_w, k_norm_w, pool_w, pool_scale, w_out):
    depth = norm_w.shape[0]
    rope_p = _rope_tables(x_prompt.shape[1])
    rope_s = _rope_tables(x_sample.shape[1])
    y_prompt, y_sample = x_prompt, x_sample
    for l in range(depth):
        lw = _layer_weights(l, norm_w, w_in, sgu_w, sgu_b, conv_w, a_log, dt_bias, dn_norm_w,
                            q_norm_w, k_norm_w, pool_w, pool_scale, w_out)
        y_prompt = _layer(y_prompt, lw, *rope_p)
        y_sample = _layer(y_sample, lw, *rope_s)
    return (y_prompt, y_sample)
```

```python
import functools

import jax
import jax.numpy as jnp
from jax import lax
from jax.experimental import pallas as pl
from jax.experimental.pallas import tpu as pltpu

F32 = jnp.float32
BF16 = jnp.bfloat16

D_MODEL = 1024
HEAD = 64
N_HEADS = 4
GRID_W = 64
EPS = 1e-6
SGU_CHUNK = 128
DN_CHUNK = 64
ROPE_THETA = 10000.0
POOL_WINDOWS = (2, 4, 8, 16)
HALO = 16
NEG_BIG = -1e30
LOG2_E = 1.4426950408889634
ATTN_TK = 512
SCORE_BOUND = 64.0
SCORE_MARGIN = 1.05

V7X_VMEM_LIMIT_BYTES = 56 * 1024 * 1024

_A_U = 0
_B_Q, _B_Z, _B_BETA = 768, 1536, 1792
_C_Q = 1808
_D_X, _D_Z = 2576, 2832
MAIN_COLS = 2176


def _cparams(semantics):
    return pltpu.CompilerParams(dimension_semantics=semantics,
                                vmem_limit_bytes=V7X_VMEM_LIMIT_BYTES)


def _split2(x):
    hi = x.astype(BF16)
    lo = (x - hi.astype(F32)).astype(BF16)
    return hi, lo


def _iota(shape, dim):
    return lax.broadcasted_iota(jnp.int32, shape, dim)


def _ones_where(cond):
    return jnp.where(cond, 1.0, 0.0).astype(BF16)


def _head_sum(x2):
    w = x2.shape[1]
    g = _ones_where(jnp.right_shift(_iota((w, w), 0), 6) == jnp.right_shift(_iota((w, w), 1), 6))
    return jnp.dot(x2.astype(BF16), g, preferred_element_type=F32)


def _silu(z):
    return 0.5 * z * (1.0 + jnp.tanh(0.5 * z))


def _bd_mask():
    return jnp.right_shift(_iota((256, 256), 0), 6) == jnp.right_shift(_iota((256, 256), 1), 6)


def _bd(x, mask01):
    z = jnp.zeros((HEAD, 128), x.dtype)
    blocks = []
    for h in range(N_HEADS):
        t = h // 2
        m = x[:, t * 128:(t + 1) * 128] * mask01[h * HEAD:(h + 1) * HEAD, t * 128:(t + 1) * 128]
        blocks.append(jnp.concatenate([m, z] if t == 0 else [z, m], axis=1))
    return jnp.concatenate(blocks, axis=0)


def _proj_kernel(xp_ref, xc_ref, xn_ref, nw_ref, wh_ref, wm_ref, sguw_ref, sgub_ref, convw_ref,
                 dnrow_ref, qw_ref, kw_ref, rc_ref, rs_ref, poolw_ref, pools_ref,
                 ya_ref, yd_ref, dq_ref, dk_ref, dv_ref, bg_ref, gates_ref, aqt_ref, ak_ref, avt_ref,
                 hext_ref, *, tm, tq, n_tiles, seq_len):
    i = pl.program_id(1)
    n_ext = tm + 2 * HALO
    nw = nw_ref[...]

    def norm(x):
        ms = jnp.mean(x * x, axis=-1, keepdims=True)
        return x * lax.rsqrt(ms + EPS) * nw

    hext_ref[0:HALO, :] = jnp.where(i > 0, norm(xp_ref[0]), 0.0).astype(BF16)
    hext_ref[HALO:HALO + tm, :] = norm(xc_ref[0]).astype(BF16)
    hext_ref[HALO + tm:n_ext, :] = jnp.where(i < n_tiles - 1, norm(xn_ref[0]), 0.0).astype(BF16)

    ph = jnp.dot(hext_ref[...], wh_ref[...], preferred_element_type=F32)
    pm = jnp.dot(hext_ref[HALO:HALO + tm, :], wm_ref[...], preferred_element_type=F32)

    def rows(x):
        return x[HALO:HALO + tm]

    a_u, a_v, a_z = pm[:, 0:256], pm[:, 256:512], pm[:, 512:768]
    vn = a_v * lax.rsqrt(_head_sum(a_v * a_v) * (1.0 / HEAD) + EPS)
    mask4 = _ones_where(jnp.right_shift(_iota((512, 256), 0), 7) == jnp.right_shift(_iota((512, 256), 1), 6))
    mixed = []
    for c in range(tm // SGU_CHUNK):
        vc = vn[c * SGU_CHUNK:(c + 1) * SGU_CHUNK].astype(BF16)
        bdv = jnp.concatenate([vc, vc, vc, vc], axis=0) * mask4
        mixed.append(jnp.dot(sguw_ref[...], bdv, preferred_element_type=F32) + sgub_ref[...])
    mixed = jnp.concatenate(mixed, axis=0)
    ya_ref[0] = (a_u * mixed * _silu(a_z)).astype(BF16)

    xd = ph[:, 768:1024]
    a1 = xd + pltpu.roll(xd, n_ext - 1, 0)
    a2 = a1 + pltpu.roll(a1, n_ext - 2, 0)
    a3 = a2 + pltpu.roll(a2, n_ext - 4, 0)
    a4 = a3 + pltpu.roll(a3, n_ext - 8, 0)
    w2 = rows(pltpu.roll(a1, 1, 0))
    w4 = rows(pltpu.roll(a2, 2, 0))
    w8 = rows(pltpu.roll(a3, 4, 0))
    w16 = rows(pltpu.roll(a4, 8, 0))
    grp = jnp.right_shift(_iota((tm, 256), 1), 6)
    half = jnp.left_shift(jnp.ones((tm, 256), jnp.int32), grp)
    t = i * tm + _iota((tm, 256), 0)
    cnt = (jnp.minimum(t + half, seq_len) - jnp.maximum(t - half, 0)).astype(F32)
    win = jnp.where(grp == 0, w2, jnp.where(grp == 1, w4, jnp.where(grp == 2, w8, w16)))
    diff = win / cnt - rows(xd)
    yd = jnp.dot(diff.astype(BF16), poolw_ref[...], preferred_element_type=F32) * pools_ref[...]
    yd_ref[0] = (yd * _silu(pm[:, 1792:2048])).astype(BF16)

    xb = ph[:, 0:768]
    cw = convw_ref[...]
    conv = (rows(pltpu.roll(xb, 2, 0)) * cw[0:1] + rows(pltpu.roll(xb, 1, 0)) * cw[1:2]
            + rows(xb) * cw[2:3] + rows(pltpu.roll(xb, n_ext - 1, 0)) * cw[3:4]
            + rows(pltpu.roll(xb, n_ext - 2, 0)) * cw[4:5])
    act = _silu(conv)
    bq, bk = act[:, 0:256], act[:, 256:512]
    dq_ref[0] = (bq * lax.rsqrt(_head_sum(bq * bq) + EPS) * (HEAD ** -0.5)).astype(BF16)
    dk_ref[0] = (bk * lax.rsqrt(_head_sum(bk * bk) + EPS)).astype(BF16)
    dv_ref[0] = act[:, 512:768].astype(BF16)
    ba = pm[:, 2048:2176]
    lane = _iota((tm, 128), 1)
    xa = ba + dnrow_ref[1:2, :]
    softplus = jnp.maximum(xa, 0.0) + jnp.log1p(jnp.exp(-jnp.abs(xa)))
    g = -jnp.exp(dnrow_ref[0:1, :]) * softplus
    bg_ref[0] = jnp.where(lane < 8, 1.0 / (1.0 + jnp.exp(-ba)), g)
    gates_ref[0] = jnp.concatenate([_silu(pm[:, 768:1024]), _silu(pm[:, 1536:1792])], axis=1).astype(BF16)

    rc, rs = rc_ref[...], rs_ref[...]
    first = (_iota((tm, 128), 1) & 16) == 0

    def rope(x):
        sw = jnp.where(first, pltpu.roll(x, 112, 1), pltpu.roll(x, 16, 1))
        return x * rc + sw * rs

    cq, ck = pm[:, 1024:1280], pm[:, 1280:1408]
    qn = cq * lax.rsqrt(_head_sum(cq * cq) * (1.0 / HEAD) + EPS) * qw_ref[...]
    qr = jnp.concatenate([rope(qn[:, 0:128]), rope(qn[:, 128:256])], axis=1) * (HEAD ** -0.5 * LOG2_E)
    qt = qr.T.astype(BF16)
    zq = jnp.zeros((HEAD, 2 * tq), BF16)
    for r in range(tm // tq):
        cs = slice(r * tq, (r + 1) * tq)
        top0 = jnp.concatenate([qt[0:64, cs], qt[64:128, cs]], axis=1)
        top1 = jnp.concatenate([qt[128:192, cs], qt[192:256, cs]], axis=1)
        aqt_ref[0, 0, r] = jnp.concatenate([top0, zq], axis=0)
        aqt_ref[0, 1, r] = jnp.concatenate([zq, top1], axis=0)
    kn = ck * lax.rsqrt(_head_sum(ck * ck) * (1.0 / HEAD) + EPS) * kw_ref[...]
    ak_ref[0] = rope(kn).astype(BF16)
    vt = pm[:, 1408:1536].T
    ones = jnp.ones((16, tm), F32)
    for g in range(2):
        vg = jnp.concatenate([vt[g * HEAD:(g + 1) * HEAD], ones], axis=0).astype(BF16)
        for r in range(tm // ATTN_TK):
            avt_ref[0, g, r] = vg[:, r * ATTN_TK:(r + 1) * ATTN_TK]


def _proj_call(x, lw, rope_c, rope_s, tm, tq):
    b, s, _ = x.shape
    n_tiles = s // tm
    hb = tm // HALO
    last_hb = s // HALO - 1
    full = lambda shape: pl.BlockSpec(shape, lambda bi, i: (0,) * len(shape))
    tok = lambda w: pl.BlockSpec((1, tm, w), lambda bi, i: (bi, i, 0))
    in_specs = [
        pl.BlockSpec((1, HALO, D_MODEL), lambda bi, i: (bi, jnp.maximum(i * hb - 1, 0), 0)),
        pl.BlockSpec((1, tm, D_MODEL), lambda bi, i: (bi, i, 0)),
        pl.BlockSpec((1, HALO, D_MODEL), lambda bi, i: (bi, jnp.minimum((i + 1) * hb, last_hb), 0)),
        full((1, D_MODEL)), full((D_MODEL, 1024)), full((D_MODEL, MAIN_COLS)),
        full((128, 512)), full((128, 256)), full((8, 768)), full((8, 128)),
        full((1, 256)), full((1, 128)),
        pl.BlockSpec((tm, 128), lambda bi, i: (i, 0)), pl.BlockSpec((tm, 128), lambda bi, i: (i, 0)),
        full((256, 256)), full((1, 256)),
    ]
    out_shape = [
        jax.ShapeDtypeStruct((b, s, 256), BF16),
        jax.ShapeDtypeStruct((b, s, 256), BF16),
        jax.ShapeDtypeStruct((b, s, 256), BF16),
        jax.ShapeDtypeStruct((b, s, 256), BF16),
        jax.ShapeDtypeStruct((b, s, 256), BF16),
        jax.ShapeDtypeStruct((b, s, 128), F32),
        jax.ShapeDtypeStruct((b, s, 512), BF16),
        jax.ShapeDtypeStruct((b, 2, s // tq, 128, 2 * tq), BF16),
        jax.ShapeDtypeStruct((b, s, 128), BF16),
        jax.ShapeDtypeStruct((b, 2, s // ATTN_TK, HEAD + 16, ATTN_TK), BF16),
    ]
    out_specs = [tok(256), tok(256), tok(256), tok(256), tok(256), tok(128), tok(512),
                 pl.BlockSpec((1, 2, tm // tq, 128, 2 * tq), lambda bi, i: (bi, 0, i, 0, 0)), tok(128),
                 pl.BlockSpec((1, 2, tm // ATTN_TK, HEAD + 16, ATTN_TK), lambda bi, i: (bi, 0, i, 0, 0))]
    return pl.pallas_call(
        functools.partial(_proj_kernel, tm=tm, tq=tq, n_tiles=n_tiles, seq_len=s),
        grid=(b, n_tiles), in_specs=in_specs, out_specs=out_specs, out_shape=out_shape,
        scratch_shapes=[pltpu.VMEM((tm + 2 * HALO, D_MODEL), BF16)],
        compiler_params=_cparams(("parallel", "parallel")), name="proj",
    )(x, x, x, lw["norm_w"], lw["w_halo"], lw["w_main"], lw["sgu_w"], lw["sgu_b"], lw["conv_w"],
      lw["dn_row"], lw["q_norm_w"], lw["k_norm_w"], rope_c, rope_s, lw["pool_w"], lw["pool_scale"])


def _dn_kernel(qf_ref, kf_ref, vf_ref, bgf_ref, qb_ref, kb_ref, vb_ref, bgb_ref,
               of_ref, ob_ref, state_ref, expand_ref, ones_ref, bd_ref, *, nb, tb):
    nc = tb // DN_CHUNK

    @pl.when(pl.program_id(0) == 0)
    def _():
        state_ref[...] = jnp.zeros_like(state_ref)
        r = _iota((tb, tb), 0)
        c = _iota((tb, tb), 1)
        same = jnp.right_shift(r, 6) == jnp.right_shift(c, 6)
        src = _iota((128, 768), 0)
        col = _iota((128, 768), 1)
        for d in range(2):
            expand_ref[d] = _ones_where((src & 63) == jnp.left_shift(jnp.right_shift(col, 8), 3) + 4 * d
                                        + (jnp.right_shift(col, 6) & 3))
        ones_ref[...] = jnp.concatenate([_ones_where(same), _ones_where(same)], axis=1)
        bd_ref[...] = _ones_where(_bd_mask())

    li = _iota((tb, 256), 0) & 63
    lj = _iota((tb, 256), 1) & 63
    eyecat = li == lj
    eyef = jnp.where(eyecat, 1.0, 0.0)
    j2 = ones_ref[...]
    bdm = _bd_mask()
    bdm01 = bd_ref[...]
    lane = _iota((tb, 128), 1)
    zpad = jnp.zeros((DN_CHUNK, 256), BF16)

    dir_consts = []
    for d in range(2):
        dir_consts.append(dict(expand=expand_ref[d],
                               incl=(li >= lj) if d == 0 else (li <= lj),
                               strict=(li > lj) if d == 0 else (li < lj)))
    blocks = []
    for d in range(2):
        q_ref, k_ref, v_ref, bg_ref = ((qf_ref, kf_ref, vf_ref, bgf_ref), (qb_ref, kb_ref, vb_ref, bgb_ref))[d]
        for bi in range(nb):
            blocks.append(dict(d=d, bi=bi, q_ref=q_ref, k_ref=k_ref, v_ref=v_ref, bg=bg_ref[bi], **dir_consts[d]))
    row64 = _iota((tb, 128), 0) & 63
    for blk in blocks:
        c = blk["bg"]
        for s in (1, 2, 4, 8, 16, 32):
            c = c + jnp.where(row64 >= s, pltpu.roll(c, s, 0), 0.0)
        tot = jnp.concatenate([jnp.broadcast_to(c[ci * DN_CHUNK + DN_CHUNK - 1:(ci + 1) * DN_CHUNK, :],
                                                (DN_CHUNK, 128)) for ci in range(nc)], axis=0)
        if blk["d"] == 0:
            blk["cs"] = (c, tot - c)
        else:
            blk["cs"] = (tot - c + blk["bg"], c - blk["bg"])
    for blk in blocks:
        cs_in, cs_st = blk["cs"]
        nbx = jnp.where(lane < 8, blk["bg"], jnp.where(lane < 16, cs_in, pltpu.roll(cs_st, 8, 1)))
        hi = nbx.astype(BF16)
        lo = pltpu.roll(nbx - hi.astype(F32), 64, 1)
        packed = jnp.where(lane < 64, hi.astype(F32), lo).astype(BF16)
        blk["x"] = jnp.dot(packed, blk["expand"], preferred_element_type=F32)
    for blk in blocks:
        gcx = blk["x"][:, 256:512]
        blk["rowf"] = jnp.dot(j2, jnp.concatenate(_split2(jnp.where(eyecat, gcx, 0.0)), axis=0),
                              preferred_element_type=F32)
    for blk in blocks:
        gram, qk = [], []
        for ci in range(nc):
            sl = slice(ci * DN_CHUNK, (ci + 1) * DN_CHUNK)
            kc = blk["k_ref"][blk["bi"], sl, :]
            gq = lax.dot_general(jnp.concatenate([kc, blk["q_ref"][blk["bi"], sl, :]], axis=0), _bd(kc, bdm01),
                                 (((1,), (1,)), ((), ())), preferred_element_type=F32)
            gram.append(gq[0:64])
            qk.append(gq[64:128])
        blk["gram"] = jnp.concatenate(gram, axis=0)
        blk["qk"] = jnp.concatenate(qk, axis=0)
    chains = {}
    for blk in blocks:
        d, bi = blk["d"], blk["bi"]
        q = blk["q_ref"][bi].astype(F32)
        k = blk["k_ref"][bi].astype(F32)
        v = blk["v_ref"][bi].astype(F32)
        x = blk["x"]
        bx, gcx, dglx = x[:, 0:256], x[:, 256:512], x[:, 512:768]
        decay = jnp.exp(jnp.where(blk["incl"], gcx - blk["rowf"], NEG_BIG))
        eg = jnp.exp(gcx)
        a_all = jnp.where(blk["strict"], bx * blk["gram"] * decay, 0.0)
        p_all = eyef - a_all
        intra = jnp.where(blk["incl"], blk["qk"] * decay, 0.0).astype(BF16)
        qd = (q * eg).astype(BF16)
        kdt = (k * jnp.exp(dglx)).T.astype(BF16)
        vbeta = (v * bx).astype(BF16)
        kbg = (k * bx * eg).astype(BF16)
        for ci in range(nc):
            sl = slice(ci * DN_CHUNK, (ci + 1) * DN_CHUNK)
            last = ci * DN_CHUNK + (DN_CHUNK - 1 if d == 0 else 0)
            pair = ci // 2
            chains[(d, bi, ci)] = dict(
                ak=a_all[sl], p=p_all[sl], vb=vbeta[sl], kbg=kbg[sl], intra=intra[sl], qd=qd[sl],
                kpair=kdt[:, pair * 128:(pair + 1) * 128], egl=eg[last:last + 1, :])

    def prepare(group, phase):
        for ch in group:
            if phase < 5:
                akb = ch["ak"].astype(BF16)
                lhs = akb if phase == 0 else jnp.concatenate([akb, ch["p"].astype(BF16)], axis=0)
                res = jnp.dot(lhs, _bd(akb, bdm01), preferred_element_type=F32)
                ch["ak"] = res[0:64]
                if phase > 0:
                    ch["p"] = ch["p"] + res[64:128]
            elif phase == 5:
                ch["p"] = ch["p"] + jnp.dot(ch["p"].astype(BF16), _bd(ch["ak"].astype(BF16), bdm01),
                                            preferred_element_type=F32)
            else:
                rhs = jnp.concatenate([_bd(ch["vb"], bdm01), _bd(ch["kbg"], bdm01)], axis=1)
                uw = jnp.dot(ch["p"].astype(BF16), rhs, preferred_element_type=F32)
                ch["u"] = uw[:, 0:256]
                ch["w"] = uw[:, 256:512].astype(BF16)

    def recur_a(group):
        for ch in group:
            ch["state"] = state_ref[ch["si"]]
            res = jnp.dot(jnp.concatenate([ch["w"], ch["qd"]], axis=0), ch["state"].astype(BF16),
                          preferred_element_type=F32)
            ch["v_new"] = (ch["u"] - res[0:64]).astype(BF16)
            ch["o_inter"] = res[64:128]

    def recur_b(group):
        for ch in group:
            v_new = ch["v_new"]
            ch["o_ref"][ch["bi"], ch["sl"], :] = (ch["o_inter"] + jnp.dot(
                ch["intra"], _bd(v_new, bdm01), preferred_element_type=F32)).astype(BF16)
            vpad = jnp.concatenate([v_new, zpad] if ch["even"] else [zpad, v_new], axis=0)
            ds = jnp.dot(ch["kpair"], vpad, preferred_element_type=F32)
            state_ref[ch["si"]] = ch["state"] * ch["egl"] + jnp.where(bdm, ds, 0.0)

    groups = []
    for step in range(nc):
        group = []
        for bi in range(nb):
            for d in range(2):
                ci = step if d == 0 else nc - 1 - step
                ch = chains[(d, bi, ci)]
                ch.update(si=d * nb + bi, bi=bi, o_ref=(of_ref, ob_ref)[d], even=ci % 2 == 0,
                          sl=slice(ci * DN_CHUNK, (ci + 1) * DN_CHUNK))
                group.append(ch)
        groups.append(group)

    for phase in range(7):
        prepare(groups[0], phase)
    for step in range(nc):
        nxt = groups[step + 1] if step + 1 < nc else []
        prepare(nxt, 0)
        prepare(nxt, 1)
        recur_a(groups[step])
        prepare(nxt, 2)
        prepare(nxt, 3)
        recur_b(groups[step])
        prepare(nxt, 4)
        prepare(nxt, 5)
        prepare(nxt, 6)


def _dn_call(dq, dk, dv, bg, tb):
    b, s, _ = dq.shape
    n = s // tb
    fwd = lambda w: pl.BlockSpec((b, tb, w), lambda i: (0, i, 0))
    bwd = lambda w: pl.BlockSpec((b, tb, w), lambda i: (0, n - 1 - i, 0))
    return pl.pallas_call(
        functools.partial(_dn_kernel, nb=b, tb=tb),
        grid=(n,),
        in_specs=[fwd(256), fwd(256), fwd(256), fwd(128), bwd(256), bwd(256), bwd(256), bwd(128)],
        out_specs=[fwd(256), bwd(256)],
        out_shape=[jax.ShapeDtypeStruct((b, s, 256), BF16)] * 2,
        scratch_shapes=[pltpu.VMEM((2 * b, 256, 256), F32),
                        pltpu.VMEM((2, 128, 768), BF16), pltpu.VMEM((tb, 2 * tb), BF16),
                        pltpu.VMEM((256, 256), BF16)],
        compiler_params=_cparams(("arbitrary",)), name="dn",
    )(dq, dk, dv, bg, dq, dk, dv, bg)


def _dn2_kernel(*refs, nb, tb):
    ins, (of_ref, ob_ref) = refs[:24], refs[24:26]
    (state_ref, expand_ref, ones_ref, bd_ref,
     a_sc, vb_sc, kbg_sc, in_sc, qd_sc, kdt_sc, egl_sc) = refs[26:]
    nc = tb // DN_CHUNK
    step = pl.program_id(0)

    li = _iota((tb, 256), 0) & 63
    lj = _iota((tb, 256), 1) & 63
    eyecat = li == lj
    eyef = jnp.where(eyecat, 1.0, 0.0)
    bdm = _bd_mask()
    lane = _iota((tb, 128), 1)
    row64 = _iota((tb, 128), 0) & 63
    zpad = jnp.zeros((DN_CHUNK, 256), BF16)

    def prepare_block_set(which, slot):
        j2 = ones_ref[...]
        bdm01 = bd_ref[...]
        blocks = []
        for d in range(2):
            q_ref, k_ref, v_ref, bg_ref = ins[(which * 2 + d) * 4:(which * 2 + d) * 4 + 4]
            for bi in range(nb):
                blocks.append(dict(d=d, bi=bi, idx=d * nb + bi, q_ref=q_ref, k_ref=k_ref, v_ref=v_ref,
                                   bg=bg_ref[bi], expand=expand_ref[d],
                                   incl=(li >= lj) if d == 0 else (li <= lj),
                                   strict=(li > lj) if d == 0 else (li < lj)))
        for blk in blocks:
            c = blk["bg"]
            for s in (1, 2, 4, 8, 16, 32):
                c = c + jnp.where(row64 >= s, pltpu.roll(c, s, 0), 0.0)
            tot = jnp.concatenate([jnp.broadcast_to(c[ci * DN_CHUNK + DN_CHUNK - 1:(ci + 1) * DN_CHUNK, :],
                                                    (DN_CHUNK, 128)) for ci in range(nc)], axis=0)
            if blk["d"] == 0:
                blk["cs"] = (c, tot - c)
            else:
                blk["cs"] = (tot - c + blk["bg"], c - blk["bg"])
            yield
        for blk in blocks:
            cs_in, cs_st = blk["cs"]
            nbx = jnp.where(lane < 8, blk["bg"], jnp.where(lane < 16, cs_in, pltpu.roll(cs_st, 8, 1)))
            hi = nbx.astype(BF16)
            lo = pltpu.roll(nbx - hi.astype(F32), 64, 1)
            packed = jnp.where(lane < 64, hi.astype(F32), lo).astype(BF16)
            blk["x"] = jnp.dot(packed, blk["expand"], preferred_element_type=F32)
            yield
        for blk in blocks:
            gcx = blk["x"][:, 256:512]
            blk["rowf"] = jnp.dot(j2, jnp.concatenate(_split2(jnp.where(eyecat, gcx, 0.0)), axis=0),
                                  preferred_element_type=F32)
            yield
        for blk in blocks:
            gram, qk = [], []
            for ci in range(nc):
                sl = slice(ci * DN_CHUNK, (ci + 1) * DN_CHUNK)
                kc = blk["k_ref"][blk["bi"], sl, :]
                gq = lax.dot_general(jnp.concatenate([kc, blk["q_ref"][blk["bi"], sl, :]], axis=0),
                                     _bd(kc, bdm01), (((1,), (1,)), ((), ())),
                                     preferred_element_type=F32)
                gram.append(gq[0:64])
                qk.append(gq[64:128])
            blk["gram"] = jnp.concatenate(gram, axis=0)
            blk["qk"] = jnp.concatenate(qk, axis=0)
            yield
        for blk in blocks:
            d, bi, idx = blk["d"], blk["bi"], blk["idx"]
            q = blk["q_ref"][bi].astype(F32)
            k = blk["k_ref"][bi].astype(F32)
            v = blk["v_ref"][bi].astype(F32)
            x = blk["x"]
            bx, gcx, dglx = x[:, 0:256], x[:, 256:512], x[:, 512:768]
            decay = jnp.exp(jnp.where(blk["incl"], gcx - blk["rowf"], NEG_BIG))
            eg = jnp.exp(gcx)
            a_sc[slot, idx] = jnp.where(blk["strict"], bx * blk["gram"] * decay, 0.0)
            in_sc[slot, idx] = jnp.where(blk["incl"], blk["qk"] * decay, 0.0).astype(BF16)
            qd_sc[slot, idx] = (q * eg).astype(BF16)
            kdt_sc[slot, idx] = (k * jnp.exp(dglx)).T.astype(BF16)
            vb_sc[slot, idx] = (v * bx).astype(BF16)
            kbg_sc[slot, idx] = (k * bx * eg).astype(BF16)
            for ci in range(nc):
                last = ci * DN_CHUNK + (DN_CHUNK - 1 if d == 0 else 0)
                egl_sc[slot, idx, ci * 8:(ci + 1) * 8, :] = jnp.broadcast_to(eg[last:last + 1, :], (8, 256))
            yield

    def matmul_part(slot, half):
        bdm01 = bd_ref[...]

        def prepare(group, phase):
            for ch in group:
                idx, sl = ch["idx"], ch["sl"]
                if phase == 0:
                    ch["ak"] = a_sc[slot, idx, sl, :]
                    ch["p"] = eyef[sl] - ch["ak"]
                if phase < 5:
                    akb = ch["ak"].astype(BF16)
                    lhs = akb if phase == 0 else jnp.concatenate([akb, ch["p"].astype(BF16)], axis=0)
                    res = jnp.dot(lhs, _bd(akb, bdm01), preferred_element_type=F32)
                    ch["ak"] = res[0:64]
                    if phase > 0:
                        ch["p"] = ch["p"] + res[64:128]
                elif phase == 5:
                    ch["p"] = ch["p"] + jnp.dot(ch["p"].astype(BF16), _bd(ch["ak"].astype(BF16), bdm01),
                                                preferred_element_type=F32)
                else:
                    rhs = jnp.concatenate([_bd(vb_sc[slot, idx, sl, :], bdm01),
                                           _bd(kbg_sc[slot, idx, sl, :], bdm01)], axis=1)
                    uw = jnp.dot(ch["p"].astype(BF16), rhs, preferred_element_type=F32)
                    ch["u"] = uw[:, 0:256]
                    ch["w"] = uw[:, 256:512].astype(BF16)

        def recur_a(group):
            for ch in group:
                ch["state"] = state_ref[ch["si"]]
                res = jnp.dot(jnp.concatenate([ch["w"], qd_sc[slot, ch["idx"], ch["sl"], :]], axis=0),
                              ch["state"].astype(BF16), preferred_element_type=F32)
                ch["v_new"] = (ch["u"] - res[0:64]).astype(BF16)
                ch["o_inter"] = res[64:128]

        def recur_b(group):
            for ch in group:
                v_new, idx, ci = ch["v_new"], ch["idx"], ch["ci"]
                ch["o_ref"][ch["bi"], ch["osl"], :] = (ch["o_inter"] + jnp.dot(
                    in_sc[slot, idx, ch["sl"], :], _bd(v_new, bdm01), preferred_element_type=F32)).astype(BF16)
                vpad = jnp.concatenate([v_new, zpad] if ci % 2 == 0 else [zpad, v_new], axis=0)
                kpair = kdt_sc[slot, idx, :, (ci // 2) * 128:(ci // 2 + 1) * 128]
                ds = jnp.dot(kpair, vpad, preferred_element_type=F32)
                egl = egl_sc[slot, idx, ci * 8:ci * 8 + 1, :]
                state_ref[ch["si"]] = ch["state"] * egl + jnp.where(bdm, ds, 0.0)

        groups = []
        for s in range(nc):
            group = []
            for bi in range(nb):
                for d in range(2):
                    ci = s if d == 0 else nc - 1 - s
                    row0 = (half if d == 0 else 1 - half) * tb + ci * DN_CHUNK
                    group.append(dict(idx=d * nb + bi, si=d * nb + bi, bi=bi, ci=ci,
                                      o_ref=(of_ref, ob_ref)[d],
                                      sl=slice(ci * DN_CHUNK, (ci + 1) * DN_CHUNK),
                                      osl=slice(row0, row0 + DN_CHUNK)))
            groups.append(group)
        for phase in range(7):
            prepare(groups[0], phase)
            yield
        for s in range(nc):
            nxt = groups[s + 1] if s + 1 < nc else []
            for piece in ((prepare, nxt, 0), (prepare, nxt, 1), (recur_a, groups[s]), (prepare, nxt, 2),
                          (prepare, nxt, 3), (recur_b, groups[s]), (prepare, nxt, 4), (prepare, nxt, 5),
                          (prepare, nxt, 6)):
                piece[0](*piece[1:])
                yield

    def interleave(first, second):
        streams = [first, second]
        while streams:
            for gen in list(streams):
                if next(gen, StopIteration) is StopIteration:
                    streams.remove(gen)

    @pl.when(step == 0)
    def _():
        state_ref[...] = jnp.zeros_like(state_ref)
        r = _iota((tb, tb), 0)
        c = _iota((tb, tb), 1)
        same = jnp.right_shift(r, 6) == jnp.right_shift(c, 6)
        src = _iota((128, 768), 0)
        col = _iota((128, 768), 1)
        for d in range(2):
            expand_ref[d] = _ones_where((src & 63) == jnp.left_shift(jnp.right_shift(col, 8), 3) + 4 * d
                                        + (jnp.right_shift(col, 6) & 3))
        ones_ref[...] = jnp.concatenate([_ones_where(same), _ones_where(same)], axis=1)
        bd_ref[...] = _ones_where(_bd_mask())
        for _ in prepare_block_set(0, 0):
            pass

    interleave(matmul_part(0, 0), prepare_block_set(1, 1))
    interleave(matmul_part(1, 1), prepare_block_set(2, 0))


def _dn2_call(dq, dk, dv, bg, tb):
    b, s, _ = dq.shape
    n = s // tb
    n2 = n // 2

    def spec(w, index):
        return pl.BlockSpec((b, tb, w), lambda i: (0, index(i), 0))

    orders = [lambda j: j, lambda j: n - 1 - j]
    picks = [lambda i: 0, lambda i: 2 * i + 1, lambda i: jnp.minimum(2 * i + 2, n - 1)]
    in_specs, args = [], []
    for pick in picks:
        for order in orders:
            index = functools.partial(lambda i, pick, order: order(pick(i)), pick=pick, order=order)
            in_specs += [spec(256, index), spec(256, index), spec(256, index), spec(128, index)]
            args += [dq, dk, dv, bg]
    nblk = 2 * b
    return pl.pallas_call(
        functools.partial(_dn2_kernel, nb=b, tb=tb),
        grid=(n2,), in_specs=in_specs,
        out_specs=[pl.BlockSpec((b, 2 * tb, 256), lambda i: (0, i, 0)),
                   pl.BlockSpec((b, 2 * tb, 256), lambda i: (0, n2 - 1 - i, 0))],
        out_shape=[jax.ShapeDtypeStruct((b, s, 256), BF16)] * 2,
        scratch_shapes=[pltpu.VMEM((nblk, 256, 256), F32), pltpu.VMEM((2, 128, 768), BF16),
                        pltpu.VMEM((tb, 2 * tb), BF16), pltpu.VMEM((256, 256), BF16),
                        pltpu.VMEM((2, nblk, tb, 256), F32), pltpu.VMEM((2, nblk, tb, 256), BF16),
                        pltpu.VMEM((2, nblk, tb, 256), BF16), pltpu.VMEM((2, nblk, tb, 256), BF16),
                        pltpu.VMEM((2, nblk, tb, 256), BF16), pltpu.VMEM((2, nblk, 256, tb), BF16),
                        pltpu.VMEM((2, nblk, tb // 8, 256), F32)],
        compiler_params=_cparams(("arbitrary",)), name="dn",
    )(*args)


def _attn_kernel(qt_ref, k_ref, vt_ref, o_ref, st_ref, *, tq, tk, n_q, n_kv, unroll, bounded):
    per_q = n_kv // unroll
    n_trips = n_q * per_q

    def scores(qi, j, slot):
        kj = k_ref[0, pl.ds(pl.multiple_of(j * tk, tk), tk), :]
        st = jnp.dot(kj, qt_ref[0, 0, qi], preferred_element_type=F32)
        st_ref[slot, :, 0:2 * tq] = st
        return None if bounded else jnp.max(st, axis=0, keepdims=True)

    def update(j, st, mx, m, acc):
        if bounded:
            return m, acc + jnp.dot(vt_ref[0, 0, j], jnp.exp2(st).astype(BF16), preferred_element_type=F32)
        m_new = jnp.maximum(m, mx)
        p = jnp.exp2(st - m_new).astype(BF16)
        alpha = jnp.exp2(m - m_new)
        return m_new, alpha * acc + jnp.dot(vt_ref[0, 0, j], p, preferred_element_type=F32)

    def body(t, carry):
        qi = t // per_q
        base = (t - qi * per_q) * unroll
        first = base == 0
        m = jnp.where(first, -jnp.inf, carry[0])
        acc = jnp.where(first, 0.0, carry[1])
        mx = carry[2]
        t_next = jnp.minimum(t + 1, n_trips - 1)
        qi_next = t_next // per_q
        base_next = (t_next - qi_next * per_q) * unroll
        for r in range(unroll):
            if r < unroll - 1:
                mx_next = scores(qi, base + r + 1, (r + 1) % 2)
            else:
                mx_next = scores(qi_next, base_next, 0)
            m, acc = update(base + r, st_ref[r % 2, :, 0:2 * tq], mx, m, acc)
            mx = mx if bounded else mx_next

        @pl.when(base == n_kv - unroll)
        def _():
            o = acc[0:HEAD] / acc[HEAD:HEAD + 1]
            ot = jnp.concatenate([o[:, 0:tq], o[:, tq:2 * tq]], axis=0)
            o_ref[0, pl.ds(pl.multiple_of(qi * tq, tq), tq), :] = ot.T.astype(BF16)

        return m, acc, mx

    m0 = jnp.full((1, 2 * tq), -jnp.inf, F32)
    a0 = jnp.zeros((HEAD + 16, 2 * tq), F32)
    mx0 = scores(0, 0, 0)
    lax.fori_loop(0, n_trips, body, (m0, a0, m0 if bounded else mx0))


def _attn_call(aqt, ak, avt, bounded):
    b, s, _ = ak.shape
    _, _, n_q, _, tq2 = aqt.shape
    _, _, n_kv, vrows, tk = avt.shape
    unroll = next(u for u in (32, 16, 8, 4, 2) if n_kv % u == 0)
    return pl.pallas_call(
        functools.partial(_attn_kernel, tq=tq2 // 2, tk=tk, n_q=n_q, n_kv=n_kv, unroll=unroll,
                          bounded=bounded),
        grid=(b, 2),
        in_specs=[pl.BlockSpec((1, 1, n_q, 128, tq2), lambda bi, g: (bi, g, 0, 0, 0)),
                  pl.BlockSpec((1, s, 128), lambda bi, g: (bi, 0, 0)),
                  pl.BlockSpec((1, 1, n_kv, vrows, tk), lambda bi, g: (bi, g, 0, 0, 0))],
        out_specs=pl.BlockSpec((1, s, 128), lambda bi, g: (bi, 0, g)),
        out_shape=jax.ShapeDtypeStruct((b, s, 256), BF16),
        scratch_shapes=[pltpu.VMEM((2, tk, tq2 + 128), F32)],
        compiler_params=_cparams(("parallel", "parallel")), name="attn",
    )(aqt, ak, avt)


def _out_kernel(x_ref, ya_ref, yd_ref, yc_ref, of_ref, ob_ref, gates_ref, dnw_ref, wo_ref, o_ref):
    o = of_ref[0].astype(F32) + ob_ref[0].astype(F32)
    gates = gates_ref[0].astype(F32)
    on = o * lax.rsqrt(_head_sum(o * o) * (1.0 / HEAD) + EPS) * dnw_ref[...]
    yb = (on * gates[:, 0:256]).astype(BF16)
    yc = (yc_ref[0].astype(F32) * gates[:, 256:512]).astype(BF16)
    mix = jnp.concatenate([ya_ref[0], yb, yc, yd_ref[0]], axis=1)
    o_ref[0] = x_ref[0] + jnp.dot(mix, wo_ref[...], preferred_element_type=F32)


def _out_call(x, ya, yd, yc, o_f, o_b, gates, lw, tm):
    b, s, _ = x.shape
    tok = lambda w: pl.BlockSpec((1, tm, w), lambda bi, i: (bi, i, 0))
    full = lambda shape: pl.BlockSpec(shape, lambda bi, i: (0,) * len(shape))
    return pl.pallas_call(
        _out_kernel, grid=(b, s // tm),
        in_specs=[tok(D_MODEL), tok(256), tok(256), tok(256), tok(256), tok(256), tok(512),
                  full((1, 256)), full((D_MODEL, D_MODEL))],
        out_specs=tok(D_MODEL), out_shape=jax.ShapeDtypeStruct((b, s, D_MODEL), F32),
        compiler_params=_cparams(("parallel", "parallel")), name="out",
    )(x, ya, yd, yc, o_f, o_b, gates, lw["dn_norm_w"], lw["w_out"])


def _rope_tables(seq_len):
    t = jnp.arange(seq_len)
    pos = jnp.stack([t // GRID_W, t % GRID_W], axis=-1).astype(F32)
    n_freq = HEAD // 4
    inv_freq = jnp.power(ROPE_THETA, -2.0 * jnp.arange(n_freq, dtype=F32) / (HEAD // 2))
    ang = pos[:, :, None] * inv_freq
    cos = jnp.repeat(jnp.cos(ang)[:, :, None, :], 2, axis=2).reshape(seq_len, HEAD)
    sin = jnp.sin(ang)
    sin = jnp.stack([-sin, sin], axis=2).reshape(seq_len, HEAD)
    return jnp.tile(cos, (1, 2)), jnp.tile(sin, (1, 2))


def _layer_weights(l, norm_w, w_in, sgu_w, sgu_b, conv_w, a_log, dt_bias, dn_norm_w, q_norm_w,
                   k_norm_w, pool_w, pool_scale, w_out):
    w = w_in[l]
    cols = lambda a, n: w[:, a:a + n]
    w_halo = jnp.concatenate([cols(_B_Q, 768), cols(_D_X, 256)], axis=1).astype(BF16)
    w_main = jnp.concatenate([
        cols(_A_U, 768), cols(_B_Z, 256), cols(_C_Q, 768), cols(_D_Z, 256), cols(_B_BETA, 16),
        jnp.zeros((D_MODEL, 112), F32)], axis=1).astype(BF16)
    dn_row = jnp.zeros((8, 128), F32)
    dn_row = dn_row.at[0, 8:16].set(a_log[l].reshape(8)).at[1, 8:16].set(dt_bias[l].reshape(8))
    pool_bd = jnp.zeros((256, 256), F32)
    for gi in range(len(POOL_WINDOWS)):
        pool_bd = pool_bd.at[gi * 64:(gi + 1) * 64, gi * 64:(gi + 1) * 64].set(pool_w[l, gi])
    return {
        "norm_w": norm_w[l].reshape(1, D_MODEL),
        "w_halo": w_halo, "w_main": w_main,
        "sgu_w": jnp.transpose(sgu_w[l], (1, 0, 2)).reshape(SGU_CHUNK, 4 * SGU_CHUNK).astype(BF16),
        "sgu_b": jnp.repeat(sgu_b[l].T, HEAD, axis=1),
        "conv_w": jnp.concatenate([conv_w[l], jnp.zeros((3, 768), F32)], axis=0),
        "dn_row": dn_row,
        "q_norm_w": jnp.tile(q_norm_w[l], 4).reshape(1, 256),
        "k_norm_w": jnp.tile(k_norm_w[l], 2).reshape(1, 128),
        "pool_w": pool_bd.astype(BF16),
        "pool_scale": pool_scale[l].reshape(1, 256),
        "dn_norm_w": jnp.tile(dn_norm_w[l], 4).reshape(1, 256),
        "score_bound": (HEAD * jnp.max(jnp.abs(q_norm_w[l])) * jnp.max(jnp.abs(k_norm_w[l]))
                        * (HEAD ** -0.5 * LOG2_E)),
        "w_out": w_out[l].astype(BF16),
    }


def _tiles(batch, seq_len):
    tb = min(DN_CHUNK * max(1, 8 // batch), seq_len)
    return dict(tm=min(512, seq_len), tb=tb, tq=min(256, seq_len))


def _layer(x, lw, rope_c, rope_s):
    t = _tiles(x.shape[0], x.shape[1])
    ya, yd, dq, dk, dv, bg, gates, aqt, ak, avt = _proj_call(x, lw, rope_c, rope_s, t["tm"], t["tq"])
    o_f, o_b = _dn2_call(dq, dk, dv, bg, t["tb"])
    yc = lax.cond(lw["score_bound"] * SCORE_MARGIN <= SCORE_BOUND,
                  functools.partial(_attn_call, bounded=True),
                  functools.partial(_attn_call, bounded=False), aqt, ak, avt)
    return _out_call(x, ya, yd, yc, o_f, o_b, gates, lw, t["tm"])


def kernel(x_prompt, x_sample, norm_w, w_in, sgu_w, sgu_b, conv_w, a_log, dt_bias, dn_norm_w,
           q_norm_w, k_norm_w, pool_w, pool_scale, w_out):
    depth = norm_w.shape[0]
    rope_p = _rope_tables(x_prompt.shape[1])
    rope_s = _rope_tables(x_sample.shape[1])
    y_prompt, y_sample = x_prompt, x_sample
    for l in range(depth):
        lw = _layer_weights(l, norm_w, w_in, sgu_w, sgu_b, conv_w, a_log, dt_bias, dn_norm_w,
                            q_norm_w, k_norm_w, pool_w, pool_scale, w_out)
        y_prompt = _layer(y_prompt, lw, *rope_p)
        y_sample = _layer(y_sample, lw, *rope_s)
    return (y_prompt, y_sample)
```

```python
import functools

import jax
import jax.numpy as jnp
from jax import lax
from jax.experimental import pallas as pl
from jax.experimental.pallas import tpu as pltpu

F32 = jnp.float32
BF16 = jnp.bfloat16

D_MODEL = 1024
HEAD = 64
N_HEADS = 4
GRID_W = 64
EPS = 1e-6
SGU_CHUNK = 128
DN_CHUNK = 64
ROPE_THETA = 10000.0
POOL_WINDOWS = (2, 4, 8, 16)
HALO = 16
NEG_BIG = -1e30
LOG2_E = 1.4426950408889634
ATTN_TK = 512
SCORE_BOUND = 64.0
SCORE_MARGIN = 1.05

V7X_VMEM_LIMIT_BYTES = 56 * 1024 * 1024

_A_U = 0
_B_Q, _B_Z, _B_BETA = 768, 1536, 1792
_C_Q = 1808
_D_X, _D_Z = 2576, 2832
MAIN_COLS = 2176


def _cparams(semantics):
    return pltpu.CompilerParams(dimension_semantics=semantics,
                                vmem_limit_bytes=V7X_VMEM_LIMIT_BYTES)


def _split2(x):
    hi = x.astype(BF16)
    lo = (x - hi.astype(F32)).astype(BF16)
    return hi, lo


def _iota(shape, dim):
    return lax.broadcasted_iota(jnp.int32, shape, dim)


def _ones_where(cond):
    return jnp.where(cond, 1.0, 0.0).astype(BF16)


def _head_sum(x2):
    w = x2.shape[1]
    g = _ones_where(jnp.right_shift(_iota((w, w), 0), 6) == jnp.right_shift(_iota((w, w), 1), 6))
    return jnp.dot(x2.astype(BF16), g, preferred_element_type=F32)


def _silu(z):
    return 0.5 * z * (1.0 + jnp.tanh(0.5 * z))


def _bd_mask():
    return jnp.right_shift(_iota((256, 256), 0), 6) == jnp.right_shift(_iota((256, 256), 1), 6)


def _bd(x, mask01):
    z = jnp.zeros((HEAD, 128), x.dtype)
    blocks = []
    for h in range(N_HEADS):
        t = h // 2
        m = x[:, t * 128:(t + 1) * 128] * mask01[h * HEAD:(h + 1) * HEAD, t * 128:(t + 1) * 128]
        blocks.append(jnp.concatenate([m, z] if t == 0 else [z, m], axis=1))
    return jnp.concatenate(blocks, axis=0)


def _proj_kernel(xp_ref, xc_ref, xn_ref, nw_ref, wh_ref, wm_ref, sguw_ref, sgub_ref, convw_ref,
                 dnrow_ref, qw_ref, kw_ref, rc_ref, rs_ref, poolw_ref, pools_ref,
                 ya_ref, yd_ref, dq_ref, dk_ref, dv_ref, bg_ref, gates_ref, aqt_ref, ak_ref, avt_ref,
                 hext_ref, *, tm, tq, n_tiles, seq_len):
    i = pl.program_id(1)
    n_ext = tm + 2 * HALO
    nw = nw_ref[...]

    def norm(x):
        ms = jnp.mean(x * x, axis=-1, keepdims=True)
        return x * lax.rsqrt(ms + EPS) * nw

    hext_ref[0:HALO, :] = jnp.where(i > 0, norm(xp_ref[0]), 0.0).astype(BF16)
    hext_ref[HALO:HALO + tm, :] = norm(xc_ref[0]).astype(BF16)
    hext_ref[HALO + tm:n_ext, :] = jnp.where(i < n_tiles - 1, norm(xn_ref[0]), 0.0).astype(BF16)

    ph = jnp.dot(hext_ref[...], wh_ref[...], preferred_element_type=F32)
    pm = jnp.dot(hext_ref[HALO:HALO + tm, :], wm_ref[...], preferred_element_type=F32)

    def rows(x):
        return x[HALO:HALO + tm]

    a_u, a_v, a_z = pm[:, 0:256], pm[:, 256:512], pm[:, 512:768]
    vn = a_v * lax.rsqrt(_head_sum(a_v * a_v) * (1.0 / HEAD) + EPS)
    mask4 = _ones_where(jnp.right_shift(_iota((512, 256), 0), 7) == jnp.right_shift(_iota((512, 256), 1), 6))
    mixed = []
    for c in range(tm // SGU_CHUNK):
        vc = vn[c * SGU_CHUNK:(c + 1) * SGU_CHUNK].astype(BF16)
        bdv = jnp.concatenate([vc, vc, vc, vc], axis=0) * mask4
        mixed.append(jnp.dot(sguw_ref[...], bdv, preferred_element_type=F32) + sgub_ref[...])
    mixed = jnp.concatenate(mixed, axis=0)
    ya_ref[0] = (a_u * mixed * _silu(a_z)).astype(BF16)

    xd = ph[:, 768:1024]
    a1 = xd + pltpu.roll(xd, n_ext - 1, 0)
    a2 = a1 + pltpu.roll(a1, n_ext - 2, 0)
    a3 = a2 + pltpu.roll(a2, n_ext - 4, 0)
    a4 = a3 + pltpu.roll(a3, n_ext - 8, 0)
    w2 = rows(pltpu.roll(a1, 1, 0))
    w4 = rows(pltpu.roll(a2, 2, 0))
    w8 = rows(pltpu.roll(a3, 4, 0))
    w16 = rows(pltpu.roll(a4, 8, 0))
    grp = jnp.right_shift(_iota((tm, 256), 1), 6)
    half = jnp.left_shift(jnp.ones((tm, 256), jnp.int32), grp)
    t = i * tm + _iota((tm, 256), 0)
    cnt = (jnp.minimum(t + half, seq_len) - jnp.maximum(t - half, 0)).astype(F32)
    win = jnp.where(grp == 0, w2, jnp.where(grp == 1, w4, jnp.where(grp == 2, w8, w16)))
    diff = win / cnt - rows(xd)
    yd = jnp.dot(diff.astype(BF16), poolw_ref[...], preferred_element_type=F32) * pools_ref[...]
    yd_ref[0] = (yd * _silu(pm[:, 1792:2048])).astype(BF16)

    xb = ph[:, 0:768]
    cw = convw_ref[...]
    conv = (rows(pltpu.roll(xb, 2, 0)) * cw[0:1] + rows(pltpu.roll(xb, 1, 0)) * cw[1:2]
            + rows(xb) * cw[2:3] + rows(pltpu.roll(xb, n_ext - 1, 0)) * cw[3:4]
            + rows(pltpu.roll(xb, n_ext - 2, 0)) * cw[4:5])
    act = _silu(conv)
    bq, bk = act[:, 0:256], act[:, 256:512]
    dq_ref[0] = (bq * lax.rsqrt(_head_sum(bq * bq) + EPS) * (HEAD ** -0.5)).astype(BF16)
    dk_ref[0] = (bk * lax.rsqrt(_head_sum(bk * bk) + EPS)).astype(BF16)
    dv_ref[0] = act[:, 512:768].astype(BF16)
    ba = pm[:, 2048:2176]
    lane = _iota((tm, 128), 1)
    xa = ba + dnrow_ref[1:2, :]
    softplus = jnp.maximum(xa, 0.0) + jnp.log1p(jnp.exp(-jnp.abs(xa)))
    g = -jnp.exp(dnrow_ref[0:1, :]) * softplus
    bg_ref[0] = jnp.where(lane < 8, 1.0 / (1.0 + jnp.exp(-ba)), g)
    gates_ref[0] = jnp.concatenate([_silu(pm[:, 768:1024]), _silu(pm[:, 1536:1792])], axis=1).astype(BF16)

    rc, rs = rc_ref[...], rs_ref[...]
    first = (_iota((tm, 128), 1) & 16) == 0

    def rope(x):
        sw = jnp.where(first, pltpu.roll(x, 112, 1), pltpu.roll(x, 16, 1))
        return x * rc + sw * rs

    cq, ck = pm[:, 1024:1280], pm[:, 1280:1408]
    qn = cq * lax.rsqrt(_head_sum(cq * cq) * (1.0 / HEAD) + EPS) * qw_ref[...]
    qr = jnp.concatenate([rope(qn[:, 0:128]), rope(qn[:, 128:256])], axis=1) * (HEAD ** -0.5 * LOG2_E)
    qt = qr.T.astype(BF16)
    zq = jnp.zeros((HEAD, 2 * tq), BF16)
    for r in range(tm // tq):
        cs = slice(r * tq, (r + 1) * tq)
        top0 = jnp.concatenate([qt[0:64, cs], qt[64:128, cs]], axis=1)
        top1 = jnp.concatenate([qt[128:192, cs], qt[192:256, cs]], axis=1)
        aqt_ref[0, 0, r] = jnp.concatenate([top0, zq], axis=0)
        aqt_ref[0, 1, r] = jnp.concatenate([zq, top1], axis=0)
    kn = ck * lax.rsqrt(_head_sum(ck * ck) * (1.0 / HEAD) + EPS) * kw_ref[...]
    ak_ref[0] = rope(kn).astype(BF16)
    vt = pm[:, 1408:1536].T
    ones = jnp.ones((16, tm), F32)
    for g in range(2):
        vg = jnp.concatenate([vt[g * HEAD:(g + 1) * HEAD], ones], axis=0).astype(BF16)
        for r in range(tm // ATTN_TK):
            avt_ref[0, g, r] = vg[:, r * ATTN_TK:(r + 1) * ATTN_TK]


def _proj_call(x, lw, rope_c, rope_s, tm, tq):
    b, s, _ = x.shape
    n_tiles = s // tm
    hb = tm // HALO
    last_hb = s // HALO - 1
    full = lambda shape: pl.BlockSpec(shape, lambda bi, i: (0,) * len(shape))
    tok = lambda w: pl.BlockSpec((1, tm, w), lambda bi, i: (bi, i, 0))
    in_specs = [
        pl.BlockSpec((1, HALO, D_MODEL), lambda bi, i: (bi, jnp.maximum(i * hb - 1, 0), 0)),
        pl.BlockSpec((1, tm, D_MODEL), lambda bi, i: (bi, i, 0)),
        pl.BlockSpec((1, HALO, D_MODEL), lambda bi, i: (bi, jnp.minimum((i + 1) * hb, last_hb), 0)),
        full((1, D_MODEL)), full((D_MODEL, 1024)), full((D_MODEL, MAIN_COLS)),
        full((128, 512)), full((128, 256)), full((8, 768)), full((8, 128)),
        full((1, 256)), full((1, 128)),
        pl.BlockSpec((tm, 128), lambda bi, i: (i, 0)), pl.BlockSpec((tm, 128), lambda bi, i: (i, 0)),
        full((256, 256)), full((1, 256)),
    ]
    out_shape = [
        jax.ShapeDtypeStruct((b, s, 256), BF16),
        jax.ShapeDtypeStruct((b, s, 256), BF16),
        jax.ShapeDtypeStruct((b, s, 256), BF16),
        jax.ShapeDtypeStruct((b, s, 256), BF16),
        jax.ShapeDtypeStruct((b, s, 256), BF16),
        jax.ShapeDtypeStruct((b, s, 128), F32),
        jax.ShapeDtypeStruct((b, s, 512), BF16),
        jax.ShapeDtypeStruct((b, 2, s // tq, 128, 2 * tq), BF16),
        jax.ShapeDtypeStruct((b, s, 128), BF16),
        jax.ShapeDtypeStruct((b, 2, s // ATTN_TK, HEAD + 16, ATTN_TK), BF16),
    ]
    out_specs = [tok(256), tok(256), tok(256), tok(256), tok(256), tok(128), tok(512),
                 pl.BlockSpec((1, 2, tm // tq, 128, 2 * tq), lambda bi, i: (bi, 0, i, 0, 0)), tok(128),
                 pl.BlockSpec((1, 2, tm // ATTN_TK, HEAD + 16, ATTN_TK), lambda bi, i: (bi, 0, i, 0, 0))]
    return pl.pallas_call(
        functools.partial(_proj_kernel, tm=tm, tq=tq, n_tiles=n_tiles, seq_len=s),
        grid=(b, n_tiles), in_specs=in_specs, out_specs=out_specs, out_shape=out_shape,
        scratch_shapes=[pltpu.VMEM((tm + 2 * HALO, D_MODEL), BF16)],
        compiler_params=_cparams(("parallel", "parallel")), name="proj",
    )(x, x, x, lw["norm_w"], lw["w_halo"], lw["w_main"], lw["sgu_w"], lw["sgu_b"], lw["conv_w"],
      lw["dn_row"], lw["q_norm_w"], lw["k_norm_w"], rope_c, rope_s, lw["pool_w"], lw["pool_scale"])


def _dn_kernel(*refs, nb, tb):
    ins, (of_ref, ob_ref) = refs[:24], refs[24:26]
    (state_ref, expand_ref, ones_ref, bd_ref,
     a_sc, vb_sc, kbg_sc, in_sc, qd_sc, kdt_sc, egl_sc) = refs[26:]
    nc = tb // DN_CHUNK
    step = pl.program_id(0)

    li = _iota((tb, 256), 0) & 63
    lj = _iota((tb, 256), 1) & 63
    eyecat = li == lj
    eyef = jnp.where(eyecat, 1.0, 0.0)
    bdm = _bd_mask()
    lane = _iota((tb, 128), 1)
    row64 = _iota((tb, 128), 0) & 63
    zpad = jnp.zeros((DN_CHUNK, 256), BF16)

    def prepare_block_set(which, slot):
        j2 = ones_ref[...]
        bdm01 = bd_ref[...]
        blocks = []
        for d in range(2):
            q_ref, k_ref, v_ref, bg_ref = ins[(which * 2 + d) * 4:(which * 2 + d) * 4 + 4]
            for bi in range(nb):
                blocks.append(dict(d=d, bi=bi, idx=d * nb + bi, q_ref=q_ref, k_ref=k_ref, v_ref=v_ref,
                                   bg=bg_ref[bi], expand=expand_ref[d],
                                   incl=(li >= lj) if d == 0 else (li <= lj),
                                   strict=(li > lj) if d == 0 else (li < lj)))
        for blk in blocks:
            c = blk["bg"]
            for s in (1, 2, 4, 8, 16, 32):
                c = c + jnp.where(row64 >= s, pltpu.roll(c, s, 0), 0.0)
            tot = jnp.concatenate([jnp.broadcast_to(c[ci * DN_CHUNK + DN_CHUNK - 1:(ci + 1) * DN_CHUNK, :],
                                                    (DN_CHUNK, 128)) for ci in range(nc)], axis=0)
            if blk["d"] == 0:
                blk["cs"] = (c, tot - c)
            else:
                blk["cs"] = (tot - c + blk["bg"], c - blk["bg"])
            yield
        for blk in blocks:
            cs_in, cs_st = blk["cs"]
            nbx = jnp.where(lane < 8, blk["bg"], jnp.where(lane < 16, cs_in, pltpu.roll(cs_st, 8, 1)))
            hi = nbx.astype(BF16)
            lo = pltpu.roll(nbx - hi.astype(F32), 64, 1)
            packed = jnp.where(lane < 64, hi.astype(F32), lo).astype(BF16)
            blk["x"] = jnp.dot(packed, blk["expand"], preferred_element_type=F32)
            yield
        for blk in blocks:
            gcx = blk["x"][:, 256:512]
            blk["rowf"] = jnp.dot(j2, jnp.concatenate(_split2(jnp.where(eyecat, gcx, 0.0)), axis=0),
                                  preferred_element_type=F32)
            yield
        for blk in blocks:
            gram, qk = [], []
            for ci in range(nc):
                sl = slice(ci * DN_CHUNK, (ci + 1) * DN_CHUNK)
                kc = blk["k_ref"][blk["bi"], sl, :]
                gq = lax.dot_general(jnp.concatenate([kc, blk["q_ref"][blk["bi"], sl, :]], axis=0),
                                     _bd(kc, bdm01), (((1,), (1,)), ((), ())),
                                     preferred_element_type=F32)
                gram.append(gq[0:64])
                qk.append(gq[64:128])
            blk["gram"] = jnp.concatenate(gram, axis=0)
            blk["qk"] = jnp.concatenate(qk, axis=0)
            yield
        for blk in blocks:
            d, bi, idx = blk["d"], blk["bi"], blk["idx"]
            q = blk["q_ref"][bi].astype(F32)
            k = blk["k_ref"][bi].astype(F32)
            v = blk["v_ref"][bi].astype(F32)
            x = blk["x"]
            bx, gcx, dglx = x[:, 0:256], x[:, 256:512], x[:, 512:768]
            decay = jnp.exp(jnp.where(blk["incl"], gcx - blk["rowf"], NEG_BIG))
            eg = jnp.exp(gcx)
            a_sc[slot, idx] = jnp.where(blk["strict"], bx * blk["gram"] * decay, 0.0)
            in_sc[slot, idx] = jnp.where(blk["incl"], blk["qk"] * decay, 0.0).astype(BF16)
            qd_sc[slot, idx] = (q * eg).astype(BF16)
            kdt_sc[slot, idx] = (k * jnp.exp(dglx)).T.astype(BF16)
            vb_sc[slot, idx] = (v * bx).astype(BF16)
            kbg_sc[slot, idx] = (k * bx * eg).astype(BF16)
            for ci in range(nc):
                last = ci * DN_CHUNK + (DN_CHUNK - 1 if d == 0 else 0)
                egl_sc[slot, idx, ci * 8:(ci + 1) * 8, :] = jnp.broadcast_to(eg[last:last + 1, :], (8, 256))
            yield

    def matmul_part(slot, half):
        bdm01 = bd_ref[...]

        def prepare(group, phase):
            for ch in group:
                idx, sl = ch["idx"], ch["sl"]
                if phase == 0:
                    ch["ak"] = a_sc[slot, idx, sl, :]
                    ch["p"] = eyef[sl] - ch["ak"]
                if phase < 5:
                    akb = ch["ak"].astype(BF16)
                    lhs = akb if phase == 0 else jnp.concatenate([akb, ch["p"].astype(BF16)], axis=0)
                    res = jnp.dot(lhs, _bd(akb, bdm01), preferred_element_type=F32)
                    ch["ak"] = res[0:64]
                    if phase > 0:
                        ch["p"] = ch["p"] + res[64:128]
                elif phase == 5:
                    ch["p"] = ch["p"] + jnp.dot(ch["p"].astype(BF16), _bd(ch["ak"].astype(BF16), bdm01),
                                                preferred_element_type=F32)
                else:
                    rhs = jnp.concatenate([_bd(vb_sc[slot, idx, sl, :], bdm01),
                                           _bd(kbg_sc[slot, idx, sl, :], bdm01)], axis=1)
                    uw = jnp.dot(ch["p"].astype(BF16), rhs, preferred_element_type=F32)
                    ch["u"] = uw[:, 0:256]
                    ch["w"] = uw[:, 256:512].astype(BF16)

        def recur_a(group):
            for ch in group:
                ch["state"] = state_ref[ch["si"]]
                res = jnp.dot(jnp.concatenate([ch["w"], qd_sc[slot, ch["idx"], ch["sl"], :]], axis=0),
                              ch["state"].astype(BF16), preferred_element_type=F32)
                ch["v_new"] = (ch["u"] - res[0:64]).astype(BF16)
                ch["o_inter"] = res[64:128]

        def recur_b(group):
            for ch in group:
                v_new, idx, ci = ch["v_new"], ch["idx"], ch["ci"]
                ch["o_ref"][ch["bi"], ch["osl"], :] = (ch["o_inter"] + jnp.dot(
                    in_sc[slot, idx, ch["sl"], :], _bd(v_new, bdm01), preferred_element_type=F32)).astype(BF16)
                vpad = jnp.concatenate([v_new, zpad] if ci % 2 == 0 else [zpad, v_new], axis=0)
                kpair = kdt_sc[slot, idx, :, (ci // 2) * 128:(ci // 2 + 1) * 128]
                ds = jnp.dot(kpair, vpad, preferred_element_type=F32)
                egl = egl_sc[slot, idx, ci * 8:ci * 8 + 1, :]
                state_ref[ch["si"]] = ch["state"] * egl + jnp.where(bdm, ds, 0.0)

        groups = []
        for s in range(nc):
            group = []
            for bi in range(nb):
                for d in range(2):
                    ci = s if d == 0 else nc - 1 - s
                    row0 = (half if d == 0 else 1 - half) * tb + ci * DN_CHUNK
                    group.append(dict(idx=d * nb + bi, si=d * nb + bi, bi=bi, ci=ci,
                                      o_ref=(of_ref, ob_ref)[d],
                                      sl=slice(ci * DN_CHUNK, (ci + 1) * DN_CHUNK),
                                      osl=slice(row0, row0 + DN_CHUNK)))
            groups.append(group)
        for phase in range(7):
            prepare(groups[0], phase)
            yield
        for s in range(nc):
            nxt = groups[s + 1] if s + 1 < nc else []
            for piece in ((prepare, nxt, 0), (prepare, nxt, 1), (recur_a, groups[s]), (prepare, nxt, 2),
                          (prepare, nxt, 3), (recur_b, groups[s]), (prepare, nxt, 4), (prepare, nxt, 5),
                          (prepare, nxt, 6)):
                piece[0](*piece[1:])
                yield

    def interleave(first, second):
        streams = [first, second]
        while streams:
            for gen in list(streams):
                if next(gen, StopIteration) is StopIteration:
                    streams.remove(gen)

    @pl.when(step == 0)
    def _():
        state_ref[...] = jnp.zeros_like(state_ref)
        r = _iota((tb, tb), 0)
        c = _iota((tb, tb), 1)
        same = jnp.right_shift(r, 6) == jnp.right_shift(c, 6)
        src = _iota((128, 768), 0)
        col = _iota((128, 768), 1)
        for d in range(2):
            expand_ref[d] = _ones_where((src & 63) == jnp.left_shift(jnp.right_shift(col, 8), 3) + 4 * d
                                        + (jnp.right_shift(col, 6) & 3))
        ones_ref[...] = jnp.concatenate([_ones_where(same), _ones_where(same)], axis=1)
        bd_ref[...] = _ones_where(_bd_mask())
        for _ in prepare_block_set(0, 0):
            pass

    interleave(matmul_part(0, 0), prepare_block_set(1, 1))
    interleave(matmul_part(1, 1), prepare_block_set(2, 0))


def _dn_call(dq, dk, dv, bg, tb):
    b, s, _ = dq.shape
    n = s // tb
    n2 = n // 2

    def spec(w, index):
        return pl.BlockSpec((b, tb, w), lambda i: (0, index(i), 0))

    orders = [lambda j: j, lambda j: n - 1 - j]
    picks = [lambda i: 0, lambda i: 2 * i + 1, lambda i: jnp.minimum(2 * i + 2, n - 1)]
    in_specs, args = [], []
    for pick in picks:
        for order in orders:
            index = functools.partial(lambda i, pick, order: order(pick(i)), pick=pick, order=order)
            in_specs += [spec(256, index), spec(256, index), spec(256, index), spec(128, index)]
            args += [dq, dk, dv, bg]
    nblk = 2 * b
    return pl.pallas_call(
        functools.partial(_dn_kernel, nb=b, tb=tb),
        grid=(n2,), in_specs=in_specs,
        out_specs=[pl.BlockSpec((b, 2 * tb, 256), lambda i: (0, i, 0)),
                   pl.BlockSpec((b, 2 * tb, 256), lambda i: (0, n2 - 1 - i, 0))],
        out_shape=[jax.ShapeDtypeStruct((b, s, 256), BF16)] * 2,
        scratch_shapes=[pltpu.VMEM((nblk, 256, 256), F32), pltpu.VMEM((2, 128, 768), BF16),
                        pltpu.VMEM((tb, 2 * tb), BF16), pltpu.VMEM((256, 256), BF16),
                        pltpu.VMEM((2, nblk, tb, 256), F32), pltpu.VMEM((2, nblk, tb, 256), BF16),
                        pltpu.VMEM((2, nblk, tb, 256), BF16), pltpu.VMEM((2, nblk, tb, 256), BF16),
                        pltpu.VMEM((2, nblk, tb, 256), BF16), pltpu.VMEM((2, nblk, 256, tb), BF16),
                        pltpu.VMEM((2, nblk, tb // 8, 256), F32)],
        compiler_params=_cparams(("arbitrary",)), name="dn",
    )(*args)


def _attn_kernel(qt_ref, k_ref, vt_ref, o_ref, st_ref, *, tq, tk, n_q, n_kv, unroll, bounded):
    per_q = n_kv // unroll
    n_trips = n_q * per_q

    def scores(qi, j, slot):
        kj = k_ref[0, pl.ds(pl.multiple_of(j * tk, tk), tk), :]
        st = jnp.dot(kj, qt_ref[0, 0, qi], preferred_element_type=F32)
        st_ref[slot, :, 0:2 * tq] = st
        return None if bounded else jnp.max(st, axis=0, keepdims=True)

    def update(j, st, mx, m, acc):
        if bounded:
            return m, acc + jnp.dot(vt_ref[0, 0, j], jnp.exp2(st).astype(BF16), preferred_element_type=F32)
        m_new = jnp.maximum(m, mx)
        p = jnp.exp2(st - m_new).astype(BF16)
        alpha = jnp.exp2(m - m_new)
        return m_new, alpha * acc + jnp.dot(vt_ref[0, 0, j], p, preferred_element_type=F32)

    def body(t, carry):
        qi = t // per_q
        base = (t - qi * per_q) * unroll
        first = base == 0
        m = jnp.where(first, -jnp.inf, carry[0])
        acc = jnp.where(first, 0.0, carry[1])
        mx = carry[2]
        t_next = jnp.minimum(t + 1, n_trips - 1)
        qi_next = t_next // per_q
        base_next = (t_next - qi_next * per_q) * unroll
        for r in range(unroll):
            if r < unroll - 1:
                mx_next = scores(qi, base + r + 1, (r + 1) % 2)
            else:
                mx_next = scores(qi_next, base_next, 0)
            m, acc = update(base + r, st_ref[r % 2, :, 0:2 * tq], mx, m, acc)
            mx = mx if bounded else mx_next

        @pl.when(base == n_kv - unroll)
        def _():
            o = acc[0:HEAD] / acc[HEAD:HEAD + 1]
            ot = jnp.concatenate([o[:, 0:tq], o[:, tq:2 * tq]], axis=0)
            o_ref[0, pl.ds(pl.multiple_of(qi * tq, tq), tq), :] = ot.T.astype(BF16)

        return m, acc, mx

    m0 = jnp.full((1, 2 * tq), -jnp.inf, F32)
    a0 = jnp.zeros((HEAD + 16, 2 * tq), F32)
    mx0 = scores(0, 0, 0)
    lax.fori_loop(0, n_trips, body, (m0, a0, m0 if bounded else mx0))


def _attn_call(aqt, ak, avt, bounded):
    b, s, _ = ak.shape
    _, _, n_q, _, tq2 = aqt.shape
    _, _, n_kv, vrows, tk = avt.shape
    unroll = next(u for u in (32, 16, 8, 4, 2) if n_kv % u == 0)
    return pl.pallas_call(
        functools.partial(_attn_kernel, tq=tq2 // 2, tk=tk, n_q=n_q, n_kv=n_kv, unroll=unroll,
                          bounded=bounded),
        grid=(b, 2),
        in_specs=[pl.BlockSpec((1, 1, n_q, 128, tq2), lambda bi, g: (bi, g, 0, 0, 0)),
                  pl.BlockSpec((1, s, 128), lambda bi, g: (bi, 0, 0)),
                  pl.BlockSpec((1, 1, n_kv, vrows, tk), lambda bi, g: (bi, g, 0, 0, 0))],
        out_specs=pl.BlockSpec((1, s, 128), lambda bi, g: (bi, 0, g)),
        out_shape=jax.ShapeDtypeStruct((b, s, 256), BF16),
        scratch_shapes=[pltpu.VMEM((2, tk, tq2 + 128), F32)],
        compiler_params=_cparams(("parallel", "parallel")), name="attn",
    )(aqt, ak, avt)


def _out_kernel(x_ref, ya_ref, yd_ref, yc_ref, of_ref, ob_ref, gates_ref, dnw_ref, wo_ref, o_ref):
    o = of_ref[0].astype(F32) + ob_ref[0].astype(F32)
    gates = gates_ref[0].astype(F32)
    on = o * lax.rsqrt(_head_sum(o * o) * (1.0 / HEAD) + EPS) * dnw_ref[...]
    yb = (on * gates[:, 0:256]).astype(BF16)
    yc = (yc_ref[0].astype(F32) * gates[:, 256:512]).astype(BF16)
    mix = jnp.concatenate([ya_ref[0], yb, yc, yd_ref[0]], axis=1)
    o_ref[0] = x_ref[0] + jnp.dot(mix, wo_ref[...], preferred_element_type=F32)


def _out_call(x, ya, yd, yc, o_f, o_b, gates, lw, tm):
    b, s, _ = x.shape
    tok = lambda w: pl.BlockSpec((1, tm, w), lambda bi, i: (bi, i, 0))
    full = lambda shape: pl.BlockSpec(shape, lambda bi, i: (0,) * len(shape))
    return pl.pallas_call(
        _out_kernel, grid=(b, s // tm),
        in_specs=[tok(D_MODEL), tok(256), tok(256), tok(256), tok(256), tok(256), tok(512),
                  full((1, 256)), full((D_MODEL, D_MODEL))],
        out_specs=tok(D_MODEL), out_shape=jax.ShapeDtypeStruct((b, s, D_MODEL), F32),
        compiler_params=_cparams(("parallel", "parallel")), name="out",
    )(x, ya, yd, yc, o_f, o_b, gates, lw["dn_norm_w"], lw["w_out"])


def _rope_tables(seq_len):
    t = jnp.arange(seq_len)
    pos = jnp.stack([t // GRID_W, t % GRID_W], axis=-1).astype(F32)
    n_freq = HEAD // 4
    inv_freq = jnp.power(ROPE_THETA, -2.0 * jnp.arange(n_freq, dtype=F32) / (HEAD // 2))
    ang = pos[:, :, None] * inv_freq
    cos = jnp.repeat(jnp.cos(ang)[:, :, None, :], 2, axis=2).reshape(seq_len, HEAD)
    sin = jnp.sin(ang)
    sin = jnp.stack([-sin, sin], axis=2).reshape(seq_len, HEAD)
    return jnp.tile(cos, (1, 2)), jnp.tile(sin, (1, 2))


def _layer_weights(l, norm_w, w_in, sgu_w, sgu_b, conv_w, a_log, dt_bias, dn_norm_w, q_norm_w,
                   k_norm_w, pool_w, pool_scale, w_out):
    w = w_in[l]
    cols = lambda a, n: w[:, a:a + n]
    w_halo = jnp.concatenate([cols(_B_Q, 768), cols(_D_X, 256)], axis=1).astype(BF16)
    w_main = jnp.concatenate([
        cols(_A_U, 768), cols(_B_Z, 256), cols(_C_Q, 768), cols(_D_Z, 256), cols(_B_BETA, 16),
        jnp.zeros((D_MODEL, 112), F32)], axis=1).astype(BF16)
    dn_row = jnp.zeros((8, 128), F32)
    dn_row = dn_row.at[0, 8:16].set(a_log[l].reshape(8)).at[1, 8:16].set(dt_bias[l].reshape(8))
    pool_bd = jnp.zeros((256, 256), F32)
    for gi in range(len(POOL_WINDOWS)):
        pool_bd = pool_bd.at[gi * 64:(gi + 1) * 64, gi * 64:(gi + 1) * 64].set(pool_w[l, gi])
    return {
        "norm_w": norm_w[l].reshape(1, D_MODEL),
        "w_halo": w_halo, "w_main": w_main,
        "sgu_w": jnp.transpose(sgu_w[l], (1, 0, 2)).reshape(SGU_CHUNK, 4 * SGU_CHUNK).astype(BF16),
        "sgu_b": jnp.repeat(sgu_b[l].T, HEAD, axis=1),
        "conv_w": jnp.concatenate([conv_w[l], jnp.zeros((3, 768), F32)], axis=0),
        "dn_row": dn_row,
        "q_norm_w": jnp.tile(q_norm_w[l], 4).reshape(1, 256),
        "k_norm_w": jnp.tile(k_norm_w[l], 2).reshape(1, 128),
        "pool_w": pool_bd.astype(BF16),
        "pool_scale": pool_scale[l].reshape(1, 256),
        "dn_norm_w": jnp.tile(dn_norm_w[l], 4).reshape(1, 256),
        "score_bound": (HEAD * jnp.max(jnp.abs(q_norm_w[l])) * jnp.max(jnp.abs(k_norm_w[l]))
                        * (HEAD ** -0.5 * LOG2_E)),
        "w_out": w_out[l].astype(BF16),
    }


def _tiles(batch, seq_len):
    tb = min(DN_CHUNK * max(1, 8 // batch), seq_len)
    return dict(tm=min(512, seq_len), tb=tb, tq=min(256, seq_len))


def _layer(x, lw, rope_c, rope_s):
    t = _tiles(x.shape[0], x.shape[1])
    ya, yd, dq, dk, dv, bg, gates, aqt, ak, avt = _proj_call(x, lw, rope_c, rope_s, t["tm"], t["tq"])
    o_f, o_b = _dn_call(dq, dk, dv, bg, t["tb"])
    yc = lax.cond(lw["score_bound"] * SCORE_MARGIN <= SCORE_BOUND,
                  functools.partial(_attn_call, bounded=True),
                  functools.partial(_attn_call, bounded=False), aqt, ak, avt)
    return _out_call(x, ya, yd, yc, o_f, o_b, gates, lw, t["tm"])


def kernel(x_prompt, x_sample, norm_w, w_in, sgu_w, sgu_b, conv_w, a_log, dt_bias, dn_norm_w,
           q_norm_w, k_norm_w, pool_w, pool_scale, w_out):
    depth = norm_w.shape[0]
    rope_p = _rope_tables(x_prompt.shape[1])
    rope_s = _rope_tables(x_sample.shape[1])
    y_prompt, y_sample = x_prompt, x_sample
    for l in range(depth):
        lw = _layer_weights(l, norm_w, w_in, sgu_w, sgu_b, conv_w, a_log, dt_bias, dn_norm_w,
                            q_norm_w, k_norm_w, pool_w, pool_scale, w_out)
        y_prompt = _layer(y_prompt, lw, *rope_p)
        y_sample = _layer(y_sample, lw, *rope_s)
    return (y_prompt, y_sample)
```

```python
import functools

import jax
import jax.numpy as jnp
from jax import lax
from jax.experimental import pallas as pl
from jax.experimental.pallas import tpu as pltpu

F32 = jnp.float32
BF16 = jnp.bfloat16

D_MODEL = 1024
HEAD = 64
N_HEADS = 4
GRID_W = 64
EPS = 1e-6
SGU_CHUNK = 128
DN_CHUNK = 64
ROPE_THETA = 10000.0
POOL_WINDOWS = (2, 4, 8, 16)
HALO = 16
NEG_BIG = -1e30
LOG2_E = 1.4426950408889634
ATTN_TK = 512
SCORE_BOUND = 64.0
SCORE_MARGIN = 1.05

V7X_VMEM_LIMIT_BYTES = 56 * 1024 * 1024

_A_U = 0
_B_Q, _B_Z, _B_BETA = 768, 1536, 1792
_C_Q = 1808
_D_X, _D_Z = 2576, 2832
MAIN_COLS = 2176


def _cparams(semantics):
    return pltpu.CompilerParams(dimension_semantics=semantics,
                                vmem_limit_bytes=V7X_VMEM_LIMIT_BYTES)


def _split2(x):
    hi = x.astype(BF16)
    lo = (x - hi.astype(F32)).astype(BF16)
    return hi, lo


def _iota(shape, dim):
    return lax.broadcasted_iota(jnp.int32, shape, dim)


def _ones_where(cond):
    return jnp.where(cond, 1.0, 0.0).astype(BF16)


def _head_sum(x2):
    w = x2.shape[1]
    g = _ones_where(jnp.right_shift(_iota((w, w), 0), 6) == jnp.right_shift(_iota((w, w), 1), 6))
    return jnp.dot(x2.astype(BF16), g, preferred_element_type=F32)


def _silu(z):
    return 0.5 * z * (1.0 + jnp.tanh(0.5 * z))


def _bd_mask():
    return jnp.right_shift(_iota((256, 256), 0), 6) == jnp.right_shift(_iota((256, 256), 1), 6)


def _bd(x, mask01):
    z = jnp.zeros((HEAD, 128), x.dtype)
    blocks = []
    for h in range(N_HEADS):
        t = h // 2
        m = x[:, t * 128:(t + 1) * 128] * mask01[h * HEAD:(h + 1) * HEAD, t * 128:(t + 1) * 128]
        blocks.append(jnp.concatenate([m, z] if t == 0 else [z, m], axis=1))
    return jnp.concatenate(blocks, axis=0)


def _proj_kernel(xp_ref, xc_ref, xn_ref, nw_ref, wh_ref, wm_ref, sguw_ref, sgub_ref, convw_ref,
                 dnrow_ref, qw_ref, kw_ref, rc_ref, rs_ref, poolw_ref, pools_ref,
                 ya_ref, yd_ref, dq_ref, dk_ref, dv_ref, bg_ref, gates_ref, aqt_ref, ak_ref, avt_ref,
                 hext_ref, *, tm, tq, n_tiles, seq_len):
    i = pl.program_id(1)
    n_ext = tm + 2 * HALO
    nw = nw_ref[...]

    def norm(x):
        ms = jnp.mean(x * x, axis=-1, keepdims=True)
        return x * lax.rsqrt(ms + EPS) * nw

    hext_ref[0:HALO, :] = jnp.where(i > 0, norm(xp_ref[0]), 0.0).astype(BF16)
    hext_ref[HALO:HALO + tm, :] = norm(xc_ref[0]).astype(BF16)
    hext_ref[HALO + tm:n_ext, :] = jnp.where(i < n_tiles - 1, norm(xn_ref[0]), 0.0).astype(BF16)

    ph = jnp.dot(hext_ref[...], wh_ref[...], preferred_element_type=F32)
    pm = jnp.dot(hext_ref[HALO:HALO + tm, :], wm_ref[...], preferred_element_type=F32)

    def rows(x):
        return x[HALO:HALO + tm]

    def mixer_a():
        a_u, a_v, a_z = pm[:, 0:256], pm[:, 256:512], pm[:, 512:768]
        vn = a_v * lax.rsqrt(_head_sum(a_v * a_v) * (1.0 / HEAD) + EPS)
        mask4 = _ones_where(jnp.right_shift(_iota((512, 256), 0), 7) == jnp.right_shift(_iota((512, 256), 1), 6))
        yield
        mixed = []
        for c in range(tm // SGU_CHUNK):
            vc = vn[c * SGU_CHUNK:(c + 1) * SGU_CHUNK].astype(BF16)
            bdv = jnp.concatenate([vc, vc, vc, vc], axis=0) * mask4
            mixed.append(jnp.dot(sguw_ref[...], bdv, preferred_element_type=F32) + sgub_ref[...])
            yield
        mixed = jnp.concatenate(mixed, axis=0)
        ya_ref[0] = (a_u * mixed * _silu(a_z)).astype(BF16)
        yield

    def mixer_d():
        xd = ph[:, 768:1024]
        a1 = xd + pltpu.roll(xd, n_ext - 1, 0)
        a2 = a1 + pltpu.roll(a1, n_ext - 2, 0)
        yield
        a3 = a2 + pltpu.roll(a2, n_ext - 4, 0)
        a4 = a3 + pltpu.roll(a3, n_ext - 8, 0)
        yield
        w2 = rows(pltpu.roll(a1, 1, 0))
        w4 = rows(pltpu.roll(a2, 2, 0))
        w8 = rows(pltpu.roll(a3, 4, 0))
        w16 = rows(pltpu.roll(a4, 8, 0))
        yield
        grp = jnp.right_shift(_iota((tm, 256), 1), 6)
        half = jnp.left_shift(jnp.ones((tm, 256), jnp.int32), grp)
        t = i * tm + _iota((tm, 256), 0)
        cnt = (jnp.minimum(t + half, seq_len) - jnp.maximum(t - half, 0)).astype(F32)
        win = jnp.where(grp == 0, w2, jnp.where(grp == 1, w4, jnp.where(grp == 2, w8, w16)))
        diff = win / cnt - rows(xd)
        yield
        yd = jnp.dot(diff.astype(BF16), poolw_ref[...], preferred_element_type=F32) * pools_ref[...]
        yd_ref[0] = (yd * _silu(pm[:, 1792:2048])).astype(BF16)
        yield

    def mixer_b():
        xb = ph[:, 0:768]
        cw = convw_ref[...]
        conv = rows(pltpu.roll(xb, 2, 0)) * cw[0:1] + rows(pltpu.roll(xb, 1, 0)) * cw[1:2]
        yield
        conv = conv + rows(xb) * cw[2:3] + rows(pltpu.roll(xb, n_ext - 1, 0)) * cw[3:4]
        yield
        conv = conv + rows(pltpu.roll(xb, n_ext - 2, 0)) * cw[4:5]
        act = _silu(conv)
        yield
        bq, bk = act[:, 0:256], act[:, 256:512]
        dq_ref[0] = (bq * lax.rsqrt(_head_sum(bq * bq) + EPS) * (HEAD ** -0.5)).astype(BF16)
        yield
        dk_ref[0] = (bk * lax.rsqrt(_head_sum(bk * bk) + EPS)).astype(BF16)
        dv_ref[0] = act[:, 512:768].astype(BF16)
        yield
        ba = pm[:, 2048:2176]
        lane = _iota((tm, 128), 1)
        xa = ba + dnrow_ref[1:2, :]
        softplus = jnp.maximum(xa, 0.0) + jnp.log1p(jnp.exp(-jnp.abs(xa)))
        g = -jnp.exp(dnrow_ref[0:1, :]) * softplus
        bg_ref[0] = jnp.where(lane < 8, 1.0 / (1.0 + jnp.exp(-ba)), g)
        yield
        gates_ref[0] = jnp.concatenate([_silu(pm[:, 768:1024]), _silu(pm[:, 1536:1792])], axis=1).astype(BF16)
        yield

    def mixer_c():
        rc, rs = rc_ref[...], rs_ref[...]
        first = (_iota((tm, 128), 1) & 16) == 0

        def rope(x):
            sw = jnp.where(first, pltpu.roll(x, 112, 1), pltpu.roll(x, 16, 1))
            return x * rc + sw * rs

        cq, ck = pm[:, 1024:1280], pm[:, 1280:1408]
        qn = cq * lax.rsqrt(_head_sum(cq * cq) * (1.0 / HEAD) + EPS) * qw_ref[...]
        yield
        qr = jnp.concatenate([rope(qn[:, 0:128]), rope(qn[:, 128:256])], axis=1) * (HEAD ** -0.5 * LOG2_E)
        qt = qr.T.astype(BF16)
        yield
        zq = jnp.zeros((HEAD, 2 * tq), BF16)
        for r in range(tm // tq):
            cs = slice(r * tq, (r + 1) * tq)
            top0 = jnp.concatenate([qt[0:64, cs], qt[64:128, cs]], axis=1)
            top1 = jnp.concatenate([qt[128:192, cs], qt[192:256, cs]], axis=1)
            aqt_ref[0, 0, r] = jnp.concatenate([top0, zq], axis=0)
            aqt_ref[0, 1, r] = jnp.concatenate([zq, top1], axis=0)
        yield
        kn = ck * lax.rsqrt(_head_sum(ck * ck) * (1.0 / HEAD) + EPS) * kw_ref[...]
        ak_ref[0] = rope(kn).astype(BF16)
        yield
        vt = pm[:, 1408:1536].T
        ones = jnp.ones((16, tm), F32)
        for g in range(2):
            vg = jnp.concatenate([vt[g * HEAD:(g + 1) * HEAD], ones], axis=0).astype(BF16)
            for r in range(tm // ATTN_TK):
                avt_ref[0, g, r] = vg[:, r * ATTN_TK:(r + 1) * ATTN_TK]
        yield

    streams = [mixer_b(), mixer_a(), mixer_c(), mixer_d()]
    while streams:
        for gen in list(streams):
            if next(gen, StopIteration) is StopIteration:
                streams.remove(gen)


def _proj_call(x, lw, rope_c, rope_s, tm, tq):
    b, s, _ = x.shape
    n_tiles = s // tm
    hb = tm // HALO
    last_hb = s // HALO - 1
    full = lambda shape: pl.BlockSpec(shape, lambda bi, i: (0,) * len(shape))
    tok = lambda w: pl.BlockSpec((1, tm, w), lambda bi, i: (bi, i, 0))
    in_specs = [
        pl.BlockSpec((1, HALO, D_MODEL), lambda bi, i: (bi, jnp.maximum(i * hb - 1, 0), 0)),
        pl.BlockSpec((1, tm, D_MODEL), lambda bi, i: (bi, i, 0)),
        pl.BlockSpec((1, HALO, D_MODEL), lambda bi, i: (bi, jnp.minimum((i + 1) * hb, last_hb), 0)),
        full((1, D_MODEL)), full((D_MODEL, 1024)), full((D_MODEL, MAIN_COLS)),
        full((128, 512)), full((128, 256)), full((8, 768)), full((8, 128)),
        full((1, 256)), full((1, 128)),
        pl.BlockSpec((tm, 128), lambda bi, i: (i, 0)), pl.BlockSpec((tm, 128), lambda bi, i: (i, 0)),
        full((256, 256)), full((1, 256)),
    ]
    out_shape = [
        jax.ShapeDtypeStruct((b, s, 256), BF16),
        jax.ShapeDtypeStruct((b, s, 256), BF16),
        jax.ShapeDtypeStruct((b, s, 256), BF16),
        jax.ShapeDtypeStruct((b, s, 256), BF16),
        jax.ShapeDtypeStruct((b, s, 256), BF16),
        jax.ShapeDtypeStruct((b, s, 128), F32),
        jax.ShapeDtypeStruct((b, s, 512), BF16),
        jax.ShapeDtypeStruct((b, 2, s // tq, 128, 2 * tq), BF16),
        jax.ShapeDtypeStruct((b, s, 128), BF16),
        jax.ShapeDtypeStruct((b, 2, s // ATTN_TK, HEAD + 16, ATTN_TK), BF16),
    ]
    out_specs = [tok(256), tok(256), tok(256), tok(256), tok(256), tok(128), tok(512),
                 pl.BlockSpec((1, 2, tm // tq, 128, 2 * tq), lambda bi, i: (bi, 0, i, 0, 0)), tok(128),
                 pl.BlockSpec((1, 2, tm // ATTN_TK, HEAD + 16, ATTN_TK), lambda bi, i: (bi, 0, i, 0, 0))]
    return pl.pallas_call(
        functools.partial(_proj_kernel, tm=tm, tq=tq, n_tiles=n_tiles, seq_len=s),
        grid=(b, n_tiles), in_specs=in_specs, out_specs=out_specs, out_shape=out_shape,
        scratch_shapes=[pltpu.VMEM((tm + 2 * HALO, D_MODEL), BF16)],
        compiler_params=_cparams(("parallel", "parallel")), name="proj",
    )(x, x, x, lw["norm_w"], lw["w_halo"], lw["w_main"], lw["sgu_w"], lw["sgu_b"], lw["conv_w"],
      lw["dn_row"], lw["q_norm_w"], lw["k_norm_w"], rope_c, rope_s, lw["pool_w"], lw["pool_scale"])


def _dn_kernel(*refs, nb, tb):
    ins, (of_ref, ob_ref) = refs[:24], refs[24:26]
    (state_ref, expand_ref, ones_ref, bd_ref,
     a_sc, vb_sc, kbg_sc, in_sc, qd_sc, kdt_sc, egl_sc) = refs[26:]
    nc = tb // DN_CHUNK
    step = pl.program_id(0)

    li = _iota((tb, 256), 0) & 63
    lj = _iota((tb, 256), 1) & 63
    eyecat = li == lj
    eyef = jnp.where(eyecat, 1.0, 0.0)
    bdm = _bd_mask()
    lane = _iota((tb, 128), 1)
    row64 = _iota((tb, 128), 0) & 63
    zpad = jnp.zeros((DN_CHUNK, 256), BF16)

    def prepare_block_set(which, slot):
        j2 = ones_ref[...]
        bdm01 = bd_ref[...]
        blocks = []
        for d in range(2):
            q_ref, k_ref, v_ref, bg_ref = ins[(which * 2 + d) * 4:(which * 2 + d) * 4 + 4]
            for bi in range(nb):
                blocks.append(dict(d=d, bi=bi, idx=d * nb + bi, q_ref=q_ref, k_ref=k_ref, v_ref=v_ref,
                                   bg=bg_ref[bi], expand=expand_ref[d],
                                   incl=(li >= lj) if d == 0 else (li <= lj),
                                   strict=(li > lj) if d == 0 else (li < lj)))
        for blk in blocks:
            c = blk["bg"]
            for s in (1, 2, 4, 8, 16, 32):
                c = c + jnp.where(row64 >= s, pltpu.roll(c, s, 0), 0.0)
            tot = jnp.concatenate([jnp.broadcast_to(c[ci * DN_CHUNK + DN_CHUNK - 1:(ci + 1) * DN_CHUNK, :],
                                                    (DN_CHUNK, 128)) for ci in range(nc)], axis=0)
            if blk["d"] == 0:
                blk["cs"] = (c, tot - c)
            else:
                blk["cs"] = (tot - c + blk["bg"], c - blk["bg"])
            yield
        for blk in blocks:
            cs_in, cs_st = blk["cs"]
            nbx = jnp.where(lane < 8, blk["bg"], jnp.where(lane < 16, cs_in, pltpu.roll(cs_st, 8, 1)))
            hi = nbx.astype(BF16)
            lo = pltpu.roll(nbx - hi.astype(F32), 64, 1)
            packed = jnp.where(lane < 64, hi.astype(F32), lo).astype(BF16)
            blk["x"] = jnp.dot(packed, blk["expand"], preferred_element_type=F32)
            yield
        for blk in blocks:
            gcx = blk["x"][:, 256:512]
            blk["rowf"] = jnp.dot(j2, jnp.concatenate(_split2(jnp.where(eyecat, gcx, 0.0)), axis=0),
                                  preferred_element_type=F32)
            yield
        for blk in blocks:
            gram, qk = [], []
            for ci in range(nc):
                sl = slice(ci * DN_CHUNK, (ci + 1) * DN_CHUNK)
                kc = blk["k_ref"][blk["bi"], sl, :]
                gq = lax.dot_general(jnp.concatenate([kc, blk["q_ref"][blk["bi"], sl, :]], axis=0),
                                     _bd(kc, bdm01), (((1,), (1,)), ((), ())),
                                     preferred_element_type=F32)
                gram.append(gq[0:64])
                qk.append(gq[64:128])
            blk["gram"] = jnp.concatenate(gram, axis=0)
            blk["qk"] = jnp.concatenate(qk, axis=0)
            yield
        for blk in blocks:
            d, bi, idx = blk["d"], blk["bi"], blk["idx"]
            q = blk["q_ref"][bi].astype(F32)
            k = blk["k_ref"][bi].astype(F32)
            v = blk["v_ref"][bi].astype(F32)
            x = blk["x"]
            bx, gcx, dglx = x[:, 0:256], x[:, 256:512], x[:, 512:768]
            decay = jnp.exp(jnp.where(blk["incl"], gcx - blk["rowf"], NEG_BIG))
            eg = jnp.exp(gcx)
            a_sc[slot, idx] = jnp.where(blk["strict"], bx * blk["gram"] * decay, 0.0)
            in_sc[slot, idx] = jnp.where(blk["incl"], blk["qk"] * decay, 0.0).astype(BF16)
            qd_sc[slot, idx] = (q * eg).astype(BF16)
            kdt_sc[slot, idx] = (k * jnp.exp(dglx)).T.astype(BF16)
            vb_sc[slot, idx] = (v * bx).astype(BF16)
            kbg_sc[slot, idx] = (k * bx * eg).astype(BF16)
            for ci in range(nc):
                last = ci * DN_CHUNK + (DN_CHUNK - 1 if d == 0 else 0)
                egl_sc[slot, idx, ci * 8:(ci + 1) * 8, :] = jnp.broadcast_to(eg[last:last + 1, :], (8, 256))
            yield

    def matmul_part(slot, half):
        bdm01 = bd_ref[...]

        def prepare(group, phase):
            for ch in group:
                idx, sl = ch["idx"], ch["sl"]
                if phase == 0:
                    ch["ak"] = a_sc[slot, idx, sl, :]
                    ch["p"] = eyef[sl] - ch["ak"]
                if phase < 5:
                    akb = ch["ak"].astype(BF16)
                    lhs = akb if phase == 0 else jnp.concatenate([akb, ch["p"].astype(BF16)], axis=0)
                    res = jnp.dot(lhs, _bd(akb, bdm01), preferred_element_type=F32)
                    ch["ak"] = res[0:64]
                    if phase > 0:
                        ch["p"] = ch["p"] + res[64:128]
                elif phase == 5:
                    ch["p"] = ch["p"] + jnp.dot(ch["p"].astype(BF16), _bd(ch["ak"].astype(BF16), bdm01),
                                                preferred_element_type=F32)
                else:
                    rhs = jnp.concatenate([_bd(vb_sc[slot, idx, sl, :], bdm01),
                                           _bd(kbg_sc[slot, idx, sl, :], bdm01)], axis=1)
                    uw = jnp.dot(ch["p"].astype(BF16), rhs, preferred_element_type=F32)
                    ch["u"] = uw[:, 0:256]
                    ch["w"] = uw[:, 256:512].astype(BF16)

        def recur_a(group):
            for ch in group:
                ch["state"] = state_ref[ch["si"]]
                res = jnp.dot(jnp.concatenate([ch["w"], qd_sc[slot, ch["idx"], ch["sl"], :]], axis=0),
                              ch["state"].astype(BF16), preferred_element_type=F32)
                ch["v_new"] = (ch["u"] - res[0:64]).astype(BF16)
                ch["o_inter"] = res[64:128]

        def recur_b(group):
            for ch in group:
                v_new, idx, ci = ch["v_new"], ch["idx"], ch["ci"]
                ch["o_ref"][ch["bi"], ch["osl"], :] = (ch["o_inter"] + jnp.dot(
                    in_sc[slot, idx, ch["sl"], :], _bd(v_new, bdm01), preferred_element_type=F32)).astype(BF16)
                vpad = jnp.concatenate([v_new, zpad] if ci % 2 == 0 else [zpad, v_new], axis=0)
                kpair = kdt_sc[slot, idx, :, (ci // 2) * 128:(ci // 2 + 1) * 128]
                ds = jnp.dot(kpair, vpad, preferred_element_type=F32)
                egl = egl_sc[slot, idx, ci * 8:ci * 8 + 1, :]
                state_ref[ch["si"]] = ch["state"] * egl + jnp.where(bdm, ds, 0.0)

        groups = []
        for s in range(nc):
            group = []
            for bi in range(nb):
                for d in range(2):
                    ci = s if d == 0 else nc - 1 - s
                    row0 = (half if d == 0 else 1 - half) * tb + ci * DN_CHUNK
                    group.append(dict(idx=d * nb + bi, si=d * nb + bi, bi=bi, ci=ci,
                                      o_ref=(of_ref, ob_ref)[d],
                                      sl=slice(ci * DN_CHUNK, (ci + 1) * DN_CHUNK),
                                      osl=slice(row0, row0 + DN_CHUNK)))
            groups.append(group)
        for phase in range(7):
            prepare(groups[0], phase)
            yield
        for s in range(nc):
            nxt = groups[s + 1] if s + 1 < nc else []
            for piece in ((prepare, nxt, 0), (prepare, nxt, 1), (recur_a, groups[s]), (prepare, nxt, 2),
                          (prepare, nxt, 3), (recur_b, groups[s]), (prepare, nxt, 4), (prepare, nxt, 5),
                          (prepare, nxt, 6)):
                piece[0](*piece[1:])
                yield

    def interleave(first, second):
        streams = [first, second]
        while streams:
            for gen in list(streams):
                if next(gen, StopIteration) is StopIteration:
                    streams.remove(gen)

    @pl.when(step == 0)
    def _():
        state_ref[...] = jnp.zeros_like(state_ref)
        r = _iota((tb, tb), 0)
        c = _iota((tb, tb), 1)
        same = jnp.right_shift(r, 6) == jnp.right_shift(c, 6)
        src = _iota((128, 768), 0)
        col = _iota((128, 768), 1)
        for d in range(2):
            expand_ref[d] = _ones_where((src & 63) == jnp.left_shift(jnp.right_shift(col, 8), 3) + 4 * d
                                        + (jnp.right_shift(col, 6) & 3))
        ones_ref[...] = jnp.concatenate([_ones_where(same), _ones_where(same)], axis=1)
        bd_ref[...] = _ones_where(_bd_mask())
        for _ in prepare_block_set(0, 0):
            pass

    interleave(matmul_part(0, 0), prepare_block_set(1, 1))
    interleave(matmul_part(1, 1), prepare_block_set(2, 0))


def _dn_call(dq, dk, dv, bg, tb):
    b, s, _ = dq.shape
    n = s // tb
    n2 = n // 2

    def spec(w, index):
        return pl.BlockSpec((b, tb, w), lambda i: (0, index(i), 0))

    orders = [lambda j: j, lambda j: n - 1 - j]
    picks = [lambda i: 0, lambda i: 2 * i + 1, lambda i: jnp.minimum(2 * i + 2, n - 1)]
    in_specs, args = [], []
    for pick in picks:
        for order in orders:
            index = functools.partial(lambda i, pick, order: order(pick(i)), pick=pick, order=order)
            in_specs += [spec(256, index), spec(256, index), spec(256, index), spec(128, index)]
            args += [dq, dk, dv, bg]
    nblk = 2 * b
    return pl.pallas_call(
        functools.partial(_dn_kernel, nb=b, tb=tb),
        grid=(n2,), in_specs=in_specs,
        out_specs=[pl.BlockSpec((b, 2 * tb, 256), lambda i: (0, i, 0)),
                   pl.BlockSpec((b, 2 * tb, 256), lambda i: (0, n2 - 1 - i, 0))],
        out_shape=[jax.ShapeDtypeStruct((b, s, 256), BF16)] * 2,
        scratch_shapes=[pltpu.VMEM((nblk, 256, 256), F32), pltpu.VMEM((2, 128, 768), BF16),
                        pltpu.VMEM((tb, 2 * tb), BF16), pltpu.VMEM((256, 256), BF16),
                        pltpu.VMEM((2, nblk, tb, 256), F32), pltpu.VMEM((2, nblk, tb, 256), BF16),
                        pltpu.VMEM((2, nblk, tb, 256), BF16), pltpu.VMEM((2, nblk, tb, 256), BF16),
                        pltpu.VMEM((2, nblk, tb, 256), BF16), pltpu.VMEM((2, nblk, 256, tb), BF16),
                        pltpu.VMEM((2, nblk, tb // 8, 256), F32)],
        compiler_params=_cparams(("arbitrary",)), name="dn",
    )(*args)


def _attn_kernel(qt_ref, k_ref, vt_ref, o_ref, st_ref, *, tq, tk, n_q, n_kv, unroll, bounded):
    per_q = n_kv // unroll
    n_trips = n_q * per_q

    def scores(qi, j, slot):
        kj = k_ref[0, pl.ds(pl.multiple_of(j * tk, tk), tk), :]
        st = jnp.dot(kj, qt_ref[0, 0, qi], preferred_element_type=F32)
        st_ref[slot, :, 0:2 * tq] = st
        return None if bounded else jnp.max(st, axis=0, keepdims=True)

    def update(j, st, mx, m, acc):
        if bounded:
            return m, acc + jnp.dot(vt_ref[0, 0, j], jnp.exp2(st).astype(BF16), preferred_element_type=F32)
        m_new = jnp.maximum(m, mx)
        p = jnp.exp2(st - m_new).astype(BF16)
        alpha = jnp.exp2(m - m_new)
        return m_new, alpha * acc + jnp.dot(vt_ref[0, 0, j], p, preferred_element_type=F32)

    def body(t, carry):
        qi = t // per_q
        base = (t - qi * per_q) * unroll
        first = base == 0
        m = jnp.where(first, -jnp.inf, carry[0])
        acc = jnp.where(first, 0.0, carry[1])
        mx = carry[2]
        t_next = jnp.minimum(t + 1, n_trips - 1)
        qi_next = t_next // per_q
        base_next = (t_next - qi_next * per_q) * unroll
        for r in range(unroll):
            if r < unroll - 1:
                mx_next = scores(qi, base + r + 1, (r + 1) % 2)
            else:
                mx_next = scores(qi_next, base_next, 0)
            m, acc = update(base + r, st_ref[r % 2, :, 0:2 * tq], mx, m, acc)
            mx = mx if bounded else mx_next

        @pl.when(base == n_kv - unroll)
        def _():
            o = acc[0:HEAD] / acc[HEAD:HEAD + 1]
            ot = jnp.concatenate([o[:, 0:tq], o[:, tq:2 * tq]], axis=0)
            o_ref[0, pl.ds(pl.multiple_of(qi * tq, tq), tq), :] = ot.T.astype(BF16)

        return m, acc, mx

    m0 = jnp.full((1, 2 * tq), -jnp.inf, F32)
    a0 = jnp.zeros((HEAD + 16, 2 * tq), F32)
    mx0 = scores(0, 0, 0)
    lax.fori_loop(0, n_trips, body, (m0, a0, m0 if bounded else mx0))


def _attn_call(aqt, ak, avt, bounded):
    b, s, _ = ak.shape
    _, _, n_q, _, tq2 = aqt.shape
    _, _, n_kv, vrows, tk = avt.shape
    unroll = next(u for u in (32, 16, 8, 4, 2) if n_kv % u == 0)
    return pl.pallas_call(
        functools.partial(_attn_kernel, tq=tq2 // 2, tk=tk, n_q=n_q, n_kv=n_kv, unroll=unroll,
                          bounded=bounded),
        grid=(b, 2),
        in_specs=[pl.BlockSpec((1, 1, n_q, 128, tq2), lambda bi, g: (bi, g, 0, 0, 0)),
                  pl.BlockSpec((1, s, 128), lambda bi, g: (bi, 0, 0)),
                  pl.BlockSpec((1, 1, n_kv, vrows, tk), lambda bi, g: (bi, g, 0, 0, 0))],
        out_specs=pl.BlockSpec((1, s, 128), lambda bi, g: (bi, 0, g)),
        out_shape=jax.ShapeDtypeStruct((b, s, 256), BF16),
        scratch_shapes=[pltpu.VMEM((2, tk, tq2 + 128), F32)],
        compiler_params=_cparams(("parallel", "parallel")), name="attn",
    )(aqt, ak, avt)


def _out_kernel(x_ref, ya_ref, yd_ref, yc_ref, of_ref, ob_ref, gates_ref, dnw_ref, wo_ref, o_ref):
    o = of_ref[0].astype(F32) + ob_ref[0].astype(F32)
    gates = gates_ref[0].astype(F32)
    on = o * lax.rsqrt(_head_sum(o * o) * (1.0 / HEAD) + EPS) * dnw_ref[...]
    yb = (on * gates[:, 0:256]).astype(BF16)
    yc = (yc_ref[0].astype(F32) * gates[:, 256:512]).astype(BF16)
    mix = jnp.concatenate([ya_ref[0], yb, yc, yd_ref[0]], axis=1)
    o_ref[0] = x_ref[0] + jnp.dot(mix, wo_ref[...], preferred_element_type=F32)


def _out_call(x, ya, yd, yc, o_f, o_b, gates, lw, tm):
    b, s, _ = x.shape
    tok = lambda w: pl.BlockSpec((1, tm, w), lambda bi, i: (bi, i, 0))
    full = lambda shape: pl.BlockSpec(shape, lambda bi, i: (0,) * len(shape))
    return pl.pallas_call(
        _out_kernel, grid=(b, s // tm),
        in_specs=[tok(D_MODEL), tok(256), tok(256), tok(256), tok(256), tok(256), tok(512),
                  full((1, 256)), full((D_MODEL, D_MODEL))],
        out_specs=tok(D_MODEL), out_shape=jax.ShapeDtypeStruct((b, s, D_MODEL), F32),
        compiler_params=_cparams(("parallel", "parallel")), name="out",
    )(x, ya, yd, yc, o_f, o_b, gates, lw["dn_norm_w"], lw["w_out"])


def _rope_tables(seq_len):
    t = jnp.arange(seq_len)
    pos = jnp.stack([t // GRID_W, t % GRID_W], axis=-1).astype(F32)
    n_freq = HEAD // 4
    inv_freq = jnp.power(ROPE_THETA, -2.0 * jnp.arange(n_freq, dtype=F32) / (HEAD // 2))
    ang = pos[:, :, None] * inv_freq
    cos = jnp.repeat(jnp.cos(ang)[:, :, None, :], 2, axis=2).reshape(seq_len, HEAD)
    sin = jnp.sin(ang)
    sin = jnp.stack([-sin, sin], axis=2).reshape(seq_len, HEAD)
    return jnp.tile(cos, (1, 2)), jnp.tile(sin, (1, 2))


def _layer_weights(l, norm_w, w_in, sgu_w, sgu_b, conv_w, a_log, dt_bias, dn_norm_w, q_norm_w,
                   k_norm_w, pool_w, pool_scale, w_out):
    w = w_in[l]
    cols = lambda a, n: w[:, a:a + n]
    w_halo = jnp.concatenate([cols(_B_Q, 768), cols(_D_X, 256)], axis=1).astype(BF16)
    w_main = jnp.concatenate([
        cols(_A_U, 768), cols(_B_Z, 256), cols(_C_Q, 768), cols(_D_Z, 256), cols(_B_BETA, 16),
        jnp.zeros((D_MODEL, 112), F32)], axis=1).astype(BF16)
    dn_row = jnp.zeros((8, 128), F32)
    dn_row = dn_row.at[0, 8:16].set(a_log[l].reshape(8)).at[1, 8:16].set(dt_bias[l].reshape(8))
    pool_bd = jnp.zeros((256, 256), F32)
    for gi in range(len(POOL_WINDOWS)):
        pool_bd = pool_bd.at[gi * 64:(gi + 1) * 64, gi * 64:(gi + 1) * 64].set(pool_w[l, gi])
    return {
        "norm_w": norm_w[l].reshape(1, D_MODEL),
        "w_halo": w_halo, "w_main": w_main,
        "sgu_w": jnp.transpose(sgu_w[l], (1, 0, 2)).reshape(SGU_CHUNK, 4 * SGU_CHUNK).astype(BF16),
        "sgu_b": jnp.repeat(sgu_b[l].T, HEAD, axis=1),
        "conv_w": jnp.concatenate([conv_w[l], jnp.zeros((3, 768), F32)], axis=0),
        "dn_row": dn_row,
        "q_norm_w": jnp.tile(q_norm_w[l], 4).reshape(1, 256),
        "k_norm_w": jnp.tile(k_norm_w[l], 2).reshape(1, 128),
        "pool_w": pool_bd.astype(BF16),
        "pool_scale": pool_scale[l].reshape(1, 256),
        "dn_norm_w": jnp.tile(dn_norm_w[l], 4).reshape(1, 256),
        "score_bound": (HEAD * jnp.max(jnp.abs(q_norm_w[l])) * jnp.max(jnp.abs(k_norm_w[l]))
                        * (HEAD ** -0.5 * LOG2_E)),
        "w_out": w_out[l].astype(BF16),
    }


def _tiles(batch, seq_len):
    tb = min(DN_CHUNK * max(1, 8 // batch), seq_len)
    return dict(tm=min(512, seq_len), tb=tb, tq=min(256, seq_len))


def _layer(x, lw, rope_c, rope_s):
    t = _tiles(x.shape[0], x.shape[1])
    ya, yd, dq, dk, dv, bg, gates, aqt, ak, avt = _proj_call(x, lw, rope_c, rope_s, t["tm"], t["tq"])
    o_f, o_b = _dn_call(dq, dk, dv, bg, t["tb"])
    yc = lax.cond(lw["score_bound"] * SCORE_MARGIN <= SCORE_BOUND,
                  functools.partial(_attn_call, bounded=True),
                  functools.partial(_attn_call, bounded=False), aqt, ak, avt)
    return _out_call(x, ya, yd, yc, o_f, o_b, gates, lw, t["tm"])


def kernel(x_prompt, x_sample, norm_w, w_in, sgu_w, sgu_b, conv_w, a_log, dt_bias, dn_norm_w,
           q_norm_w, k_norm_w, pool_w, pool_scale, w_out):
    depth = norm_w.shape[0]
    rope_p = _rope_tables(x_prompt.shape[1])
    rope_s = _rope_tables(x_sample.shape[1])
    y_prompt, y_sample = x_prompt, x_sample
    for l in range(depth):
        lw = _layer_weights(l, norm_w, w_in, sgu_w, sgu_b, conv_w, a_log, dt_bias, dn_norm_w,
                            q_norm_w, k_norm_w, pool_w, pool_scale, w_out)
        y_prompt = _layer(y_prompt, lw, *rope_p)
        y_sample = _layer(y_sample, lw, *rope_s)
    return (y_prompt, y_sample)
```

```python
import functools

import jax
import jax.numpy as jnp
from jax import lax
from jax.experimental import pallas as pl
from jax.experimental.pallas import tpu as pltpu

F32 = jnp.float32
BF16 = jnp.bfloat16

D_MODEL = 1024
HEAD = 64
N_HEADS = 4
GRID_W = 64
EPS = 1e-6
SGU_CHUNK = 128
DN_CHUNK = 64
ROPE_THETA = 10000.0
POOL_WINDOWS = (2, 4, 8, 16)
HALO = 16
NEG_BIG = -1e30
LOG2_E = 1.4426950408889634
ATTN_TK = 512
SCORE_BOUND = 64.0
SCORE_MARGIN = 1.05

V7X_VMEM_LIMIT_BYTES = 56 * 1024 * 1024

_A_U = 0
_B_Q, _B_Z, _B_BETA = 768, 1536, 1792
_C_Q = 1808
_D_X, _D_Z = 2576, 2832
MAIN_COLS = 2176


def _cparams(semantics):
    return pltpu.CompilerParams(dimension_semantics=semantics,
                                vmem_limit_bytes=V7X_VMEM_LIMIT_BYTES)


def _split2(x):
    hi = x.astype(BF16)
    lo = (x - hi.astype(F32)).astype(BF16)
    return hi, lo


def _iota(shape, dim):
    return lax.broadcasted_iota(jnp.int32, shape, dim)


def _ones_where(cond):
    return jnp.where(cond, 1.0, 0.0).astype(BF16)


def _head_sum(x2):
    w = x2.shape[1]
    g = _ones_where(jnp.right_shift(_iota((w, w), 0), 6) == jnp.right_shift(_iota((w, w), 1), 6))
    return jnp.dot(x2.astype(BF16), g, preferred_element_type=F32)


def _silu(z):
    return 0.5 * z * (1.0 + jnp.tanh(0.5 * z))


def _bd_mask():
    return jnp.right_shift(_iota((256, 256), 0), 6) == jnp.right_shift(_iota((256, 256), 1), 6)


def _bd(x, mask01):
    z = jnp.zeros((HEAD, 128), x.dtype)
    blocks = []
    for h in range(N_HEADS):
        t = h // 2
        m = x[:, t * 128:(t + 1) * 128] * mask01[h * HEAD:(h + 1) * HEAD, t * 128:(t + 1) * 128]
        blocks.append(jnp.concatenate([m, z] if t == 0 else [z, m], axis=1))
    return jnp.concatenate(blocks, axis=0)


def _proj_kernel(xp_ref, xc_ref, xn_ref, nw_ref, wh_ref, wm_ref, sguw_ref, sgub_ref, convw_ref,
                 dnrow_ref, qw_ref, kw_ref, rc_ref, rs_ref, poolw_ref, pools_ref,
                 ya_ref, yd_ref, dq_ref, dk_ref, dv_ref, bg_ref, gates_ref, aqt_ref, ak_ref, avt_ref,
                 hext_ref, *, tm, tq, n_tiles, seq_len):
    i = pl.program_id(1)
    n_ext = tm + 2 * HALO
    nw = nw_ref[...]

    def norm(x):
        ms = jnp.mean(x * x, axis=-1, keepdims=True)
        return x * lax.rsqrt(ms + EPS) * nw

    hext_ref[0:HALO, :] = jnp.where(i > 0, norm(xp_ref[0]), 0.0).astype(BF16)
    hext_ref[HALO:HALO + tm, :] = norm(xc_ref[0]).astype(BF16)
    hext_ref[HALO + tm:n_ext, :] = jnp.where(i < n_tiles - 1, norm(xn_ref[0]), 0.0).astype(BF16)

    ph = jnp.dot(hext_ref[...], wh_ref[...], preferred_element_type=F32)
    pm = jnp.dot(hext_ref[HALO:HALO + tm, :], wm_ref[...], preferred_element_type=F32)

    def rows(x):
        return x[HALO:HALO + tm]

    def mixer_a():
        a_u, a_v, a_z = pm[:, 0:256], pm[:, 256:512], pm[:, 512:768]
        vn = a_v * lax.rsqrt(_head_sum(a_v * a_v) * (1.0 / HEAD) + EPS)
        mask4 = _ones_where(jnp.right_shift(_iota((512, 256), 0), 7) == jnp.right_shift(_iota((512, 256), 1), 6))
        yield
        mixed = []
        for c in range(tm // SGU_CHUNK):
            vc = vn[c * SGU_CHUNK:(c + 1) * SGU_CHUNK].astype(BF16)
            bdv = jnp.concatenate([vc, vc, vc, vc], axis=0) * mask4
            mixed.append(jnp.dot(sguw_ref[...], bdv, preferred_element_type=F32) + sgub_ref[...])
            yield
        mixed = jnp.concatenate(mixed, axis=0)
        ya_ref[0] = (a_u * mixed * _silu(a_z)).astype(BF16)
        yield

    def mixer_d():
        xd = ph[:, 768:1024]
        a1 = xd + pltpu.roll(xd, n_ext - 1, 0)
        a2 = a1 + pltpu.roll(a1, n_ext - 2, 0)
        yield
        a3 = a2 + pltpu.roll(a2, n_ext - 4, 0)
        a4 = a3 + pltpu.roll(a3, n_ext - 8, 0)
        yield
        w2 = rows(pltpu.roll(a1, 1, 0))
        w4 = rows(pltpu.roll(a2, 2, 0))
        w8 = rows(pltpu.roll(a3, 4, 0))
        w16 = rows(pltpu.roll(a4, 8, 0))
        yield
        grp = jnp.right_shift(_iota((tm, 256), 1), 6)
        half = jnp.left_shift(jnp.ones((tm, 256), jnp.int32), grp)
        t = i * tm + _iota((tm, 256), 0)
        cnt = (jnp.minimum(t + half, seq_len) - jnp.maximum(t - half, 0)).astype(F32)
        win = jnp.where(grp == 0, w2, jnp.where(grp == 1, w4, jnp.where(grp == 2, w8, w16)))
        diff = win / cnt - rows(xd)
        yield
        yd = jnp.dot(diff.astype(BF16), poolw_ref[...], preferred_element_type=F32) * pools_ref[...]
        yd_ref[0] = (yd * _silu(pm[:, 1792:2048])).astype(BF16)
        yield

    def mixer_b():
        xb = ph[:, 0:768]
        cw = convw_ref[...]
        conv = rows(pltpu.roll(xb, 2, 0)) * cw[0:1] + rows(pltpu.roll(xb, 1, 0)) * cw[1:2]
        yield
        conv = conv + rows(xb) * cw[2:3] + rows(pltpu.roll(xb, n_ext - 1, 0)) * cw[3:4]
        yield
        conv = conv + rows(pltpu.roll(xb, n_ext - 2, 0)) * cw[4:5]
        act = _silu(conv)
        yield
        bq, bk = act[:, 0:256], act[:, 256:512]
        dq_ref[0] = (bq * lax.rsqrt(_head_sum(bq * bq) + EPS) * (HEAD ** -0.5)).astype(BF16)
        yield
        dk_ref[0] = (bk * lax.rsqrt(_head_sum(bk * bk) + EPS)).astype(BF16)
        dv_ref[0] = act[:, 512:768].astype(BF16)
        yield
        ba = pm[:, 2048:2176]
        lane = _iota((tm, 128), 1)
        xa = ba + dnrow_ref[1:2, :]
        softplus = jnp.maximum(xa, 0.0) + jnp.log1p(jnp.exp(-jnp.abs(xa)))
        g = -jnp.exp(dnrow_ref[0:1, :]) * softplus
        bg_ref[0] = jnp.where(lane < 8, 1.0 / (1.0 + jnp.exp(-ba)), g)
        yield
        gates_ref[0] = jnp.concatenate([_silu(pm[:, 768:1024]), _silu(pm[:, 1536:1792])], axis=1).astype(BF16)
        yield

    def mixer_c():
        rc, rs = rc_ref[...], rs_ref[...]
        first = (_iota((tm, 128), 1) & 16) == 0

        def rope(x):
            sw = jnp.where(first, pltpu.roll(x, 112, 1), pltpu.roll(x, 16, 1))
            return x * rc + sw * rs

        cq, ck = pm[:, 1024:1280], pm[:, 1280:1408]
        qn = cq * lax.rsqrt(_head_sum(cq * cq) * (1.0 / HEAD) + EPS) * qw_ref[...]
        yield
        qr = jnp.concatenate([rope(qn[:, 0:128]), rope(qn[:, 128:256])], axis=1) * (HEAD ** -0.5 * LOG2_E)
        qt = qr.T.astype(BF16)
        yield
        zq = jnp.zeros((HEAD, 2 * tq), BF16)
        for r in range(tm // tq):
            cs = slice(r * tq, (r + 1) * tq)
            top0 = jnp.concatenate([qt[0:64, cs], qt[64:128, cs]], axis=1)
            top1 = jnp.concatenate([qt[128:192, cs], qt[192:256, cs]], axis=1)
            aqt_ref[0, 0, r] = jnp.concatenate([top0, zq], axis=0)
            aqt_ref[0, 1, r] = jnp.concatenate([zq, top1], axis=0)
        yield
        kn = ck * lax.rsqrt(_head_sum(ck * ck) * (1.0 / HEAD) + EPS) * kw_ref[...]
        ak_ref[0] = rope(kn).astype(BF16)
        yield
        vt = pm[:, 1408:1536].T
        ones = jnp.ones((16, tm), F32)
        for g in range(2):
            vg = jnp.concatenate([vt[g * HEAD:(g + 1) * HEAD], ones], axis=0).astype(BF16)
            for r in range(tm // ATTN_TK):
                avt_ref[0, g, r] = vg[:, r * ATTN_TK:(r + 1) * ATTN_TK]
        yield

    streams = [mixer_b(), mixer_a(), mixer_c(), mixer_d()]
    while streams:
        for gen in list(streams):
            if next(gen, StopIteration) is StopIteration:
                streams.remove(gen)


def _proj_call(x, lw, rope_c, rope_s, tm, tq):
    b, s, _ = x.shape
    n_tiles = s // tm
    hb = tm // HALO
    last_hb = s // HALO - 1
    full = lambda shape: pl.BlockSpec(shape, lambda bi, i: (0,) * len(shape))
    tok = lambda w: pl.BlockSpec((1, tm, w), lambda bi, i: (bi, i, 0))
    in_specs = [
        pl.BlockSpec((1, HALO, D_MODEL), lambda bi, i: (bi, jnp.maximum(i * hb - 1, 0), 0)),
        pl.BlockSpec((1, tm, D_MODEL), lambda bi, i: (bi, i, 0)),
        pl.BlockSpec((1, HALO, D_MODEL), lambda bi, i: (bi, jnp.minimum((i + 1) * hb, last_hb), 0)),
        full((1, D_MODEL)), full((D_MODEL, 1024)), full((D_MODEL, MAIN_COLS)),
        full((128, 512)), full((128, 256)), full((8, 768)), full((8, 128)),
        full((1, 256)), full((1, 128)),
        pl.BlockSpec((tm, 128), lambda bi, i: (i, 0)), pl.BlockSpec((tm, 128), lambda bi, i: (i, 0)),
        full((256, 256)), full((1, 256)),
    ]
    out_shape = [
        jax.ShapeDtypeStruct((b, s, 256), BF16),
        jax.ShapeDtypeStruct((b, s, 256), BF16),
        jax.ShapeDtypeStruct((b, s, 256), BF16),
        jax.ShapeDtypeStruct((b, s, 256), BF16),
        jax.ShapeDtypeStruct((b, s, 256), BF16),
        jax.ShapeDtypeStruct((b, s, 128), F32),
        jax.ShapeDtypeStruct((b, s, 512), BF16),
        jax.ShapeDtypeStruct((b, 2, s // tq, 128, 2 * tq), BF16),
        jax.ShapeDtypeStruct((b, s, 128), BF16),
        jax.ShapeDtypeStruct((b, 2, s // ATTN_TK, HEAD + 16, ATTN_TK), BF16),
    ]
    out_specs = [tok(256), tok(256), tok(256), tok(256), tok(256), tok(128), tok(512),
                 pl.BlockSpec((1, 2, tm // tq, 128, 2 * tq), lambda bi, i: (bi, 0, i, 0, 0)), tok(128),
                 pl.BlockSpec((1, 2, tm // ATTN_TK, HEAD + 16, ATTN_TK), lambda bi, i: (bi, 0, i, 0, 0))]
    return pl.pallas_call(
        functools.partial(_proj_kernel, tm=tm, tq=tq, n_tiles=n_tiles, seq_len=s),
        grid=(b, n_tiles), in_specs=in_specs, out_specs=out_specs, out_shape=out_shape,
        scratch_shapes=[pltpu.VMEM((tm + 2 * HALO, D_MODEL), BF16)],
        compiler_params=_cparams(("parallel", "parallel")), name="proj",
    )(x, x, x, lw["norm_w"], lw["w_halo"], lw["w_main"], lw["sgu_w"], lw["sgu_b"], lw["conv_w"],
      lw["dn_row"], lw["q_norm_w"], lw["k_norm_w"], rope_c, rope_s, lw["pool_w"], lw["pool_scale"])


def _dn_kernel(*refs, nb, tb):
    ins, (of_ref, ob_ref) = refs[:24], refs[24:26]
    (state_ref, expand_ref, ones_ref, bd_ref,
     a_sc, vb_sc, kbg_sc, in_sc, qd_sc, kdt_sc, egl_sc) = refs[26:]
    nc = tb // DN_CHUNK
    step = pl.program_id(0)

    li = _iota((tb, 256), 0) & 63
    lj = _iota((tb, 256), 1) & 63
    eyecat = li == lj
    eyef = jnp.where(eyecat, 1.0, 0.0)
    bdm = _bd_mask()
    lane = _iota((tb, 128), 1)
    row64 = _iota((tb, 128), 0) & 63
    zpad = jnp.zeros((DN_CHUNK, 256), BF16)

    def prepare_block_set(which, slot):
        j2 = ones_ref[...]
        bdm01 = bd_ref[...]
        blocks = []
        for d in range(2):
            q_ref, k_ref, v_ref, bg_ref = ins[(which * 2 + d) * 4:(which * 2 + d) * 4 + 4]
            for bi in range(nb):
                blocks.append(dict(d=d, bi=bi, idx=d * nb + bi, q_ref=q_ref, k_ref=k_ref, v_ref=v_ref,
                                   bg=bg_ref[bi], expand=expand_ref[d],
                                   incl=(li >= lj) if d == 0 else (li <= lj),
                                   strict=(li > lj) if d == 0 else (li < lj)))
        for blk in blocks:
            c = blk["bg"]
            for s in (1, 2, 4, 8, 16, 32):
                c = c + jnp.where(row64 >= s, pltpu.roll(c, s, 0), 0.0)
            tot = jnp.concatenate([jnp.broadcast_to(c[ci * DN_CHUNK + DN_CHUNK - 1:(ci + 1) * DN_CHUNK, :],
                                                    (DN_CHUNK, 128)) for ci in range(nc)], axis=0)
            if blk["d"] == 0:
                blk["cs"] = (c, tot - c)
            else:
                blk["cs"] = (tot - c + blk["bg"], c - blk["bg"])
            yield
        for blk in blocks:
            cs_in, cs_st = blk["cs"]
            nbx = jnp.where(lane < 8, blk["bg"], jnp.where(lane < 16, cs_in, pltpu.roll(cs_st, 8, 1)))
            hi = nbx.astype(BF16)
            lo = pltpu.roll(nbx - hi.astype(F32), 64, 1)
            packed = jnp.where(lane < 64, hi.astype(F32), lo).astype(BF16)
            blk["x"] = jnp.dot(packed, blk["expand"], preferred_element_type=F32)
            yield
        for blk in blocks:
            gcx = blk["x"][:, 256:512]
            blk["rowf"] = jnp.dot(j2, jnp.concatenate(_split2(jnp.where(eyecat, gcx, 0.0)), axis=0),
                                  preferred_element_type=F32)
            yield
        for blk in blocks:
            gram, qk = [], []
            for ci in range(nc):
                sl = slice(ci * DN_CHUNK, (ci + 1) * DN_CHUNK)
                kc = blk["k_ref"][blk["bi"], sl, :]
                gq = lax.dot_general(jnp.concatenate([kc, blk["q_ref"][blk["bi"], sl, :]], axis=0),
                                     _bd(kc, bdm01), (((1,), (1,)), ((), ())),
                                     preferred_element_type=F32)
                gram.append(gq[0:64])
                qk.append(gq[64:128])
            blk["gram"] = jnp.concatenate(gram, axis=0)
            blk["qk"] = jnp.concatenate(qk, axis=0)
            yield
        for blk in blocks:
            d, bi, idx = blk["d"], blk["bi"], blk["idx"]
            q = blk["q_ref"][bi].astype(F32)
            k = blk["k_ref"][bi].astype(F32)
            v = blk["v_ref"][bi].astype(F32)
            x = blk["x"]
            bx, gcx, dglx = x[:, 0:256], x[:, 256:512], x[:, 512:768]
            decay = jnp.exp(jnp.where(blk["incl"], gcx - blk["rowf"], NEG_BIG))
            eg = jnp.exp(gcx)
            a_sc[slot, idx] = jnp.where(blk["strict"], bx * blk["gram"] * decay, 0.0)
            in_sc[slot, idx] = jnp.where(blk["incl"], blk["qk"] * decay, 0.0).astype(BF16)
            qd_sc[slot, idx] = (q * eg).astype(BF16)
            kdt_sc[slot, idx] = (k * jnp.exp(dglx)).T.astype(BF16)
            vb_sc[slot, idx] = (v * bx).astype(BF16)
            kbg_sc[slot, idx] = (k * bx * eg).astype(BF16)
            for ci in range(nc):
                last = ci * DN_CHUNK + (DN_CHUNK - 1 if d == 0 else 0)
                egl_sc[slot, idx, ci * 8:(ci + 1) * 8, :] = jnp.broadcast_to(eg[last:last + 1, :], (8, 256))
            yield

    def matmul_part(slot, half):
        bdm01 = bd_ref[...]

        def prepare(group, phase):
            for ch in group:
                idx, sl = ch["idx"], ch["sl"]
                if phase == 0:
                    ch["ak"] = a_sc[slot, idx, sl, :]
                    ch["p"] = eyef[sl] - ch["ak"]
                if phase < 5:
                    akb = ch["ak"].astype(BF16)
                    lhs = akb if phase == 0 else jnp.concatenate([akb, ch["p"].astype(BF16)], axis=0)
                    res = jnp.dot(lhs, _bd(akb, bdm01), preferred_element_type=F32)
                    ch["ak"] = res[0:64]
                    if phase > 0:
                        ch["p"] = ch["p"] + res[64:128]
                elif phase == 5:
                    ch["p"] = ch["p"] + jnp.dot(ch["p"].astype(BF16), _bd(ch["ak"].astype(BF16), bdm01),
                                                preferred_element_type=F32)
                else:
                    rhs = jnp.concatenate([_bd(vb_sc[slot, idx, sl, :], bdm01),
                                           _bd(kbg_sc[slot, idx, sl, :], bdm01)], axis=1)
                    uw = jnp.dot(ch["p"].astype(BF16), rhs, preferred_element_type=F32)
                    ch["u"] = uw[:, 0:256]
                    ch["w"] = uw[:, 256:512].astype(BF16)

        def recur_a(group):
            for ch in group:
                ch["state"] = state_ref[ch["si"]]
                res = jnp.dot(jnp.concatenate([ch["w"], qd_sc[slot, ch["idx"], ch["sl"], :]], axis=0),
                              ch["state"].astype(BF16), preferred_element_type=F32)
                ch["v_new"] = (ch["u"] - res[0:64]).astype(BF16)
                ch["o_inter"] = res[64:128]

        def recur_b(group):
            for ch in group:
                v_new, idx, ci = ch["v_new"], ch["idx"], ch["ci"]
                ch["o_ref"][ch["bi"], ch["osl"], :] = (ch["o_inter"] + jnp.dot(
                    in_sc[slot, idx, ch["sl"], :], _bd(v_new, bdm01), preferred_element_type=F32)).astype(BF16)
                vpad = jnp.concatenate([v_new, zpad] if ci % 2 == 0 else [zpad, v_new], axis=0)
                kpair = kdt_sc[slot, idx, :, (ci // 2) * 128:(ci // 2 + 1) * 128]
                ds = jnp.dot(kpair, vpad, preferred_element_type=F32)
                egl = egl_sc[slot, idx, ci * 8:ci * 8 + 1, :]
                state_ref[ch["si"]] = ch["state"] * egl + jnp.where(bdm, ds, 0.0)

        groups = []
        for s in range(nc):
            group = []
            for bi in range(nb):
                for d in range(2):
                    ci = s if d == 0 else nc - 1 - s
                    row0 = (half if d == 0 else 1 - half) * tb + ci * DN_CHUNK
                    group.append(dict(idx=d * nb + bi, si=d * nb + bi, bi=bi, ci=ci,
                                      o_ref=(of_ref, ob_ref)[d],
                                      sl=slice(ci * DN_CHUNK, (ci + 1) * DN_CHUNK),
                                      osl=slice(row0, row0 + DN_CHUNK)))
            groups.append(group)
        for phase in range(7):
            prepare(groups[0], phase)
            yield
        for s in range(nc):
            nxt = groups[s + 1] if s + 1 < nc else []
            for piece in ((prepare, nxt, 0), (prepare, nxt, 1), (recur_a, groups[s]), (prepare, nxt, 2),
                          (prepare, nxt, 3), (recur_b, groups[s]), (prepare, nxt, 4), (prepare, nxt, 5),
                          (prepare, nxt, 6)):
                piece[0](*piece[1:])
                yield

    def interleave(first, second):
        streams = [first, second]
        while streams:
            for gen in list(streams):
                if next(gen, StopIteration) is StopIteration:
                    streams.remove(gen)

    @pl.when(step == 0)
    def _():
        state_ref[...] = jnp.zeros_like(state_ref)
        r = _iota((tb, tb), 0)
        c = _iota((tb, tb), 1)
        same = jnp.right_shift(r, 6) == jnp.right_shift(c, 6)
        src = _iota((128, 768), 0)
        col = _iota((128, 768), 1)
        for d in range(2):
            expand_ref[d] = _ones_where((src & 63) == jnp.left_shift(jnp.right_shift(col, 8), 3) + 4 * d
                                        + (jnp.right_shift(col, 6) & 3))
        ones_ref[...] = jnp.concatenate([_ones_where(same), _ones_where(same)], axis=1)
        bd_ref[...] = _ones_where(_bd_mask())
        for _ in prepare_block_set(0, 0):
            pass

    interleave(matmul_part(0, 0), prepare_block_set(1, 1))
    interleave(matmul_part(1, 1), prepare_block_set(2, 0))


def _dn_call(dq, dk, dv, bg, tb):
    b, s, _ = dq.shape
    n = s // tb
    n2 = n // 2

    def spec(w, index):
        return pl.BlockSpec((b, tb, w), lambda i: (0, index(i), 0))

    orders = [lambda j: j, lambda j: n - 1 - j]
    picks = [lambda i: 0, lambda i: 2 * i + 1, lambda i: jnp.minimum(2 * i + 2, n - 1)]
    in_specs, args = [], []
    for pick in picks:
        for order in orders:
            index = functools.partial(lambda i, pick, order: order(pick(i)), pick=pick, order=order)
            in_specs += [spec(256, index), spec(256, index), spec(256, index), spec(128, index)]
            args += [dq, dk, dv, bg]
    nblk = 2 * b
    return pl.pallas_call(
        functools.partial(_dn_kernel, nb=b, tb=tb),
        grid=(n2,), in_specs=in_specs,
        out_specs=[pl.BlockSpec((b, 2 * tb, 256), lambda i: (0, i, 0)),
                   pl.BlockSpec((b, 2 * tb, 256), lambda i: (0, n2 - 1 - i, 0))],
        out_shape=[jax.ShapeDtypeStruct((b, s, 256), BF16)] * 2,
        scratch_shapes=[pltpu.VMEM((nblk, 256, 256), F32), pltpu.VMEM((2, 128, 768), BF16),
                        pltpu.VMEM((tb, 2 * tb), BF16), pltpu.VMEM((256, 256), BF16),
                        pltpu.VMEM((2, nblk, tb, 256), F32), pltpu.VMEM((2, nblk, tb, 256), BF16),
                        pltpu.VMEM((2, nblk, tb, 256), BF16), pltpu.VMEM((2, nblk, tb, 256), BF16),
                        pltpu.VMEM((2, nblk, tb, 256), BF16), pltpu.VMEM((2, nblk, 256, tb), BF16),
                        pltpu.VMEM((2, nblk, tb // 8, 256), F32)],
        compiler_params=_cparams(("arbitrary",)), name="dn",
    )(*args)


def _attn_kernel(qt_ref, k_ref, vt_ref, o_ref, st_ref, *, tq, tk, n_q, n_kv, unroll, bounded):
    per_q = n_kv // unroll
    n_trips = n_q * per_q

    def scores(qi, j, slot):
        kj = k_ref[0, pl.ds(pl.multiple_of(j * tk, tk), tk), :]
        st = jnp.dot(kj, qt_ref[0, 0, qi], preferred_element_type=F32)
        st_ref[slot, :, 0:2 * tq] = st
        return None if bounded else jnp.max(st, axis=0, keepdims=True)

    def update(j, st, mx, m, acc):
        if bounded:
            return m, acc + jnp.dot(vt_ref[0, 0, j], jnp.exp2(st).astype(BF16), preferred_element_type=F32)
        m_new = jnp.maximum(m, mx)
        p = jnp.exp2(st - m_new).astype(BF16)
        alpha = jnp.exp2(m - m_new)
        return m_new, alpha * acc + jnp.dot(vt_ref[0, 0, j], p, preferred_element_type=F32)

    def body(t, carry):
        qi = t // per_q
        base = (t - qi * per_q) * unroll
        first = base == 0
        m = jnp.where(first, -jnp.inf, carry[0])
        acc = jnp.where(first, 0.0, carry[1])
        mx = carry[2]
        t_next = jnp.minimum(t + 1, n_trips - 1)
        qi_next = t_next // per_q
        base_next = (t_next - qi_next * per_q) * unroll
        for r in range(unroll):
            if r < unroll - 1:
                mx_next = scores(qi, base + r + 1, (r + 1) % 2)
            else:
                mx_next = scores(qi_next, base_next, 0)
            m, acc = update(base + r, st_ref[r % 2, :, 0:2 * tq], mx, m, acc)
            mx = mx if bounded else mx_next

        @pl.when(base == n_kv - unroll)
        def _():
            o = acc[0:HEAD] / acc[HEAD:HEAD + 1]
            ot = jnp.concatenate([o[:, 0:tq], o[:, tq:2 * tq]], axis=0)
            o_ref[0, pl.ds(pl.multiple_of(qi * tq, tq), tq), :] = ot.T.astype(BF16)

        return m, acc, mx

    m0 = jnp.full((1, 2 * tq), -jnp.inf, F32)
    a0 = jnp.zeros((HEAD + 16, 2 * tq), F32)
    mx0 = scores(0, 0, 0)
    lax.fori_loop(0, n_trips, body, (m0, a0, m0 if bounded else mx0))


def _attn_call(aqt, ak, avt, bounded):
    b, s, _ = ak.shape
    _, _, n_q, _, tq2 = aqt.shape
    _, _, n_kv, vrows, tk = avt.shape
    unroll = next(u for u in (32, 16, 8, 4, 2) if n_kv % u == 0)
    return pl.pallas_call(
        functools.partial(_attn_kernel, tq=tq2 // 2, tk=tk, n_q=n_q, n_kv=n_kv, unroll=unroll,
                          bounded=bounded),
        grid=(b, 2),
        in_specs=[pl.BlockSpec((1, 1, n_q, 128, tq2), lambda bi, g: (bi, g, 0, 0, 0)),
                  pl.BlockSpec((1, s, 128), lambda bi, g: (bi, 0, 0)),
                  pl.BlockSpec((1, 1, n_kv, vrows, tk), lambda bi, g: (bi, g, 0, 0, 0))],
        out_specs=pl.BlockSpec((1, s, 128), lambda bi, g: (bi, 0, g)),
        out_shape=jax.ShapeDtypeStruct((b, s, 256), BF16),
        scratch_shapes=[pltpu.VMEM((2, tk, tq2 + 128), F32)],
        compiler_params=_cparams(("parallel", "parallel")), name="attn",
    )(aqt, ak, avt)


def _out_kernel(x_ref, ya_ref, yd_ref, yc_ref, of_ref, ob_ref, gates_ref, dnw_ref, wo_ref, o_ref):
    o = of_ref[0].astype(F32) + ob_ref[0].astype(F32)
    gates = gates_ref[0].astype(F32)
    on = o * lax.rsqrt(_head_sum(o * o) * (1.0 / HEAD) + EPS) * dnw_ref[...]
    yb = (on * gates[:, 0:256]).astype(BF16)
    yc = (yc_ref[0].astype(F32) * gates[:, 256:512]).astype(BF16)
    mix = jnp.concatenate([ya_ref[0], yb, yc, yd_ref[0]], axis=1)
    o_ref[0] = x_ref[0] + jnp.dot(mix, wo_ref[...], preferred_element_type=F32)


def _out_call(x, ya, yd, yc, o_f, o_b, gates, lw, tm):
    b, s, _ = x.shape
    tok = lambda w: pl.BlockSpec((1, tm, w), lambda bi, i: (bi, i, 0))
    full = lambda shape: pl.BlockSpec(shape, lambda bi, i: (0,) * len(shape))
    return pl.pallas_call(
        _out_kernel, grid=(b, s // tm),
        in_specs=[tok(D_MODEL), tok(256), tok(256), tok(256), tok(256), tok(256), tok(512),
                  full((1, 256)), full((D_MODEL, D_MODEL))],
        out_specs=tok(D_MODEL), out_shape=jax.ShapeDtypeStruct((b, s, D_MODEL), F32),
        compiler_params=_cparams(("parallel", "parallel")), name="out",
    )(x, ya, yd, yc, o_f, o_b, gates, lw["dn_norm_w"], lw["w_out"])


def _rope_tables(seq_len):
    t = jnp.arange(seq_len)
    pos = jnp.stack([t // GRID_W, t % GRID_W], axis=-1).astype(F32)
    n_freq = HEAD // 4
    inv_freq = jnp.power(ROPE_THETA, -2.0 * jnp.arange(n_freq, dtype=F32) / (HEAD // 2))
    ang = pos[:, :, None] * inv_freq
    cos = jnp.repeat(jnp.cos(ang)[:, :, None, :], 2, axis=2).reshape(seq_len, HEAD)
    sin = jnp.sin(ang)
    sin = jnp.stack([-sin, sin], axis=2).reshape(seq_len, HEAD)
    return jnp.tile(cos, (1, 2)), jnp.tile(sin, (1, 2))


def _layer_weights(l, norm_w, w_in, sgu_w, sgu_b, conv_w, a_log, dt_bias, dn_norm_w, q_norm_w,
                   k_norm_w, pool_w, pool_scale, w_out):
    w = w_in[l]
    cols = lambda a, n: w[:, a:a + n]
    w_halo = jnp.concatenate([cols(_B_Q, 768), cols(_D_X, 256)], axis=1).astype(BF16)
    w_main = jnp.concatenate([
        cols(_A_U, 768), cols(_B_Z, 256), cols(_C_Q, 768), cols(_D_Z, 256), cols(_B_BETA, 16),
        jnp.zeros((D_MODEL, 112), F32)], axis=1).astype(BF16)
    dn_row = jnp.zeros((8, 128), F32)
    dn_row = dn_row.at[0, 8:16].set(a_log[l].reshape(8)).at[1, 8:16].set(dt_bias[l].reshape(8))
    pool_bd = jnp.zeros((256, 256), F32)
    for gi in range(len(POOL_WINDOWS)):
        pool_bd = pool_bd.at[gi * 64:(gi + 1) * 64, gi * 64:(gi + 1) * 64].set(pool_w[l, gi])
    return {
        "norm_w": norm_w[l].reshape(1, D_MODEL),
        "w_halo": w_halo, "w_main": w_main,
        "sgu_w": jnp.transpose(sgu_w[l], (1, 0, 2)).reshape(SGU_CHUNK, 4 * SGU_CHUNK).astype(BF16),
        "sgu_b": jnp.repeat(sgu_b[l].T, HEAD, axis=1),
        "conv_w": jnp.concatenate([conv_w[l], jnp.zeros((3, 768), F32)], axis=0),
        "dn_row": dn_row,
        "q_norm_w": jnp.tile(q_norm_w[l], 4).reshape(1, 256),
        "k_norm_w": jnp.tile(k_norm_w[l], 2).reshape(1, 128),
        "pool_w": pool_bd.astype(BF16),
        "pool_scale": pool_scale[l].reshape(1, 256),
        "dn_norm_w": jnp.tile(dn_norm_w[l], 4).reshape(1, 256),
        "score_bound": (HEAD * jnp.max(jnp.abs(q_norm_w[l])) * jnp.max(jnp.abs(k_norm_w[l]))
                        * (HEAD ** -0.5 * LOG2_E)),
        "w_out": w_out[l].astype(BF16),
    }


def _tiles(batch, seq_len):
    tb = min(DN_CHUNK * max(1, 8 // batch), seq_len)
    return dict(tm=min(512, seq_len), to=min(1024, seq_len), tb=tb, tq=min(256, seq_len))


def _layer(x, lw, rope_c, rope_s):
    t = _tiles(x.shape[0], x.shape[1])
    ya, yd, dq, dk, dv, bg, gates, aqt, ak, avt = _proj_call(x, lw, rope_c, rope_s, t["tm"], t["tq"])
    o_f, o_b = _dn_call(dq, dk, dv, bg, t["tb"])
    yc = lax.cond(lw["score_bound"] * SCORE_MARGIN <= SCORE_BOUND,
                  functools.partial(_attn_call, bounded=True),
                  functools.partial(_attn_call, bounded=False), aqt, ak, avt)
    return _out_call(x, ya, yd, yc, o_f, o_b, gates, lw, t["to"])


def kernel(x_prompt, x_sample, norm_w, w_in, sgu_w, sgu_b, conv_w, a_log, dt_bias, dn_norm_w,
           q_norm_w, k_norm_w, pool_w, pool_scale, w_out):
    depth = norm_w.shape[0]
    rope_p = _rope_tables(x_prompt.shape[1])
    rope_s = _rope_tables(x_sample.shape[1])
    y_prompt, y_sample = x_prompt, x_sample
    for l in range(depth):
        lw = _layer_weights(l, norm_w, w_in, sgu_w, sgu_b, conv_w, a_log, dt_bias, dn_norm_w,
                            q_norm_w, k_norm_w, pool_w, pool_scale, w_out)
        y_prompt = _layer(y_prompt, lw, *rope_p)
        y_sample = _layer(y_sample, lw, *rope_s)
    return (y_prompt, y_sample)
```

```python
import functools

import jax
import jax.numpy as jnp
from jax import lax
from jax.experimental import pallas as pl
from jax.experimental.pallas import tpu as pltpu

F32 = jnp.float32
BF16 = jnp.bfloat16

D_MODEL = 1024
HEAD = 64
N_HEADS = 4
GRID_W = 64
EPS = 1e-6
SGU_CHUNK = 128
DN_CHUNK = 64
ROPE_THETA = 10000.0
POOL_WINDOWS = (2, 4, 8, 16)
HALO = 16
NEG_BIG = -1e30
LOG2_E = 1.4426950408889634
ATTN_TK = 512
SCORE_BOUND = 64.0
SCORE_MARGIN = 1.05

V7X_VMEM_LIMIT_BYTES = 56 * 1024 * 1024

_A_U = 0
_B_Q, _B_Z, _B_BETA = 768, 1536, 1792
_C_Q = 1808
_D_X, _D_Z = 2576, 2832
MAIN_COLS = 2176


def _cparams(semantics):
    return pltpu.CompilerParams(dimension_semantics=semantics,
                                vmem_limit_bytes=V7X_VMEM_LIMIT_BYTES)


def _split2(x):
    hi = x.astype(BF16)
    lo = (x - hi.astype(F32)).astype(BF16)
    return hi, lo


def _iota(shape, dim):
    return lax.broadcasted_iota(jnp.int32, shape, dim)


def _ones_where(cond):
    return jnp.where(cond, 1.0, 0.0).astype(BF16)


def _head_sum(x2):
    w = x2.shape[1]
    g = _ones_where(jnp.right_shift(_iota((w, w), 0), 6) == jnp.right_shift(_iota((w, w), 1), 6))
    return jnp.dot(x2.astype(BF16), g, preferred_element_type=F32)


def _silu(z):
    return 0.5 * z * (1.0 + jnp.tanh(0.5 * z))


def _bd_mask():
    return jnp.right_shift(_iota((256, 256), 0), 6) == jnp.right_shift(_iota((256, 256), 1), 6)


def _bd(x, mask01):
    z = jnp.zeros((HEAD, 128), x.dtype)
    blocks = []
    for h in range(N_HEADS):
        t = h // 2
        m = x[:, t * 128:(t + 1) * 128] * mask01[h * HEAD:(h + 1) * HEAD, t * 128:(t + 1) * 128]
        blocks.append(jnp.concatenate([m, z] if t == 0 else [z, m], axis=1))
    return jnp.concatenate(blocks, axis=0)


def _proj_kernel(xp_ref, xc_ref, xn_ref, nw_ref, wh_ref, wm_ref, sguw_ref, sgub_ref, convw_ref,
                 dnrow_ref, qw_ref, kw_ref, rc_ref, rs_ref, poolw_ref, pools_ref,
                 ya_ref, yd_ref, dq_ref, dk_ref, dv_ref, bg_ref, gates_ref, aqt_ref, ak_ref, avt_ref,
                 hext_ref, *, tm, tq, n_tiles, seq_len):
    i = pl.program_id(1)
    n_ext = tm + 2 * HALO
    nw = nw_ref[...]

    def norm(x):
        ms = jnp.mean(x * x, axis=-1, keepdims=True)
        return x * lax.rsqrt(ms + EPS) * nw

    hext_ref[0:HALO, :] = jnp.where(i > 0, norm(xp_ref[0]), 0.0).astype(BF16)
    hext_ref[HALO:HALO + tm, :] = norm(xc_ref[0]).astype(BF16)
    hext_ref[HALO + tm:n_ext, :] = jnp.where(i < n_tiles - 1, norm(xn_ref[0]), 0.0).astype(BF16)

    ph = jnp.dot(hext_ref[...], wh_ref[...], preferred_element_type=F32)
    pm = jnp.dot(hext_ref[HALO:HALO + tm, :], wm_ref[...], preferred_element_type=F32)

    def rows(x):
        return x[HALO:HALO + tm]

    def mixer_a():
        a_u, a_v, a_z = pm[:, 0:256], pm[:, 256:512], pm[:, 512:768]
        vn = a_v * lax.rsqrt(_head_sum(a_v * a_v) * (1.0 / HEAD) + EPS)
        mask4 = _ones_where(jnp.right_shift(_iota((512, 256), 0), 7) == jnp.right_shift(_iota((512, 256), 1), 6))
        yield
        mixed = []
        for c in range(tm // SGU_CHUNK):
            vc = vn[c * SGU_CHUNK:(c + 1) * SGU_CHUNK].astype(BF16)
            bdv = jnp.concatenate([vc, vc, vc, vc], axis=0) * mask4
            mixed.append(jnp.dot(sguw_ref[...], bdv, preferred_element_type=F32) + sgub_ref[...])
            yield
        mixed = jnp.concatenate(mixed, axis=0)
        ya_ref[0] = (a_u * mixed * _silu(a_z)).astype(BF16)
        yield

    def mixer_d():
        xd = ph[:, 768:1024]
        a1 = xd + pltpu.roll(xd, n_ext - 1, 0)
        a2 = a1 + pltpu.roll(a1, n_ext - 2, 0)
        yield
        a3 = a2 + pltpu.roll(a2, n_ext - 4, 0)
        a4 = a3 + pltpu.roll(a3, n_ext - 8, 0)
        yield
        w2 = rows(pltpu.roll(a1, 1, 0))
        w4 = rows(pltpu.roll(a2, 2, 0))
        w8 = rows(pltpu.roll(a3, 4, 0))
        w16 = rows(pltpu.roll(a4, 8, 0))
        yield
        grp = jnp.right_shift(_iota((tm, 256), 1), 6)
        half = jnp.left_shift(jnp.ones((tm, 256), jnp.int32), grp)
        t = i * tm + _iota((tm, 256), 0)
        cnt = (jnp.minimum(t + half, seq_len) - jnp.maximum(t - half, 0)).astype(F32)
        win = jnp.where(grp == 0, w2, jnp.where(grp == 1, w4, jnp.where(grp == 2, w8, w16)))
        diff = win / cnt - rows(xd)
        yield
        yd = jnp.dot(diff.astype(BF16), poolw_ref[...], preferred_element_type=F32) * pools_ref[...]
        yd_ref[0] = (yd * _silu(pm[:, 1792:2048])).astype(BF16)
        yield

    def mixer_b():
        xb = ph[:, 0:768]
        cw = convw_ref[...]
        conv = rows(pltpu.roll(xb, 2, 0)) * cw[0:1] + rows(pltpu.roll(xb, 1, 0)) * cw[1:2]
        yield
        conv = conv + rows(xb) * cw[2:3] + rows(pltpu.roll(xb, n_ext - 1, 0)) * cw[3:4]
        yield
        conv = conv + rows(pltpu.roll(xb, n_ext - 2, 0)) * cw[4:5]
        act = _silu(conv)
        yield
        bq, bk = act[:, 0:256], act[:, 256:512]
        dq_ref[0] = (bq * lax.rsqrt(_head_sum(bq * bq) + EPS) * (HEAD ** -0.5)).astype(BF16)
        yield
        dk_ref[0] = (bk * lax.rsqrt(_head_sum(bk * bk) + EPS)).astype(BF16)
        dv_ref[0] = act[:, 512:768].astype(BF16)
        yield
        ba = pm[:, 2048:2176]
        lane = _iota((tm, 128), 1)
        xa = ba + dnrow_ref[1:2, :]
        softplus = jnp.maximum(xa, 0.0) + jnp.log1p(jnp.exp(-jnp.abs(xa)))
        g = -jnp.exp(dnrow_ref[0:1, :]) * softplus
        bg_ref[0] = jnp.where(lane < 8, 1.0 / (1.0 + jnp.exp(-ba)), g)
        yield
        gates_ref[0] = jnp.concatenate([_silu(pm[:, 768:1024]), _silu(pm[:, 1536:1792])], axis=1).astype(BF16)
        yield

    def mixer_c():
        rc, rs = rc_ref[...], rs_ref[...]
        first = (_iota((tm, 128), 1) & 16) == 0

        def rope(x):
            sw = jnp.where(first, pltpu.roll(x, 112, 1), pltpu.roll(x, 16, 1))
            return x * rc + sw * rs

        cq, ck = pm[:, 1024:1280], pm[:, 1280:1408]
        qn = cq * lax.rsqrt(_head_sum(cq * cq) * (1.0 / HEAD) + EPS) * qw_ref[...]
        yield
        qr = jnp.concatenate([rope(qn[:, 0:128]), rope(qn[:, 128:256])], axis=1) * (HEAD ** -0.5 * LOG2_E)
        qt = qr.T.astype(BF16)
        yield
        zq = jnp.zeros((HEAD, 2 * tq), BF16)
        for r in range(tm // tq):
            cs = slice(r * tq, (r + 1) * tq)
            top0 = jnp.concatenate([qt[0:64, cs], qt[64:128, cs]], axis=1)
            top1 = jnp.concatenate([qt[128:192, cs], qt[192:256, cs]], axis=1)
            aqt_ref[0, 0, r] = jnp.concatenate([top0, zq], axis=0)
            aqt_ref[0, 1, r] = jnp.concatenate([zq, top1], axis=0)
        yield
        kn = ck * lax.rsqrt(_head_sum(ck * ck) * (1.0 / HEAD) + EPS) * kw_ref[...]
        ak_ref[0] = rope(kn).astype(BF16)
        yield
        vt = pm[:, 1408:1536].T
        ones = jnp.ones((16, tm), F32)
        for g in range(2):
            vg = jnp.concatenate([vt[g * HEAD:(g + 1) * HEAD], ones], axis=0).astype(BF16)
            for r in range(tm // ATTN_TK):
                avt_ref[0, g, r] = vg[:, r * ATTN_TK:(r + 1) * ATTN_TK]
        yield

    streams = [mixer_b(), mixer_a(), mixer_c(), mixer_d()]
    while streams:
        for gen in list(streams):
            if next(gen, StopIteration) is StopIteration:
                streams.remove(gen)


def _proj_call(x, lw, rope_c, rope_s, tm, tq):
    b, s, _ = x.shape
    n_tiles = s // tm
    hb = tm // HALO
    last_hb = s // HALO - 1
    full = lambda shape: pl.BlockSpec(shape, lambda bi, i: (0,) * len(shape), pipeline_mode=pl.Buffered(1))
    tok = lambda w: pl.BlockSpec((1, tm, w), lambda bi, i: (bi, i, 0))
    in_specs = [
        pl.BlockSpec((1, HALO, D_MODEL), lambda bi, i: (bi, jnp.maximum(i * hb - 1, 0), 0)),
        pl.BlockSpec((1, tm, D_MODEL), lambda bi, i: (bi, i, 0)),
        pl.BlockSpec((1, HALO, D_MODEL), lambda bi, i: (bi, jnp.minimum((i + 1) * hb, last_hb), 0)),
        full((1, D_MODEL)), full((D_MODEL, 1024)), full((D_MODEL, MAIN_COLS)),
        full((128, 512)), full((128, 256)), full((8, 768)), full((8, 128)),
        full((1, 256)), full((1, 128)),
        pl.BlockSpec((tm, 128), lambda bi, i: (i, 0)), pl.BlockSpec((tm, 128), lambda bi, i: (i, 0)),
        full((256, 256)), full((1, 256)),
    ]
    out_shape = [
        jax.ShapeDtypeStruct((b, s, 256), BF16),
        jax.ShapeDtypeStruct((b, s, 256), BF16),
        jax.ShapeDtypeStruct((b, s, 256), BF16),
        jax.ShapeDtypeStruct((b, s, 256), BF16),
        jax.ShapeDtypeStruct((b, s, 256), BF16),
        jax.ShapeDtypeStruct((b, s, 128), F32),
        jax.ShapeDtypeStruct((b, s, 512), BF16),
        jax.ShapeDtypeStruct((b, 2, s // tq, 128, 2 * tq), BF16),
        jax.ShapeDtypeStruct((b, s, 128), BF16),
        jax.ShapeDtypeStruct((b, 2, s // ATTN_TK, HEAD + 16, ATTN_TK), BF16),
    ]
    out_specs = [tok(256), tok(256), tok(256), tok(256), tok(256), tok(128), tok(512),
                 pl.BlockSpec((1, 2, tm // tq, 128, 2 * tq), lambda bi, i: (bi, 0, i, 0, 0)), tok(128),
                 pl.BlockSpec((1, 2, tm // ATTN_TK, HEAD + 16, ATTN_TK), lambda bi, i: (bi, 0, i, 0, 0))]
    return pl.pallas_call(
        functools.partial(_proj_kernel, tm=tm, tq=tq, n_tiles=n_tiles, seq_len=s),
        grid=(b, n_tiles), in_specs=in_specs, out_specs=out_specs, out_shape=out_shape,
        scratch_shapes=[pltpu.VMEM((tm + 2 * HALO, D_MODEL), BF16)],
        compiler_params=_cparams(("parallel", "parallel")), name="proj",
    )(x, x, x, lw["norm_w"], lw["w_halo"], lw["w_main"], lw["sgu_w"], lw["sgu_b"], lw["conv_w"],
      lw["dn_row"], lw["q_norm_w"], lw["k_norm_w"], rope_c, rope_s, lw["pool_w"], lw["pool_scale"])


def _dn_kernel(*refs, nb, tb):
    ins, (of_ref, ob_ref) = refs[:24], refs[24:26]
    (state_ref, expand_ref, ones_ref, bd_ref,
     a_sc, vb_sc, kbg_sc, in_sc, qd_sc, kdt_sc, egl_sc) = refs[26:]
    nc = tb // DN_CHUNK
    step = pl.program_id(0)

    li = _iota((tb, 256), 0) & 63
    lj = _iota((tb, 256), 1) & 63
    eyecat = li == lj
    eyef = jnp.where(eyecat, 1.0, 0.0)
    bdm = _bd_mask()
    lane = _iota((tb, 128), 1)
    row64 = _iota((tb, 128), 0) & 63
    zpad = jnp.zeros((DN_CHUNK, 256), BF16)

    def prepare_block_set(which, slot):
        j2 = ones_ref[...]
        bdm01 = bd_ref[...]
        blocks = []
        for d in range(2):
            q_ref, k_ref, v_ref, bg_ref = ins[(which * 2 + d) * 4:(which * 2 + d) * 4 + 4]
            for bi in range(nb):
                blocks.append(dict(d=d, bi=bi, idx=d * nb + bi, q_ref=q_ref, k_ref=k_ref, v_ref=v_ref,
                                   bg=bg_ref[bi], expand=expand_ref[d],
                                   incl=(li >= lj) if d == 0 else (li <= lj),
                                   strict=(li > lj) if d == 0 else (li < lj)))
        for blk in blocks:
            c = blk["bg"]
            for s in (1, 2, 4, 8, 16, 32):
                c = c + jnp.where(row64 >= s, pltpu.roll(c, s, 0), 0.0)
            tot = jnp.concatenate([jnp.broadcast_to(c[ci * DN_CHUNK + DN_CHUNK - 1:(ci + 1) * DN_CHUNK, :],
                                                    (DN_CHUNK, 128)) for ci in range(nc)], axis=0)
            if blk["d"] == 0:
                blk["cs"] = (c, tot - c)
            else:
                blk["cs"] = (tot - c + blk["bg"], c - blk["bg"])
            yield
        for blk in blocks:
            cs_in, cs_st = blk["cs"]
            nbx = jnp.where(lane < 8, blk["bg"], jnp.where(lane < 16, cs_in, pltpu.roll(cs_st, 8, 1)))
            hi = nbx.astype(BF16)
            lo = pltpu.roll(nbx - hi.astype(F32), 64, 1)
            packed = jnp.where(lane < 64, hi.astype(F32), lo).astype(BF16)
            blk["x"] = jnp.dot(packed, blk["expand"], preferred_element_type=F32)
            yield
        for blk in blocks:
            gcx = blk["x"][:, 256:512]
            blk["rowf"] = jnp.dot(j2, jnp.concatenate(_split2(jnp.where(eyecat, gcx, 0.0)), axis=0),
                                  preferred_element_type=F32)
            yield
        for blk in blocks:
            gram, qk = [], []
            for ci in range(nc):
                sl = slice(ci * DN_CHUNK, (ci + 1) * DN_CHUNK)
                kc = blk["k_ref"][blk["bi"], sl, :]
                gq = lax.dot_general(jnp.concatenate([kc, blk["q_ref"][blk["bi"], sl, :]], axis=0),
                                     _bd(kc, bdm01), (((1,), (1,)), ((), ())),
                                     preferred_element_type=F32)
                gram.append(gq[0:64])
                qk.append(gq[64:128])
            blk["gram"] = jnp.concatenate(gram, axis=0)
            blk["qk"] = jnp.concatenate(qk, axis=0)
            yield
        for blk in blocks:
            d, bi, idx = blk["d"], blk["bi"], blk["idx"]
            q = blk["q_ref"][bi].astype(F32)
            k = blk["k_ref"][bi].astype(F32)
            v = blk["v_ref"][bi].astype(F32)
            x = blk["x"]
            bx, gcx, dglx = x[:, 0:256], x[:, 256:512], x[:, 512:768]
            decay = jnp.exp(jnp.where(blk["incl"], gcx - blk["rowf"], NEG_BIG))
            eg = jnp.exp(gcx)
            a_sc[slot, idx] = jnp.where(blk["strict"], bx * blk["gram"] * decay, 0.0)
            in_sc[slot, idx] = jnp.where(blk["incl"], blk["qk"] * decay, 0.0).astype(BF16)
            qd_sc[slot, idx] = (q * eg).astype(BF16)
            kdt_sc[slot, idx] = (k * jnp.exp(dglx)).T.astype(BF16)
            vb_sc[slot, idx] = (v * bx).astype(BF16)
            kbg_sc[slot, idx] = (k * bx * eg).astype(BF16)
            for ci in range(nc):
                last = ci * DN_CHUNK + (DN_CHUNK - 1 if d == 0 else 0)
                egl_sc[slot, idx, ci * 8:(ci + 1) * 8, :] = jnp.broadcast_to(eg[last:last + 1, :], (8, 256))
            yield

    def matmul_part(slot, half):
        bdm01 = bd_ref[...]

        def prepare(group, phase):
            for ch in group:
                idx, sl = ch["idx"], ch["sl"]
                if phase == 0:
                    ch["ak"] = a_sc[slot, idx, sl, :]
                    ch["p"] = eyef[sl] - ch["ak"]
                if phase < 5:
                    akb = ch["ak"].astype(BF16)
                    lhs = akb if phase == 0 else jnp.concatenate([akb, ch["p"].astype(BF16)], axis=0)
                    res = jnp.dot(lhs, _bd(akb, bdm01), preferred_element_type=F32)
                    ch["ak"] = res[0:64]
                    if phase > 0:
                        ch["p"] = ch["p"] + res[64:128]
                elif phase == 5:
                    ch["p"] = ch["p"] + jnp.dot(ch["p"].astype(BF16), _bd(ch["ak"].astype(BF16), bdm01),
                                                preferred_element_type=F32)
                else:
                    rhs = jnp.concatenate([_bd(vb_sc[slot, idx, sl, :], bdm01),
                                           _bd(kbg_sc[slot, idx, sl, :], bdm01)], axis=1)
                    uw = jnp.dot(ch["p"].astype(BF16), rhs, preferred_element_type=F32)
                    ch["u"] = uw[:, 0:256]
                    ch["w"] = uw[:, 256:512].astype(BF16)

        def recur_a(group):
            for ch in group:
                ch["state"] = state_ref[ch["si"]]
                res = jnp.dot(jnp.concatenate([ch["w"], qd_sc[slot, ch["idx"], ch["sl"], :]], axis=0),
                              ch["state"].astype(BF16), preferred_element_type=F32)
                ch["v_new"] = (ch["u"] - res[0:64]).astype(BF16)
                ch["o_inter"] = res[64:128]

        def recur_b(group):
            for ch in group:
                v_new, idx, ci = ch["v_new"], ch["idx"], ch["ci"]
                ch["o_ref"][ch["bi"], ch["osl"], :] = (ch["o_inter"] + jnp.dot(
                    in_sc[slot, idx, ch["sl"], :], _bd(v_new, bdm01), preferred_element_type=F32)).astype(BF16)
                vpad = jnp.concatenate([v_new, zpad] if ci % 2 == 0 else [zpad, v_new], axis=0)
                kpair = kdt_sc[slot, idx, :, (ci // 2) * 128:(ci // 2 + 1) * 128]
                ds = jnp.dot(kpair, vpad, preferred_element_type=F32)
                egl = egl_sc[slot, idx, ci * 8:ci * 8 + 1, :]
                state_ref[ch["si"]] = ch["state"] * egl + jnp.where(bdm, ds, 0.0)

        groups = []
        for s in range(nc):
            group = []
            for bi in range(nb):
                for d in range(2):
                    ci = s if d == 0 else nc - 1 - s
                    row0 = (half if d == 0 else 1 - half) * tb + ci * DN_CHUNK
                    group.append(dict(idx=d * nb + bi, si=d * nb + bi, bi=bi, ci=ci,
                                      o_ref=(of_ref, ob_ref)[d],
                                      sl=slice(ci * DN_CHUNK, (ci + 1) * DN_CHUNK),
                                      osl=slice(row0, row0 + DN_CHUNK)))
            groups.append(group)
        for phase in range(7):
            prepare(groups[0], phase)
            yield
        for s in range(nc):
            nxt = groups[s + 1] if s + 1 < nc else []
            for piece in ((prepare, nxt, 0), (prepare, nxt, 1), (recur_a, groups[s]), (prepare, nxt, 2),
                          (prepare, nxt, 3), (recur_b, groups[s]), (prepare, nxt, 4), (prepare, nxt, 5),
                          (prepare, nxt, 6)):
                piece[0](*piece[1:])
                yield

    def interleave(first, second):
        streams = [first, second]
        while streams:
            for gen in list(streams):
                if next(gen, StopIteration) is StopIteration:
                    streams.remove(gen)

    @pl.when(step == 0)
    def _():
        state_ref[...] = jnp.zeros_like(state_ref)
        r = _iota((tb, tb), 0)
        c = _iota((tb, tb), 1)
        same = jnp.right_shift(r, 6) == jnp.right_shift(c, 6)
        src = _iota((128, 768), 0)
        col = _iota((128, 768), 1)
        for d in range(2):
            expand_ref[d] = _ones_where((src & 63) == jnp.left_shift(jnp.right_shift(col, 8), 3) + 4 * d
                                        + (jnp.right_shift(col, 6) & 3))
        ones_ref[...] = jnp.concatenate([_ones_where(same), _ones_where(same)], axis=1)
        bd_ref[...] = _ones_where(_bd_mask())
        for _ in prepare_block_set(0, 0):
            pass

    interleave(matmul_part(0, 0), prepare_block_set(1, 1))
    interleave(matmul_part(1, 1), prepare_block_set(2, 0))


def _dn_call(dq, dk, dv, bg, tb):
    b, s, _ = dq.shape
    n = s // tb
    n2 = n // 2

    def spec(w, index):
        return pl.BlockSpec((b, tb, w), lambda i: (0, index(i), 0))

    orders = [lambda j: j, lambda j: n - 1 - j]
    picks = [lambda i: 0, lambda i: 2 * i + 1, lambda i: jnp.minimum(2 * i + 2, n - 1)]
    in_specs, args = [], []
    for pick in picks:
        for order in orders:
            index = functools.partial(lambda i, pick, order: order(pick(i)), pick=pick, order=order)
            in_specs += [spec(256, index), spec(256, index), spec(256, index), spec(128, index)]
            args += [dq, dk, dv, bg]
    nblk = 2 * b
    return pl.pallas_call(
        functools.partial(_dn_kernel, nb=b, tb=tb),
        grid=(n2,), in_specs=in_specs,
        out_specs=[pl.BlockSpec((b, 2 * tb, 256), lambda i: (0, i, 0)),
                   pl.BlockSpec((b, 2 * tb, 256), lambda i: (0, n2 - 1 - i, 0))],
        out_shape=[jax.ShapeDtypeStruct((b, s, 256), BF16)] * 2,
        scratch_shapes=[pltpu.VMEM((nblk, 256, 256), F32), pltpu.VMEM((2, 128, 768), BF16),
                        pltpu.VMEM((tb, 2 * tb), BF16), pltpu.VMEM((256, 256), BF16),
                        pltpu.VMEM((2, nblk, tb, 256), F32), pltpu.VMEM((2, nblk, tb, 256), BF16),
                        pltpu.VMEM((2, nblk, tb, 256), BF16), pltpu.VMEM((2, nblk, tb, 256), BF16),
                        pltpu.VMEM((2, nblk, tb, 256), BF16), pltpu.VMEM((2, nblk, 256, tb), BF16),
                        pltpu.VMEM((2, nblk, tb // 8, 256), F32)],
        compiler_params=_cparams(("arbitrary",)), name="dn",
    )(*args)


def _attn_kernel(qt_ref, k_ref, vt_ref, o_ref, st_ref, *, tq, tk, n_q, n_kv, unroll, bounded):
    per_q = n_kv // unroll
    n_trips = n_q * per_q

    def scores(qi, j, slot):
        kj = k_ref[0, pl.ds(pl.multiple_of(j * tk, tk), tk), :]
        st = jnp.dot(kj, qt_ref[0, 0, qi], preferred_element_type=F32)
        st_ref[slot, :, 0:2 * tq] = st
        return None if bounded else jnp.max(st, axis=0, keepdims=True)

    def update(j, st, mx, m, acc):
        if bounded:
            return m, acc + jnp.dot(vt_ref[0, 0, j], jnp.exp2(st).astype(BF16), preferred_element_type=F32)
        m_new = jnp.maximum(m, mx)
        p = jnp.exp2(st - m_new).astype(BF16)
        alpha = jnp.exp2(m - m_new)
        return m_new, alpha * acc + jnp.dot(vt_ref[0, 0, j], p, preferred_element_type=F32)

    def body(t, carry):
        qi = t // per_q
        base = (t - qi * per_q) * unroll
        first = base == 0
        m = jnp.where(first, -jnp.inf, carry[0])
        acc = jnp.where(first, 0.0, carry[1])
        mx = carry[2]
        t_next = jnp.minimum(t + 1, n_trips - 1)
        qi_next = t_next // per_q
        base_next = (t_next - qi_next * per_q) * unroll
        for r in range(unroll):
            if r < unroll - 1:
                mx_next = scores(qi, base + r + 1, (r + 1) % 2)
            else:
                mx_next = scores(qi_next, base_next, 0)
            m, acc = update(base + r, st_ref[r % 2, :, 0:2 * tq], mx, m, acc)
            mx = mx if bounded else mx_next

        @pl.when(base == n_kv - unroll)
        def _():
            o = acc[0:HEAD] / acc[HEAD:HEAD + 1]
            ot = jnp.concatenate([o[:, 0:tq], o[:, tq:2 * tq]], axis=0)
            o_ref[0, pl.ds(pl.multiple_of(qi * tq, tq), tq), :] = ot.T.astype(BF16)

        return m, acc, mx

    m0 = jnp.full((1, 2 * tq), -jnp.inf, F32)
    a0 = jnp.zeros((HEAD + 16, 2 * tq), F32)
    mx0 = scores(0, 0, 0)
    lax.fori_loop(0, n_trips, body, (m0, a0, m0 if bounded else mx0))


def _attn_call(aqt, ak, avt, bounded):
    b, s, _ = ak.shape
    _, _, n_q, _, tq2 = aqt.shape
    _, _, n_kv, vrows, tk = avt.shape
    unroll = next(u for u in (32, 16, 8, 4, 2) if n_kv % u == 0)
    return pl.pallas_call(
        functools.partial(_attn_kernel, tq=tq2 // 2, tk=tk, n_q=n_q, n_kv=n_kv, unroll=unroll,
                          bounded=bounded),
        grid=(b, 2),
        in_specs=[pl.BlockSpec((1, 1, n_q, 128, tq2), lambda bi, g: (bi, g, 0, 0, 0)),
                  pl.BlockSpec((1, s, 128), lambda bi, g: (bi, 0, 0)),
                  pl.BlockSpec((1, 1, n_kv, vrows, tk), lambda bi, g: (bi, g, 0, 0, 0))],
        out_specs=pl.BlockSpec((1, s, 128), lambda bi, g: (bi, 0, g)),
        out_shape=jax.ShapeDtypeStruct((b, s, 256), BF16),
        scratch_shapes=[pltpu.VMEM((2, tk, tq2 + 128), F32)],
        compiler_params=_cparams(("parallel", "parallel")), name="attn",
    )(aqt, ak, avt)


def _out_kernel(x_ref, ya_ref, yd_ref, yc_ref, of_ref, ob_ref, gates_ref, dnw_ref, wo_ref, o_ref):
    o = of_ref[0].astype(F32) + ob_ref[0].astype(F32)
    gates = gates_ref[0].astype(F32)
    on = o * lax.rsqrt(_head_sum(o * o) * (1.0 / HEAD) + EPS) * dnw_ref[...]
    yb = (on * gates[:, 0:256]).astype(BF16)
    yc = (yc_ref[0].astype(F32) * gates[:, 256:512]).astype(BF16)
    mix = jnp.concatenate([ya_ref[0], yb, yc, yd_ref[0]], axis=1)
    o_ref[0] = x_ref[0] + jnp.dot(mix, wo_ref[...], preferred_element_type=F32)


def _out_call(x, ya, yd, yc, o_f, o_b, gates, lw, tm):
    b, s, _ = x.shape
    tok = lambda w: pl.BlockSpec((1, tm, w), lambda bi, i: (bi, i, 0))
    full = lambda shape: pl.BlockSpec(shape, lambda bi, i: (0,) * len(shape), pipeline_mode=pl.Buffered(1))
    return pl.pallas_call(
        _out_kernel, grid=(b, s // tm),
        in_specs=[tok(D_MODEL), tok(256), tok(256), tok(256), tok(256), tok(256), tok(512),
                  full((1, 256)), full((D_MODEL, D_MODEL))],
        out_specs=tok(D_MODEL), out_shape=jax.ShapeDtypeStruct((b, s, D_MODEL), F32),
        compiler_params=_cparams(("parallel", "parallel")), name="out",
    )(x, ya, yd, yc, o_f, o_b, gates, lw["dn_norm_w"], lw["w_out"])


def _rope_tables(seq_len):
    t = jnp.arange(seq_len)
    pos = jnp.stack([t // GRID_W, t % GRID_W], axis=-1).astype(F32)
    n_freq = HEAD // 4
    inv_freq = jnp.power(ROPE_THETA, -2.0 * jnp.arange(n_freq, dtype=F32) / (HEAD // 2))
    ang = pos[:, :, None] * inv_freq
    cos = jnp.repeat(jnp.cos(ang)[:, :, None, :], 2, axis=2).reshape(seq_len, HEAD)
    sin = jnp.sin(ang)
    sin = jnp.stack([-sin, sin], axis=2).reshape(seq_len, HEAD)
    return jnp.tile(cos, (1, 2)), jnp.tile(sin, (1, 2))


def _layer_weights(l, norm_w, w_in, sgu_w, sgu_b, conv_w, a_log, dt_bias, dn_norm_w, q_norm_w,
                   k_norm_w, pool_w, pool_scale, w_out):
    w = w_in[l]
    cols = lambda a, n: w[:, a:a + n]
    w_halo = jnp.concatenate([cols(_B_Q, 768), cols(_D_X, 256)], axis=1).astype(BF16)
    w_main = jnp.concatenate([
        cols(_A_U, 768), cols(_B_Z, 256), cols(_C_Q, 768), cols(_D_Z, 256), cols(_B_BETA, 16),
        jnp.zeros((D_MODEL, 112), F32)], axis=1).astype(BF16)
    dn_row = jnp.zeros((8, 128), F32)
    dn_row = dn_row.at[0, 8:16].set(a_log[l].reshape(8)).at[1, 8:16].set(dt_bias[l].reshape(8))
    pool_bd = jnp.zeros((256, 256), F32)
    for gi in range(len(POOL_WINDOWS)):
        pool_bd = pool_bd.at[gi * 64:(gi + 1) * 64, gi * 64:(gi + 1) * 64].set(pool_w[l, gi])
    return {
        "norm_w": norm_w[l].reshape(1, D_MODEL),
        "w_halo": w_halo, "w_main": w_main,
        "sgu_w": jnp.transpose(sgu_w[l], (1, 0, 2)).reshape(SGU_CHUNK, 4 * SGU_CHUNK).astype(BF16),
        "sgu_b": jnp.repeat(sgu_b[l].T, HEAD, axis=1),
        "conv_w": jnp.concatenate([conv_w[l], jnp.zeros((3, 768), F32)], axis=0),
        "dn_row": dn_row,
        "q_norm_w": jnp.tile(q_norm_w[l], 4).reshape(1, 256),
        "k_norm_w": jnp.tile(k_norm_w[l], 2).reshape(1, 128),
        "pool_w": pool_bd.astype(BF16),
        "pool_scale": pool_scale[l].reshape(1, 256),
        "dn_norm_w": jnp.tile(dn_norm_w[l], 4).reshape(1, 256),
        "score_bound": (HEAD * jnp.max(jnp.abs(q_norm_w[l])) * jnp.max(jnp.abs(k_norm_w[l]))
                        * (HEAD ** -0.5 * LOG2_E)),
        "w_out": w_out[l].astype(BF16),
    }


def _tiles(batch, seq_len):
    tb = min(DN_CHUNK * max(1, 8 // batch), seq_len)
    return dict(tm=min(512, seq_len), to=min(1024, seq_len), tb=tb, tq=min(256, seq_len))


def _layer(x, lw, rope_c, rope_s):
    t = _tiles(x.shape[0], x.shape[1])
    ya, yd, dq, dk, dv, bg, gates, aqt, ak, avt = _proj_call(x, lw, rope_c, rope_s, t["tm"], t["tq"])
    o_f, o_b = _dn_call(dq, dk, dv, bg, t["tb"])
    yc = lax.cond(lw["score_bound"] * SCORE_MARGIN <= SCORE_BOUND,
                  functools.partial(_attn_call, bounded=True),
                  functools.partial(_attn_call, bounded=False), aqt, ak, avt)
    return _out_call(x, ya, yd, yc, o_f, o_b, gates, lw, t["to"])


def kernel(x_prompt, x_sample, norm_w, w_in, sgu_w, sgu_b, conv_w, a_log, dt_bias, dn_norm_w,
           q_norm_w, k_norm_w, pool_w, pool_scale, w_out):
    depth = norm_w.shape[0]
    rope_p = _rope_tables(x_prompt.shape[1])
    rope_s = _rope_tables(x_sample.shape[1])
    y_prompt, y_sample = x_prompt, x_sample
    for l in range(depth):
        lw = _layer_weights(l, norm_w, w_in, sgu_w, sgu_b, conv_w, a_log, dt_bias, dn_norm_w,
                            q_norm_w, k_norm_w, pool_w, pool_scale, w_out)
        y_prompt = _layer(y_prompt, lw, *rope_p)
        y_sample = _layer(y_sample, lw, *rope_s)
    return (y_prompt, y_sample)
```
